```python
import math
import jax, jax.numpy as jnp
from jax import lax
import numpy as np

D_MODEL = 1024
BATCH = 8
SEQ = 2048
DEPTH = 2
DEC_BATCH = 32
DEC_SEQ = 4
PAST_LEN = 8192
PAGE_SIZE = 128

N_LAYERS_A = (DEPTH + 1) // 2
N_LAYERS_B = DEPTH // 2
NORM_EPS = 1e-6
ROPE_THETA = 500000.0
ROPE_FRACTION = 4
Q_BLOCK = 128
NEG_INF = -1e30

DA_HEADS = 8
DA_HEAD_DIM = 64
DA_IN = 3 * DA_HEADS * 2 * DA_HEAD_DIM

NSA_HEADS = 16
NSA_KV_HEADS = 4
NSA_GROUP = NSA_HEADS // NSA_KV_HEADS
NSA_HEAD_DIM = 64
NSA_BLOCK = 64
NSA_TOPN = 16
NSA_WINDOW = 512
NSA_CMP_HIDDEN = 64
NSA_FORCE_SCORE = 1e4
NSA_Q = NSA_HEADS * NSA_HEAD_DIM
NSA_KV = 2 * NSA_KV_HEADS * NSA_HEAD_DIM
NSA_IN = NSA_Q + 3 * NSA_KV + 3 * NSA_HEADS
NSA_SPLITS = (NSA_Q, NSA_Q + NSA_KV, NSA_Q + 2 * NSA_KV, NSA_Q + 3 * NSA_KV)

PEER_HEADS = 8
PEER_N_KEYS = 128
PEER_N_EXPERTS = PEER_N_KEYS * PEER_N_KEYS
PEER_TOPK = 16
PEER_KEY_DIM = 256
PEER_TOKEN_BLOCK = 128

kernel_name = "diffattn_nsa_peer_hybrid_step"


def rmsnorm(x, g):
    xf = x.astype(jnp.float32)
    y = xf * lax.rsqrt(jnp.mean(xf * xf, axis=-1, keepdims=True) + NORM_EPS)
    return (y * g.astype(jnp.float32)).astype(x.dtype)


def rope_partial(x, pos):
    d_rot = x.shape[-1] // ROPE_FRACTION
    half = d_rot // 2
    inv_freq = ROPE_THETA ** (-jnp.arange(half, dtype=jnp.float32) / half)
    ang = pos.astype(jnp.float32)[:, None] * inv_freq[None, :]
    cos = jnp.cos(ang)[:, None, :].astype(x.dtype)
    sin = jnp.sin(ang)[:, None, :].astype(x.dtype)
    x1, x2, rest = x[..., :half], x[..., half:d_rot], x[..., d_rot:]
    return jnp.concatenate([x1 * cos - x2 * sin, x2 * cos + x1 * sin, rest], axis=-1)


def masked_softmax(s, mask):
    s = jnp.where(mask, s, NEG_INF)
    m = jnp.max(s, axis=-1, keepdims=True)
    e = jnp.where(mask, jnp.exp(s - m), 0.0)
    return e / jnp.maximum(jnp.sum(e, axis=-1, keepdims=True), 1e-30)


def diff_attn_core(q, k, v, q_pos, k_pos, lam):
    s = jnp.einsum('...qhcd,...khcd->...chqk', q, k,
                   preferred_element_type=jnp.float32) * (DA_HEAD_DIM ** -0.5)
    mask = k_pos[None, :] <= q_pos[:, None]
    p = masked_softmax(s, mask)
    a = p[..., 0, :, :, :] - lam * p[..., 1, :, :, :]
    return jnp.einsum('...hqk,...khe->...qhe', a.astype(v.dtype), v)


def diff_layer(xn_p, xn_s, cache, layer, depth_idx, page_table, w_in, lam_vec, subln_g, w_out):
    lam_init = 0.8 - 0.6 * math.exp(-0.3 * depth_idx)
    lv = lam_vec.astype(jnp.float32)
    lam = jnp.exp(jnp.sum(lv[0] * lv[1])) - jnp.exp(jnp.sum(lv[2] * lv[3])) + lam_init

    def project(xn, pos):
        n, t = xn.shape[:2]
        q, k, v = jnp.split(xn @ w_in, 3, axis=-1)
        q = rope_partial(q.reshape(n, t, 2 * DA_HEADS, DA_HEAD_DIM), pos).reshape(n, t, DA_HEADS, 2, DA_HEAD_DIM)
        k = rope_partial(k.reshape(n, t, 2 * DA_HEADS, DA_HEAD_DIM), pos).reshape(n, t, DA_HEADS, 2, DA_HEAD_DIM)
        return q, k, v.reshape(n, t, DA_HEADS, 2 * DA_HEAD_DIM)

    def finish(o):
        n, t = o.shape[:2]
        o = rmsnorm(o, subln_g) * (1.0 - lam_init)
        return o.reshape(n, t, -1) @ w_out

    n, t = xn_p.shape[:2]
    pos_p = jnp.arange(t)
    q, k, v = project(xn_p, pos_p)
    n_qb = t // Q_BLOCK
    qb = q.reshape(n, n_qb, Q_BLOCK, DA_HEADS, 2, DA_HEAD_DIM).swapaxes(0, 1)
    o = lax.map(lambda a: diff_attn_core(a[0], k, v, a[1], pos_p, lam),
                (qb, pos_p.reshape(n_qb, Q_BLOCK)))
    out_p = finish(o.swapaxes(0, 1).reshape(n, t, DA_HEADS, 2 * DA_HEAD_DIM))
    kv_p = jnp.stack([k.reshape(n, t, DA_HEADS, 2 * DA_HEAD_DIM), v], axis=2)

    ns, ts = xn_s.shape[:2]
    pos_s = PAST_LEN + jnp.arange(ts)
    qs, ks, vs = project(xn_s, pos_s)
    k_pos = jnp.arange(PAST_LEN + ts)

    def one_seq(a):
        q1, k1, v1, pt = a
        past = cache[layer, pt].reshape(-1, 2, DA_HEADS, 2 * DA_HEAD_DIM)
        k_all = jnp.concatenate([past[:, 0].reshape(-1, DA_HEADS, 2, DA_HEAD_DIM), k1], axis=0)
        v_all = jnp.concatenate([past[:, 1], v1], axis=0)
        return diff_attn_core(q1, k_all, v_all, pos_s, k_pos, lam)

    out_s = finish(lax.map(one_seq, (qs, ks, vs, page_table)))
    kv_s = jnp.stack([ks.reshape(ns, ts, DA_HEADS, 2 * DA_HEAD_DIM), vs], axis=2)
    return out_p, out_s, kv_p, kv_s


def nsa_project(xn, pos, w_in):
    n, t = xn.shape[:2]
    q, kv_c, kv_s, kv_w, gl = jnp.split(xn @ w_in, NSA_SPLITS, axis=-1)
    q = q.reshape(n, t, NSA_HEADS, NSA_HEAD_DIM)
    grp = (n, t, NSA_KV_HEADS, NSA_GROUP, NSA_HEAD_DIM)
    kv_shape = (n, t, 2, NSA_KV_HEADS, NSA_HEAD_DIM)

    def rot_k(kv):
        kv = kv.reshape(kv_shape)
        return jnp.stack([rope_partial(kv[:, :, 0], pos), kv[:, :, 1]], axis=2)

    gates = jax.nn.sigmoid(gl.astype(jnp.float32)).reshape(n, t, NSA_KV_HEADS, NSA_GROUP, 3)
    return (q.reshape(grp), rope_partial(q, pos).reshape(grp), kv_c.reshape(kv_shape),
            rot_k(kv_s), rot_k(kv_w), gates)


def nsa_compress(kv, pos_emb, w1, b1, w2, b2):
    n, length = kv.shape[:2]
    nc = length // NSA_BLOCK
    blk = kv[:, :nc * NSA_BLOCK].reshape(n, nc, NSA_BLOCK, 2, NSA_KV_HEADS, NSA_HEAD_DIM)
    blk = blk + pos_emb[:, :, None, :]
    hid = jax.nn.gelu(jnp.einsum('nbrckd,crde->nbcke', blk, w1) + b1[:, None, :], approximate=False)
    return jnp.einsum('nbcke,ced->nbckd', hid, w2) + b2[:, None, :]


def nsa_cmp_attn(q, kc, q_pos):
    nc = kc.shape[1]
    s = jnp.einsum('nqkgd,nckd->nqkgc', q, kc[:, :, 0],
                   preferred_element_type=jnp.float32) * (NSA_HEAD_DIM ** -0.5)
    vis = ((jnp.arange(nc) + 1) * NSA_BLOCK - 1)[None, :] <= q_pos[:, None]
    p = masked_softmax(s, vis[None, :, None, None, :])
    o = jnp.einsum('nqkgc,nckd->nqkgd', p.astype(q.dtype), kc[:, :, 1])
    return o, jnp.sum(p, axis=3)


def nsa_select(imp, q_pos, n_blocks):
    nc = imp.shape[-1]
    imp = jnp.pad(imp, ((0, 0), (0, 0), (0, 0), (0, n_blocks - nc)))
    j = jnp.arange(n_blocks)[None, :]
    cur = (q_pos // NSA_BLOCK)[:, None]
    forced = (j == 0) | (j == cur) | (j == cur - 1)
    score = jnp.where(forced[None, :, None, :], NSA_FORCE_SCORE, imp)
    score = jnp.where((j <= cur)[None, :, None, :], score, NEG_INF)
    val, idx = lax.top_k(score, min(NSA_TOPN, n_blocks))
    return idx, val > 0.5 * NEG_INF


def nsa_sel_core(q, ks, vs, idx, ok, q_pos):
    kpos = idx[..., None] * NSA_BLOCK + jnp.arange(NSA_BLOCK)
    mask = ok[..., None] & (kpos <= q_pos[:, None, None, None])
    s = jnp.einsum('...qkgd,...qksrd->...qkgsr', q, ks,
                   preferred_element_type=jnp.float32) * (NSA_HEAD_DIM ** -0.5)
    sh = s.shape
    p = masked_softmax(s.reshape(sh[:-2] + (-1,)), mask.reshape(mask.shape[:-2] + (-1,))[..., None, :])
    return jnp.einsum('...qkgsr,...qksrd->...qkgd', p.reshape(sh).astype(vs.dtype), vs)


def nsa_win_core(q, kv, q_pos, k_pos):
    s = jnp.einsum('nqkgd,nskd->nqkgs', q, kv[:, :, 0],
                   preferred_element_type=jnp.float32) * (NSA_HEAD_DIM ** -0.5)
    dist = q_pos[:, None] - k_pos[None, :]
    mask = (dist >= 0) & (dist < NSA_WINDOW) & (k_pos >= 0)[None, :]
    p = masked_softmax(s, mask[None, :, None, None, :])
    return jnp.einsum('nqkgs,nskd->nqkgd', p.astype(q.dtype), kv[:, :, 1])


def nsa_combine(gates, o_c, o_s, o_w, w_out):
    g = gates.astype(o_c.dtype)
    o = g[..., 0:1] * o_c + g[..., 1:2] * o_s + g[..., 2:3] * o_w
    n, t = o.shape[:2]
    return o.reshape(n, t, -1) @ w_out


def nsa_layer(xn_p, xn_s, cmp_cache, sel_cache, win_state, layer, page_table,
              w_in, cmp_pos, cmp_w1, cmp_b1, cmp_w2, cmp_b2, w_out):
    kvh, grp, dh, blk = NSA_KV_HEADS, NSA_GROUP, NSA_HEAD_DIM, NSA_BLOCK

    def compress(kv):
        return nsa_compress(kv, cmp_pos, cmp_w1, cmp_b1, cmp_w2, cmp_b2)

    n, t = xn_p.shape[:2]
    pos = jnp.arange(t)
    q_pl, q_rt, kv_c, kv_s, kv_w, gates = nsa_project(xn_p, pos, w_in)
    o_c, imp = nsa_cmp_attn(q_pl, compress(kv_c), pos)
    nb = -(-t // blk)
    idx, ok = nsa_select(imp, pos, nb)
    nsel = idx.shape[-1]
    blocks = jnp.pad(kv_s, ((0, 0), (0, nb * blk - t), (0, 0), (0, 0), (0, 0)))
    blocks = blocks.reshape(n, nb, blk, 2, kvh, dh).transpose(0, 4, 1, 2, 3, 5)
    head_i = jnp.arange(kvh)[None, :, None]

    def sel_block(a):
        qb, ib, okb, pb, bi = a
        g = blocks[bi][head_i, ib]
        return nsa_sel_core(qb, g[..., 0, :], g[..., 1, :], ib, okb, pb)

    n_qb = t // Q_BLOCK
    m = n * n_qb
    xs = (q_rt.reshape(m, Q_BLOCK, kvh, grp, dh),
          idx.reshape(m, Q_BLOCK, kvh, nsel),
          ok.reshape(m, Q_BLOCK, kvh, nsel),
          jnp.broadcast_to(pos.reshape(1, n_qb, Q_BLOCK), (n, n_qb, Q_BLOCK)).reshape(m, Q_BLOCK),
          jnp.repeat(jnp.arange(n), n_qb))
    o_s = lax.map(sel_block, xs).reshape(n, t, kvh, grp, dh)

    padded = jnp.pad(kv_w, ((0, 0), (NSA_WINDOW, 0), (0, 0), (0, 0), (0, 0)))
    span = NSA_WINDOW + Q_BLOCK

    def win_block(a):
        qb, s0 = a
        kvb = lax.dynamic_slice_in_dim(padded, s0, span, axis=1)
        return nsa_win_core(qb, kvb, s0 + jnp.arange(Q_BLOCK), s0 - NSA_WINDOW + jnp.arange(span))

    o_w = lax.map(win_block, (q_rt.reshape(n, n_qb, Q_BLOCK, kvh, grp, dh).swapaxes(0, 1),
                              jnp.arange(n_qb) * Q_BLOCK))
    o_w = o_w.swapaxes(0, 1).reshape(n, t, kvh, grp, dh)
    out_p = nsa_combine(gates, o_c, o_s, o_w, w_out)
    win_p = kv_w[:, t - min(NSA_WINDOW, t):]

    ns, ts = xn_s.shape[:2]
    pos_s = PAST_LEN + jnp.arange(ts)
    q_pl_s, q_rt_s, kv_c_s, kv_s_s, kv_w_s, gates_s = nsa_project(xn_s, pos_s, w_in)
    past_c = cmp_cache[layer, page_table].reshape(ns, -1, 2, kvh, dh)
    o_c_s, imp_s = nsa_cmp_attn(q_pl_s, compress(jnp.concatenate([past_c, kv_c_s], axis=1)), pos_s)
    n_pages = page_table.shape[1]
    bpp = PAGE_SIZE // blk
    n_past_blk = PAST_LEN // blk
    nb_s = -(-(PAST_LEN + ts) // blk)
    n_new_blk = nb_s - n_past_blk
    idx_s, ok_s = nsa_select(imp_s, pos_s, nb_s)
    b_i = jnp.arange(ns)[:, None, None, None]
    h_i = jnp.arange(kvh)[None, None, :, None]
    page = page_table[b_i, jnp.minimum(idx_s // bpp, n_pages - 1)]
    rows = (idx_s % bpp)[..., None] * blk + jnp.arange(blk)
    g_past = sel_cache[layer, page[..., None], rows, :, h_i[..., None], :]
    new_blocks = jnp.pad(kv_s_s, ((0, 0), (0, n_new_blk * blk - ts), (0, 0), (0, 0), (0, 0)))
    new_blocks = new_blocks.reshape(ns, n_new_blk, blk, 2, kvh, dh)
    g_new = new_blocks[b_i, jnp.clip(idx_s - n_past_blk, 0, n_new_blk - 1), :, :, h_i, :]
    g = jnp.where((idx_s < n_past_blk)[..., None, None, None], g_past, g_new)
    o_s_s = nsa_sel_core(q_rt_s, g[..., 0, :], g[..., 1, :], idx_s, ok_s, pos_s)
    w_buf = win_state.shape[2]
    kv_all = jnp.concatenate([win_state[layer], kv_w_s], axis=1)
    o_w_s = nsa_win_core(q_rt_s, kv_all, pos_s, PAST_LEN - w_buf + jnp.arange(w_buf + ts))
    out_s = nsa_combine(gates_s, o_c_s, o_s_s, o_w_s, w_out)
    win_s = kv_all[:, kv_all.shape[1] - w_buf:]
    return out_p, out_s, kv_c, kv_c_s, kv_s, kv_s_s, win_p, win_s


def peer_block(x, wq, subkeys, u_tab, v_tab):
    m = x.shape[0]
    q = (x @ wq).reshape(m, PEER_HEADS, 2, PEER_KEY_DIM // 2)
    s = jnp.einsum('mhcd,cnd->mhcn', q, subkeys, preferred_element_type=jnp.float32)
    s1, i1 = lax.top_k(s[:, :, 0], PEER_TOPK)
    s2, i2 = lax.top_k(s[:, :, 1], PEER_TOPK)
    cand = (s1[..., :, None] + s2[..., None, :]).reshape(m, PEER_HEADS, PEER_TOPK * PEER_TOPK)
    cidx = (i1[..., :, None] * PEER_N_KEYS + i2[..., None, :]).reshape(m, PEER_HEADS, PEER_TOPK * PEER_TOPK)
    top, sel = lax.top_k(cand, PEER_TOPK)
    eidx = jnp.take_along_axis(cidx, sel, axis=-1)
    gate = jax.nn.softmax(top, axis=-1)
    act = jax.nn.gelu(jnp.einsum('mhkd,md->mhk', u_tab[eidx], x, preferred_element_type=jnp.float32),
                      approximate=False)
    return jnp.einsum('mhk,mhkd->md', (gate * act).astype(x.dtype), v_tab[eidx])


def peer_ffn(x, wq, subkeys, u_tab, v_tab):
    n, t, d = x.shape
    m = n * t
    nb = -(-m // PEER_TOKEN_BLOCK)
    flat = jnp.pad(x.reshape(m, d), ((0, nb * PEER_TOKEN_BLOCK - m), (0, 0))).reshape(nb, PEER_TOKEN_BLOCK, d)
    out = lax.map(lambda xb: peer_block(xb, wq, subkeys, u_tab, v_tab), flat)
    return out.reshape(nb * PEER_TOKEN_BLOCK, d)[:m].reshape(n, t, d)


def setup_inputs(seed: int = 0) -> dict:
    key = jax.random.key(seed)
    ks = jax.random.split(key, 26)
    f32 = jnp.float32
    D = D_MODEL

    def nrm(k, shape, scale):
        return jax.random.normal(k, shape, f32) * scale

    n_pages = PAST_LEN // PAGE_SIZE
    n_used = DEC_BATCH * n_pages
    n_phys = n_used + max(1, n_used // 4)
    page_table = jax.random.permutation(ks[6], n_phys)[:n_used].reshape(DEC_BATCH, n_pages).astype(jnp.int32)
    w_buf = min(NSA_WINDOW, PAST_LEN)
    return {
        'x_prompt': nrm(ks[0], (BATCH, SEQ, D), 1.0),
        'x_sample': nrm(ks[1], (DEC_BATCH, DEC_SEQ, D), 1.0),
        'cache_diff_kv': nrm(ks[2], (N_LAYERS_A, n_phys, PAGE_SIZE, 2, DA_HEADS, 2 * DA_HEAD_DIM), 1.0),
        'cache_nsa_cmp_kv': nrm(ks[3], (N_LAYERS_B, n_phys, PAGE_SIZE, 2, NSA_KV_HEADS, NSA_HEAD_DIM), 1.0),
        'cache_nsa_sel_kv': nrm(ks[4], (N_LAYERS_B, n_phys, PAGE_SIZE, 2, NSA_KV_HEADS, NSA_HEAD_DIM), 1.0),
        'state_nsa_win_kv': nrm(ks[5], (N_LAYERS_B, DEC_BATCH, w_buf, 2, NSA_KV_HEADS, NSA_HEAD_DIM), 1.0),
        'page_table': page_table,
        'norm_mix_g': 1.0 + nrm(ks[7], (DEPTH, D), 0.02),
        'diff_w_in': nrm(ks[8], (N_LAYERS_A, D, DA_IN), D ** -0.5),
        'diff_lambda': nrm(ks[9], (N_LAYERS_A, 4, DA_HEAD_DIM), 0.1),
        'diff_subln_g': 1.0 + nrm(ks[10], (N_LAYERS_A, 2 * DA_HEAD_DIM), 0.02),
        'diff_w_out': nrm(ks[11], (N_LAYERS_A, DA_HEADS * 2 * DA_HEAD_DIM, D), (DA_HEADS * 2 * DA_HEAD_DIM) ** -0.5),
        'nsa_w_in': nrm(ks[12], (N_LAYERS_B, D, NSA_IN), D ** -0.5),
        'nsa_cmp_pos': nrm(ks[13], (N_LAYERS_B, NSA_BLOCK, 2, NSA_HEAD_DIM), 0.1),
        'nsa_cmp_w1': nrm(ks[14], (N_LAYERS_B, 2, NSA_BLOCK, NSA_HEAD_DIM, NSA_CMP_HIDDEN), (NSA_BLOCK * NSA_HEAD_DIM) ** -0.5),
        'nsa_cmp_b1': nrm(ks[15], (N_LAYERS_B, 2, NSA_CMP_HIDDEN), 0.01),
        'nsa_cmp_w2': nrm(ks[16], (N_LAYERS_B, 2, NSA_CMP_HIDDEN, NSA_HEAD_DIM), NSA_CMP_HIDDEN ** -0.5),
        'nsa_cmp_b2': nrm(ks[17], (N_LAYERS_B, 2, NSA_HEAD_DIM), 0.01),
        'nsa_w_out': nrm(ks[18], (N_LAYERS_B, NSA_HEADS * NSA_HEAD_DIM, D), (NSA_HEADS * NSA_HEAD_DIM) ** -0.5),
        'norm_ffn_g': 1.0 + nrm(ks[19], (DEPTH, D), 0.02),
        'peer_wq': nrm(ks[20], (DEPTH, D, PEER_HEADS * PEER_KEY_DIM), D ** -0.5),
        'peer_subkeys': nrm(ks[21], (DEPTH, 2, PEER_N_KEYS, PEER_KEY_DIM // 2), (PEER_KEY_DIM // 2) ** -0.5),
        'peer_u': nrm(ks[22], (DEPTH, PEER_N_EXPERTS, D), D ** -0.5),
        'peer_v': nrm(ks[23], (DEPTH, PEER_N_EXPERTS, D), PEER_HEADS ** -0.5),
        'final_norm_g': 1.0 + nrm(ks[24], (D,), 0.02),
    }


def reference(x_prompt, x_sample, cache_diff_kv, cache_nsa_cmp_kv, cache_nsa_sel_kv, state_nsa_win_kv,
              page_table, norm_mix_g, diff_w_in, diff_lambda, diff_subln_g, diff_w_out,
              nsa_w_in, nsa_cmp_pos, nsa_cmp_w1, nsa_cmp_b1, nsa_cmp_w2, nsa_cmp_b2, nsa_w_out,
              norm_ffn_g, peer_wq, peer_subkeys, peer_u, peer_v, final_norm_g):
    hp, hs = x_prompt, x_sample
    diff_p, diff_s, cmp_p, cmp_s, sel_p, sel_s, win_p, win_s = [], [], [], [], [], [], [], []
    for i in range(DEPTH):
        xp = rmsnorm(hp, norm_mix_g[i])
        xs = rmsnorm(hs, norm_mix_g[i])
        if i % 2 == 0:
            a = i // 2
            op, os_, kvp, kvs = diff_layer(xp, xs, cache_diff_kv, a, i, page_table,
                                           diff_w_in[a], diff_lambda[a], diff_subln_g[a], diff_w_out[a])
            diff_p.append(kvp)
            diff_s.append(kvs)
        else:
            b = i // 2
            op, os_, cp, cs, sp, ss, wp, ws = nsa_layer(
                xp, xs, cache_nsa_cmp_kv, cache_nsa_sel_kv, state_nsa_win_kv, b, page_table,
                nsa_w_in[b], nsa_cmp_pos[b], nsa_cmp_w1[b], nsa_cmp_b1[b], nsa_cmp_w2[b], nsa_cmp_b2[b], nsa_w_out[b])
            cmp_p.append(cp)
            cmp_s.append(cs)
            sel_p.append(sp)
            sel_s.append(ss)
            win_p.append(wp)
            win_s.append(ws)
        hp = hp + op
        hs = hs + os_
        hp = hp + peer_ffn(rmsnorm(hp, norm_ffn_g[i]), peer_wq[i], peer_subkeys[i], peer_u[i], peer_v[i])
        hs = hs + peer_ffn(rmsnorm(hs, norm_ffn_g[i]), peer_wq[i], peer_subkeys[i], peer_u[i], peer_v[i])
    y_prompt = rmsnorm(hp, final_norm_g)
    y_sample = rmsnorm(hs, final_norm_g)
    return (y_prompt, y_sample,
            jnp.stack(diff_p), jnp.stack(diff_s),
            jnp.stack(cmp_p), jnp.stack(cmp_s),
            jnp.stack(sel_p), jnp.stack(sel_s),
            jnp.stack(win_p), jnp.stack(win_s))
```

```python
import functools
import math

import jax
import jax.numpy as jnp
from jax import lax
from jax.experimental import pallas as pl
from jax.experimental.pallas import tpu as pltpu

F32 = jnp.float32
BF16 = jnp.bfloat16

NORM_EPS = 1e-6
ROPE_THETA = 500000.0
ROPE_FRACTION = 4
NEG_INF = -1e30
MASKED = -3.0e38

DA_HEADS = 8
DA_HEAD_DIM = 64
NSA_HEADS = 16
NSA_KV_HEADS = 4
NSA_GROUP = NSA_HEADS // NSA_KV_HEADS
NSA_HEAD_DIM = 64
NSA_BLOCK = 64
NSA_TOPN = 16
NSA_WINDOW = 512
NSA_FORCE_SCORE = 1e4
PEER_HEADS = 8
PEER_TOPK = 16

LANES = 128
VMEM_LIMIT_BYTES = 56 * 1024 * 1024


def _cparams(*sem):
    return pltpu.CompilerParams(dimension_semantics=tuple(sem), vmem_limit_bytes=VMEM_LIMIT_BYTES)


def _nt_dot(a, b):
    return lax.dot_general(a, b, (((1,), (1,)), ((), ())), preferred_element_type=F32)


def _tn_dot(a, b):
    return lax.dot_general(a, b, (((0,), (0,)), ((), ())), preferred_element_type=F32)


def _rmsnorm_rows(x, g):
    ms = jnp.mean(x * x, axis=-1, keepdims=True)
    return x * lax.rsqrt(ms + NORM_EPS) * g


def _rope_tables(pos, head_dim):
    d_rot = head_dim // ROPE_FRACTION
    half = d_rot // 2
    inv_freq = ROPE_THETA ** (-jnp.arange(half, dtype=F32) / half)
    ang = pos.astype(F32)[:, None] * inv_freq[None, :]
    cos, sin = jnp.cos(ang), jnp.sin(ang)
    n = pos.shape[0]
    zeros = lambda w: jnp.zeros((n, w), F32)
    c = jnp.concatenate([cos, cos, jnp.ones((n, head_dim - d_rot), F32)], axis=1)
    sa = jnp.concatenate([-sin, zeros(head_dim - half)], axis=1)
    sb = jnp.concatenate([zeros(half), sin, zeros(head_dim - d_rot)], axis=1)
    rep = LANES // head_dim
    return jnp.tile(c, (1, rep)), jnp.tile(sa, (1, rep)), jnp.tile(sb, (1, rep)), half


def _rope_cols(y, c, sa, sb, half):
    outs = []
    for j in range(y.shape[1] // LANES):
        ch = y[:, j * LANES:(j + 1) * LANES]
        outs.append(ch * c + pltpu.roll(ch, LANES - half, 1) * sa + pltpu.roll(ch, half, 1) * sb)
    return outs[0] if len(outs) == 1 else jnp.concatenate(outs, axis=1)


def _diff_proj_kernel(x_ref, g_ref, w_ref, c_ref, sa_ref, sb_ref,
                      q_ref, kv_ref, kb_ref, vb_ref, *, half, scale):
    d = x_ref.shape[1]
    xn = _rmsnorm_rows(x_ref[...], g_ref[...]).astype(BF16)
    y = jnp.dot(xn, w_ref[...], preferred_element_type=F32)
    c, sa, sb = c_ref[...], sa_ref[...], sb_ref[...]
    q = _rope_cols(y[:, :d], c, sa, sb, half) * scale
    k = _rope_cols(y[:, d:2 * d], c, sa, sb, half)
    v = y[:, 2 * d:]
    q_ref[...] = q.astype(BF16)
    kv_ref[:, :d] = k
    kv_ref[:, d:] = v
    kb_ref[...] = k.astype(BF16)
    vb_ref[...] = v.astype(BF16)


def _diff_proj(x, g, w_bf, tables, period_rows):
    rows, d = x.shape
    c, sa, sb, half = tables
    tm = min(512, rows)
    nper = period_rows // tm
    row = lambda i: (i, 0)
    tab = pl.BlockSpec((tm, LANES), lambda i: (i % nper, 0))
    return pl.pallas_call(
        functools.partial(_diff_proj_kernel, half=half, scale=DA_HEAD_DIM ** -0.5),
        grid=(rows // tm,),
        in_specs=[pl.BlockSpec((tm, d), row), pl.BlockSpec((1, d), lambda i: (0, 0)),
                  pl.BlockSpec((d, 3 * d), lambda i: (0, 0)), tab, tab, tab],
        out_specs=[pl.BlockSpec((tm, d), row), pl.BlockSpec((tm, 2 * d), row),
                   pl.BlockSpec((tm, d), row), pl.BlockSpec((tm, d), row)],
        out_shape=[jax.ShapeDtypeStruct((rows, d), BF16), jax.ShapeDtypeStruct((rows, 2 * d), F32),
                   jax.ShapeDtypeStruct((rows, d), BF16), jax.ShapeDtypeStruct((rows, d), BF16)],
        compiler_params=_cparams("parallel"),
        name="diff_proj",
    )(x, g.reshape(1, d), w_bf, c, sa, sb)


def _diff_lambda(lam_ref, lam_init):
    lv = lam_ref[...]
    a = jnp.sum(lv[0:1, :] * lv[1:2, :], axis=-1, keepdims=True)
    b = jnp.sum(lv[2:3, :] * lv[3:4, :], axis=-1, keepdims=True)
    return jnp.exp(a) - jnp.exp(b) + lam_init


def _subln(o, g, lam_init):
    ms = jnp.mean(o * o, axis=-1, keepdims=True)
    return o * lax.rsqrt(ms + NORM_EPS) * g * (1.0 - lam_init)


def _online_update(s, mask, v, m, l, acc):
    if mask is not None:
        s = jnp.where(mask, s, NEG_INF)
    m_new = jnp.maximum(m, jnp.max(s, axis=-1, keepdims=True))
    alpha = jnp.exp(m - m_new)
    p = jnp.exp(s - m_new)
    if mask is not None:
        p = jnp.where(mask, p, 0.0)
    l_new = alpha * l + jnp.sum(p, axis=-1, keepdims=True)
    acc_new = alpha * acc + jnp.dot(p.astype(BF16), v, preferred_element_type=F32)
    return m_new, l_new, acc_new


def _diff_attn_kernel(q_ref, k_ref, v_ref, lam_ref, g_ref, o_ref, *, tq, tk, lam_init):
    i = pl.program_id(2)
    hd2 = q_ref.shape[1]
    q = q_ref[...]
    lane = lax.broadcasted_iota(jnp.int32, (tq, hd2), 1)
    zero = jnp.zeros_like(q)
    qq = jnp.concatenate([jnp.where(lane < hd2 // 2, q, zero), jnp.where(lane >= hd2 // 2, q, zero)], axis=0)
    qpos = i * tq + lax.broadcasted_iota(jnp.int32, (2 * tq, tk), 0) % tq
    kcol = lax.broadcasted_iota(jnp.int32, (2 * tq, tk), 1)

    def body(j, carry):
        m, l, acc = carry
        start = pl.multiple_of(j * tk, tk)
        kj = k_ref[pl.ds(start, tk), :]
        vj = v_ref[pl.ds(start, tk), :]
        s = _nt_dot(qq, kj)
        mask = (kcol + j * tk) <= qpos
        return _online_update(s, mask, vj, m, l, acc)

    n_kv = ((i + 1) * tq + tk - 1) // tk
    m0 = jnp.full((2 * tq, 1), NEG_INF, F32)
    l0 = jnp.zeros((2 * tq, 1), F32)
    a0 = jnp.zeros((2 * tq, v_ref.shape[1]), F32)
    m, l, acc = lax.fori_loop(0, n_kv, body, (m0, l0, a0))
    lam = _diff_lambda(lam_ref, lam_init)
    inv = 1.0 / jnp.maximum(l, 1e-30)
    o = acc[:tq] * inv[:tq] - lam * (acc[tq:] * inv[tq:])
    o_ref[...] = _subln(o, g_ref[...], lam_init).astype(BF16)


def _diff_attn_prompt(q, kb, vb, lam_vec, subln_g, batch, seq, lam_init):
    rows, d = q.shape
    hd2 = 2 * DA_HEAD_DIM
    tq = tk = min(256, seq)
    nq = seq // tq
    return pl.pallas_call(
        functools.partial(_diff_attn_kernel, tq=tq, tk=tk, lam_init=lam_init),
        grid=(batch, DA_HEADS, nq),
        in_specs=[pl.BlockSpec((tq, hd2), lambda b, h, i: (b * nq + i, h)),
                  pl.BlockSpec((seq, hd2), lambda b, h, i: (b, h)),
                  pl.BlockSpec((seq, hd2), lambda b, h, i: (b, h)),
                  pl.BlockSpec(lam_vec.shape, lambda b, h, i: (0, 0)),
                  pl.BlockSpec((1, hd2), lambda b, h, i: (0, 0))],
        out_specs=pl.BlockSpec((tq, hd2), lambda b, h, i: (b * nq + i, h)),
        out_shape=jax.ShapeDtypeStruct((rows, d), BF16),
        compiler_params=_cparams("parallel", "parallel", "parallel"),
        name="diff_attn_prompt",
    )(q, kb, vb, lam_vec, subln_g.reshape(1, hd2))


def _block_diag_rows(q, n_groups, group_w):
    t, w = q.shape
    tiled = jnp.concatenate([q] * n_groups, axis=0)
    r = lax.broadcasted_iota(jnp.int32, (n_groups * t, w), 0) // t
    c = lax.broadcasted_iota(jnp.int32, (n_groups * t, w), 1) // group_w
    return jnp.where(r == c, tiled, jnp.zeros_like(tiled))


def _diff_dec_kernel(pt_ref, q_ref, *rest, n_pg, ts, lam_init):
    cache_refs = rest[:n_pg]
    kvn_ref, lam_ref, g_ref, o_ref, qbd_sc, m_sc, l_sc, acc_sc = rest[n_pg:]
    j = pl.program_id(1)
    d = q_ref.shape[1]
    n_comp = 2 * DA_HEADS
    rows = n_comp * ts

    @pl.when(j == 0)
    def _():
        qbd_sc[...] = _block_diag_rows(q_ref[...], n_comp, DA_HEAD_DIM)
        m_sc[...] = jnp.full(m_sc.shape, NEG_INF, F32)
        l_sc[...] = jnp.zeros(l_sc.shape, F32)
        acc_sc[...] = jnp.zeros(acc_sc.shape, F32)

    qbd = qbd_sc[...]
    m, l, acc = m_sc[...], l_sc[...], acc_sc[...]
    for ref in cache_refs:
        kv = ref[...]
        k = kv[:, :d].astype(BF16)
        v = kv[:, d:].astype(BF16)
        m, l, acc = _online_update(_nt_dot(qbd, k), None, v, m, l, acc)
    m_sc[...], l_sc[...], acc_sc[...] = m, l, acc

    @pl.when(j == pl.num_programs(1) - 1)
    def _():
        kvn = kvn_ref[...]
        pad = jnp.zeros((LANES - ts, d), F32)
        kn = jnp.concatenate([kvn[:, :d], pad], axis=0).astype(BF16)
        vn = jnp.concatenate([kvn[:, d:], pad], axis=0).astype(BF16)
        qi = lax.broadcasted_iota(jnp.int32, (rows, LANES), 0) % ts
        col = lax.broadcasted_iota(jnp.int32, (rows, LANES), 1)
        mf, lf, af = _online_update(_nt_dot(qbd, kn), col <= qi, vn, m_sc[...], l_sc[...], acc_sc[...])
        lam = _diff_lambda(lam_ref, lam_init)
        af = af * (1.0 / jnp.maximum(lf, 1e-30))
        hd2 = 2 * DA_HEAD_DIM
        for h in range(DA_HEADS):
            blk = af[2 * h * ts:(2 * h + 2) * ts, h * hd2:(h + 1) * hd2]
            o = blk[:ts] - lam * blk[ts:]
            o_ref[:, h * hd2:(h + 1) * hd2] = _subln(o, g_ref[...], lam_init).astype(BF16)


def _diff_attn_sample(q, cache, layer, page_table, kv_new, lam_vec, subln_g, lam_init):
    db, ts, d = q.shape
    n_pages = page_table.shape[1]
    page = cache.shape[2]
    n_pg = 4 if n_pages % 4 == 0 else 1
    cache2 = cache.reshape(cache.shape[0], cache.shape[1], page, 2 * d)
    rows = 2 * DA_HEADS * ts

    def cache_spec(u):
        return pl.BlockSpec((None, None, page, 2 * d), lambda s, j, pt: (layer, pt[s, j * n_pg + u], 0, 0))

    gs = pltpu.PrefetchScalarGridSpec(
        num_scalar_prefetch=1,
        grid=(db, n_pages // n_pg),
        in_specs=[pl.BlockSpec((None, ts, d), lambda s, j, pt: (s, 0, 0))]
                 + [cache_spec(u) for u in range(n_pg)]
                 + [pl.BlockSpec((None, ts, 2 * d), lambda s, j, pt: (s, 0, 0)),
                    pl.BlockSpec(lam_vec.shape, lambda s, j, pt: (0, 0)),
                    pl.BlockSpec((1, 2 * DA_HEAD_DIM), lambda s, j, pt: (0, 0))],
        out_specs=pl.BlockSpec((None, ts, d), lambda s, j, pt: (s, 0, 0)),
        scratch_shapes=[pltpu.VMEM((rows, d), BF16), pltpu.VMEM((rows, 1), F32),
                        pltpu.VMEM((rows, 1), F32), pltpu.VMEM((rows, d), F32)],
    )
    return pl.pallas_call(
        functools.partial(_diff_dec_kernel, n_pg=n_pg, ts=ts, lam_init=lam_init),
        grid_spec=gs,
        out_shape=jax.ShapeDtypeStruct((db, ts, d), BF16),
        compiler_params=_cparams("parallel", "arbitrary"),
        name="diff_attn_sample",
    )(page_table, q, *([cache2] * n_pg), kv_new, lam_vec, subln_g.reshape(1, -1))


def _outproj_kernel(h_ref, o_ref, w_ref, out_ref):
    out_ref[...] = h_ref[...] + jnp.dot(o_ref[...], w_ref[...], preferred_element_type=F32)


def _outproj(h, o, w_bf):
    rows, d = h.shape
    tm = min(512, rows)
    row = lambda i: (i, 0)
    return pl.pallas_call(
        _outproj_kernel,
        grid=(rows // tm,),
        in_specs=[pl.BlockSpec((tm, d), row), pl.BlockSpec((tm, o.shape[1]), row),
                  pl.BlockSpec(w_bf.shape, lambda i: (0, 0))],
        out_specs=pl.BlockSpec((tm, d), row),
        out_shape=jax.ShapeDtypeStruct((rows, d), F32),
        compiler_params=_cparams("parallel"),
        name="outproj",
    )(h, o, w_bf)


def _nsa_proj_kernel(x_ref, g_ref, w_ref, wg_ref, c_ref, sa_ref, sb_ref,
                     qpl_ref, qrt_ref, kvc_ref, kvs_ref, kvw_ref, ks_ref, vs_ref, kw_ref, vw_ref, gate_ref,
                     *, half, scale):
    d = x_ref.shape[1]
    hd = NSA_HEAD_DIM
    kw = NSA_KV_HEADS * hd
    xn = _rmsnorm_rows(x_ref[...], g_ref[...]).astype(BF16)
    y = jnp.dot(xn, w_ref[...], preferred_element_type=F32)
    gl = jnp.dot(xn, wg_ref[...], preferred_element_type=F32)
    gate_ref[...] = 1.0 / (1.0 + jnp.exp(-gl))
    c, sa, sb = c_ref[...], sa_ref[...], sb_ref[...]
    q = y[:, :d]
    q_pl = (q * scale).astype(BF16)
    q_rt = (_rope_cols(q, c, sa, sb, half) * scale).astype(BF16)
    for h in range(NSA_HEADS):
        qpl_ref[h] = q_pl[:, h * hd:(h + 1) * hd]
        qrt_ref[h] = q_rt[:, h * hd:(h + 1) * hd]
    kvc_ref[...] = y[:, d:d + 2 * kw]
    off = d + 2 * kw
    for kv_ref, k_ref, v_ref in ((kvs_ref, ks_ref, vs_ref), (kvw_ref, kw_ref, vw_ref)):
        k = _rope_cols(y[:, off:off + kw], c, sa, sb, half)
        v = y[:, off + kw:off + 2 * kw]
        kv_ref[:, :kw] = k
        kv_ref[:, kw:] = v
        kb, vb = k.astype(BF16), v.astype(BF16)
        for h in range(NSA_KV_HEADS):
            k_ref[h] = kb[:, h * hd:(h + 1) * hd]
            v_ref[h] = vb[:, h * hd:(h + 1) * hd]
        off += 2 * kw


def _nsa_proj(x, g, w_bf, wg_bf, tables, period_rows):
    rows, d = x.shape
    c, sa, sb, half = tables
    tm = min(512, rows)
    nper = period_rows // tm
    hd = NSA_HEAD_DIM
    kvw = 2 * NSA_KV_HEADS * hd
    row = lambda i: (i, 0)
    hm = lambda i: (0, i, 0)
    tab = pl.BlockSpec((tm, LANES), lambda i: (i % nper, 0))
    const = lambda i: (0, 0)
    qhm = jax.ShapeDtypeStruct((NSA_HEADS, rows, hd), BF16)
    khm = jax.ShapeDtypeStruct((NSA_KV_HEADS, rows, hd), BF16)
    kvf = jax.ShapeDtypeStruct((rows, kvw), F32)
    return pl.pallas_call(
        functools.partial(_nsa_proj_kernel, half=half, scale=hd ** -0.5),
        grid=(rows // tm,),
        in_specs=[pl.BlockSpec((tm, d), row), pl.BlockSpec((1, d), const),
                  pl.BlockSpec(w_bf.shape, const), pl.BlockSpec(wg_bf.shape, const), tab, tab, tab],
        out_specs=[pl.BlockSpec((NSA_HEADS, tm, hd), hm), pl.BlockSpec((NSA_HEADS, tm, hd), hm),
                   pl.BlockSpec((tm, kvw), row), pl.BlockSpec((tm, kvw), row), pl.BlockSpec((tm, kvw), row),
                   pl.BlockSpec((NSA_KV_HEADS, tm, hd), hm), pl.BlockSpec((NSA_KV_HEADS, tm, hd), hm),
                   pl.BlockSpec((NSA_KV_HEADS, tm, hd), hm), pl.BlockSpec((NSA_KV_HEADS, tm, hd), hm),
                   pl.BlockSpec((tm, LANES), row)],
        out_shape=[qhm, qhm, kvf, kvf, kvf, khm, khm, khm, khm, jax.ShapeDtypeStruct((rows, LANES), F32)],
        compiler_params=_cparams("parallel"),
        name="nsa_proj",
    )(x, g.reshape(1, d), w_bf, wg_bf, c, sa, sb)


def _gelu(x):
    return 0.5 * x * (1.0 + lax.erf(x * (2.0 ** -0.5)))


def _compress_tile(load_rows, n_blk, pos_ref, w1_ref, b1_ref, w2_ref, b2_ref):
    half = w1_ref.shape[2]
    acc = [jnp.zeros((n_blk, half), F32), jnp.zeros((n_blk, half), F32)]
    for r in range(NSA_BLOCK):
        xr = (load_rows(r) + pos_ref[r:r + 1, :]).astype(BF16)
        for c in range(2):
            acc[c] = acc[c] + jnp.dot(xr[:, c * half:(c + 1) * half], w1_ref[r, c], preferred_element_type=F32)
    hid = _gelu(jnp.concatenate(acc, axis=1) + b1_ref[...]).astype(BF16)
    return jnp.dot(hid, w2_ref[...], preferred_element_type=F32) + b2_ref[...]


def _store_compressed(out, kc_ref, vc_ref):
    hd = NSA_HEAD_DIM
    ob = out.astype(BF16)
    for h in range(NSA_KV_HEADS):
        kc_ref[h] = ob[:, h * hd:(h + 1) * hd]
        vc_ref[h] = ob[:, (NSA_KV_HEADS + h) * hd:(NSA_KV_HEADS + h + 1) * hd]


def _strided_token_rows(ref, r, n_blk, chunks):
    parts = [ref[pl.ds(r * chunks + q, n_blk, stride=NSA_BLOCK * chunks), :] for q in range(chunks)]
    return jnp.concatenate(parts, axis=1)


def _compress_prompt_kernel(x_ref, pos_ref, w1_ref, b1_ref, w2_ref, b2_ref, kc_ref, vc_ref):
    chunks = pos_ref.shape[1] // LANES
    n_blk = x_ref.shape[0] // (NSA_BLOCK * chunks)
    load = lambda r: _strided_token_rows(x_ref, r, n_blk, chunks)
    _store_compressed(_compress_tile(load, n_blk, pos_ref, w1_ref, b1_ref, w2_ref, b2_ref), kc_ref, vc_ref)


def _compress_weights(cmp_pos, w1, b1, w2, b2):
    kvh, hd = NSA_KV_HEADS, NSA_HEAD_DIM
    hid = w1.shape[-1]
    eye = jnp.eye(kvh, dtype=F32)
    pos_rep = jnp.broadcast_to(cmp_pos[:, :, None, :], (NSA_BLOCK, 2, kvh, hd)).reshape(NSA_BLOCK, 2 * kvh * hd)
    w1bd = jnp.einsum('crde,kl->rckdle', w1, eye).reshape(NSA_BLOCK, 2, kvh * hd, kvh * hid).astype(BF16)
    b1_rep = jnp.broadcast_to(b1[:, None, :], (2, kvh, hid)).reshape(1, 2 * kvh * hid)
    w2bd = jnp.einsum('aed,ab,kl->akebld', w2, jnp.eye(2, dtype=F32), eye)
    w2bd = w2bd.reshape(2 * kvh * hid, 2 * kvh * hd).astype(BF16)
    b2_rep = jnp.broadcast_to(b2[:, None, :], (2, kvh, hd)).reshape(1, 2 * kvh * hd)
    return pos_rep, w1bd, b1_rep, w2bd, b2_rep


def _compress_prompt(kv_c, cw, batch, seq):
    rows, w = kv_c.shape
    nc = seq // NSA_BLOCK
    chunks = w // LANES
    pos_rep, w1bd, b1_rep, w2bd, b2_rep = cw
    full = lambda a: pl.BlockSpec(a.shape, lambda b: (0,) * a.ndim)
    out = jax.ShapeDtypeStruct((batch, NSA_KV_HEADS, nc, NSA_HEAD_DIM), BF16)
    ospec = pl.BlockSpec((None, NSA_KV_HEADS, nc, NSA_HEAD_DIM), lambda b: (b, 0, 0, 0))
    return pl.pallas_call(
        _compress_prompt_kernel,
        grid=(batch,),
        in_specs=[pl.BlockSpec((seq * chunks, LANES), lambda b: (b, 0)), full(pos_rep), full(w1bd), full(b1_rep),
                  full(w2bd), full(b2_rep)],
        out_specs=[ospec, ospec],
        out_shape=[out, out],
        compiler_params=_cparams("parallel"),
        name="nsa_compress_prompt",
    )(kv_c.reshape(rows * chunks, LANES), pos_rep, w1bd, b1_rep, w2bd, b2_rep)


def _compress_sample_kernel(pt_ref, *rest, n_pg):
    page_refs = rest[:n_pg]
    pos_ref, w1_ref, b1_ref, w2_ref, b2_ref, kc_ref, vc_ref = rest[n_pg:]
    chunks = pos_ref.shape[1] // LANES
    bpp = page_refs[0].shape[0] // (NSA_BLOCK * chunks)

    def load(r):
        return jnp.concatenate([_strided_token_rows(ref, r, bpp, chunks) for ref in page_refs], axis=0)

    _store_compressed(_compress_tile(load, n_pg * bpp, pos_ref, w1_ref, b1_ref, w2_ref, b2_ref), kc_ref, vc_ref)


def _compress_sample(cache, layer, page_table, cw):
    db, n_pages = page_table.shape
    page = cache.shape[2]
    w = cache.shape[3] * cache.shape[4] * cache.shape[5]
    chunks = w // LANES
    cache2 = cache.reshape(cache.shape[0], cache.shape[1], page * chunks, LANES)
    n_pg = 8 if n_pages % 8 == 0 else 1
    bpp = page // NSA_BLOCK
    nb_step = n_pg * bpp
    nc = n_pages * bpp
    pos_rep, w1bd, b1_rep, w2bd, b2_rep = cw
    full = lambda a: pl.BlockSpec(a.shape, lambda s, j, pt: (0,) * a.ndim)

    def page_spec(u):
        return pl.BlockSpec((None, None, page * chunks, LANES), lambda s, j, pt: (layer, pt[s, j * n_pg + u], 0, 0))

    out = jax.ShapeDtypeStruct((db, NSA_KV_HEADS, nc, NSA_HEAD_DIM), BF16)
    ospec = pl.BlockSpec((None, NSA_KV_HEADS, nb_step, NSA_HEAD_DIM), lambda s, j, pt: (s, 0, j, 0))
    gs = pltpu.PrefetchScalarGridSpec(
        num_scalar_prefetch=1,
        grid=(db, n_pages // n_pg),
        in_specs=[page_spec(u) for u in range(n_pg)]
                 + [full(pos_rep), full(w1bd), full(b1_rep), full(w2bd), full(b2_rep)],
        out_specs=[ospec, ospec],
    )
    return pl.pallas_call(
        functools.partial(_compress_sample_kernel, n_pg=n_pg),
        grid_spec=gs,
        out_shape=[out, out],
        compiler_params=_cparams("parallel", "parallel"),
        name="nsa_compress_sample",
    )(page_table, *([cache2] * n_pg), pos_rep, w1bd, b1_rep, w2bd, b2_rep)


def _cmp_sel_prompt_kernel(q_ref, kc_ref, vc_ref, oc_ref, sel_ref, score_sc, *, tq, nb):
    i = pl.program_id(1)
    nc = kc_ref.shape[1]
    nbp = sel_ref.shape[2]
    pos = i * tq + lax.broadcasted_iota(jnp.int32, (nbp, tq), 1)
    jblk = lax.broadcasted_iota(jnp.int32, (nbp, tq), 0)
    vis = (((lax.broadcasted_iota(jnp.int32, (nc, tq), 0) + 1) * NSA_BLOCK - 1)
           <= i * tq + lax.broadcasted_iota(jnp.int32, (nc, tq), 1))
    cur = pos // NSA_BLOCK
    forced = (jblk == 0) | (jblk == cur) | (jblk == cur - 1)
    for k in range(NSA_KV_HEADS):
        kc, vc = kc_ref[k], vc_ref[k]
        imp = jnp.zeros((nc, tq), F32)
        for g in range(NSA_GROUP):
            h = k * NSA_GROUP + g
            s = jnp.where(vis, _nt_dot(kc, q_ref[h]), NEG_INF)
            m = jnp.max(s, axis=0, keepdims=True)
            e = jnp.where(vis, jnp.exp(s - m), 0.0)
            p = e / jnp.maximum(jnp.sum(e, axis=0, keepdims=True), 1e-30)
            imp = imp + p
            oc_ref[h] = jnp.dot(p.T.astype(BF16), vc, preferred_element_type=F32).astype(BF16)
        if nbp > nc:
            imp = jnp.concatenate([imp, jnp.zeros((nbp - nc, tq), F32)], axis=0)
        score = jnp.where(forced, NSA_FORCE_SCORE, imp)
        score = jnp.where((jblk <= cur) & (jblk < nb), score, NEG_INF)
        score_sc[...] = score

        def rank_body(r, rank):
            row = score_sc[pl.ds(r, 1), :]
            ahead = (row > score) | ((row == score) & (r < jblk))
            return rank + jnp.where(ahead, 1.0, 0.0)

        rank = lax.fori_loop(0, nb, rank_body, jnp.zeros((nbp, tq), F32))
        sel = jnp.where((rank < NSA_TOPN) & (score > 0.5 * NEG_INF), 1.0, 0.0)
        sel_ref[k] = sel.T


def _cmp_sel_prompt(q_pl, kc, vc, batch, seq):
    rows = q_pl.shape[1]
    hd = NSA_HEAD_DIM
    tq = min(256, seq)
    nq = seq // tq
    nc = kc.shape[2]
    nb = -(-seq // NSA_BLOCK)
    nbp = -(-nb // LANES) * LANES
    return pl.pallas_call(
        functools.partial(_cmp_sel_prompt_kernel, tq=tq, nb=nb),
        grid=(batch, nq),
        in_specs=[pl.BlockSpec((NSA_HEADS, tq, hd), lambda b, i: (0, b * nq + i, 0)),
                  pl.BlockSpec((None, NSA_KV_HEADS, nc, hd), lambda b, i: (b, 0, 0, 0)),
                  pl.BlockSpec((None, NSA_KV_HEADS, nc, hd), lambda b, i: (b, 0, 0, 0))],
        out_specs=[pl.BlockSpec((NSA_HEADS, tq, hd), lambda b, i: (0, b * nq + i, 0)),
                   pl.BlockSpec((NSA_KV_HEADS, tq, nbp), lambda b, i: (0, b * nq + i, 0))],
        out_shape=[jax.ShapeDtypeStruct((NSA_HEADS, rows, hd), BF16),
                   jax.ShapeDtypeStruct((NSA_KV_HEADS, rows, nbp), F32)],
        scratch_shapes=[pltpu.VMEM((nbp, tq), F32)],
        compiler_params=_cparams("parallel", "parallel"),
        name="nsa_cmp_select_prompt",
    )(q_pl, kc, vc)


def _sel_win_prompt_kernel(q_ref, ks_ref, vs_ref, kw_ref, vw_ref, sel_ref, os_ref, ow_ref, mask_sc, *, tq, tk):
    i = pl.program_id(2)
    g, _, hd = q_ref.shape
    nbp = sel_ref.shape[1]
    n_chunks = mask_sc.shape[0]
    q = q_ref[...].reshape(g * tq, hd)
    blk_of_key = lax.broadcasted_iota(jnp.int32, (nbp, n_chunks * tk), 1) // NSA_BLOCK
    expand = jnp.where(lax.broadcasted_iota(jnp.int32, (nbp, n_chunks * tk), 0) == blk_of_key, 1.0, 0.0)
    flags = jnp.dot(sel_ref[...].astype(BF16), expand.astype(BF16), preferred_element_type=F32)
    for c in range(n_chunks):
        mask_sc[c] = flags[:, c * tk:(c + 1) * tk]
    qpos = i * tq + lax.broadcasted_iota(jnp.int32, (g * tq, tk), 0) % tq
    kcol = lax.broadcasted_iota(jnp.int32, (g * tq, tk), 1)
    m0 = jnp.full((g * tq, 1), NEG_INF, F32)
    l0 = jnp.zeros((g * tq, 1), F32)
    a0 = jnp.zeros((g * tq, hd), F32)

    def sel_body(j, carry):
        start = pl.multiple_of(j * tk, tk)
        s = _nt_dot(q, ks_ref[pl.ds(start, tk), :])
        chosen = jnp.concatenate([mask_sc[j]] * g, axis=0) > 0.5
        mask = chosen & ((kcol + j * tk) <= qpos)
        return _online_update(s, mask, vs_ref[pl.ds(start, tk), :], *carry)

    def win_body(j, carry):
        start = pl.multiple_of(j * tk, tk)
        s = _nt_dot(q, kw_ref[pl.ds(start, tk), :])
        dist = qpos - (kcol + j * tk)
        mask = (dist >= 0) & (dist < NSA_WINDOW)
        return _online_update(s, mask, vw_ref[pl.ds(start, tk), :], *carry)

    n_kv = ((i + 1) * tq + tk - 1) // tk
    first_win = jnp.maximum(i * tq - (NSA_WINDOW - 1), 0) // tk
    for body, lo, o_ref in ((sel_body, 0, os_ref), (win_body, first_win, ow_ref)):
        m, l, acc = lax.fori_loop(lo, n_kv, body, (m0, l0, a0))
        o = acc * (1.0 / jnp.maximum(l, 1e-30))
        o_ref[...] = o.reshape(g, tq, hd).astype(BF16)


def _sel_win_prompt(q_rt, ks, vs, kw, vw, sel, batch, seq):
    rows = q_rt.shape[1]
    hd = NSA_HEAD_DIM
    tq = tk = min(128, seq)
    nq = seq // tq
    nbp = sel.shape[2]
    qspec = pl.BlockSpec((NSA_GROUP, tq, hd), lambda b, k, i: (k, b * nq + i, 0))
    kvspec = pl.BlockSpec((None, seq, hd), lambda b, k, i: (k, b, 0))
    out = jax.ShapeDtypeStruct((NSA_HEADS, rows, hd), BF16)
    return pl.pallas_call(
        functools.partial(_sel_win_prompt_kernel, tq=tq, tk=tk),
        grid=(batch, NSA_KV_HEADS, nq),
        in_specs=[qspec, kvspec, kvspec, kvspec, kvspec,
                  pl.BlockSpec((None, tq, nbp), lambda b, k, i: (k, b * nq + i, 0))],
        out_specs=[qspec, qspec],
        out_shape=[out, out],
        scratch_shapes=[pltpu.VMEM((seq // tk, tq, tk), F32)],
        compiler_params=_cparams("parallel", "parallel", "parallel"),
        name="nsa_sel_win_prompt",
    )(q_rt, ks, vs, kw, vw, sel)


def _rows_by_head(q_ref, ts):
    return jnp.concatenate([q_ref[h] for h in range(q_ref.shape[0])], axis=0)


def _cmp_sel_sample_kernel(q_ref, kc_ref, vc_ref, oc_ref, sel_ref, *, ts, past_len):
    nc = kc_ref.shape[1]
    nbp = sel_ref.shape[2]
    nb = -(-(past_len + ts) // NSA_BLOCK)
    gt = NSA_GROUP * ts
    q = _rows_by_head(q_ref, ts)
    pos_g = past_len + lax.broadcasted_iota(jnp.int32, (gt, nc), 0) % ts
    vis = ((lax.broadcasted_iota(jnp.int32, (gt, nc), 1) + 1) * NSA_BLOCK - 1) <= pos_g
    pos = past_len + lax.broadcasted_iota(jnp.int32, (ts, nbp), 0)
    jblk = lax.broadcasted_iota(jnp.int32, (ts, nbp), 1)
    cur = pos // NSA_BLOCK
    forced = (jblk == 0) | (jblk == cur) | (jblk == cur - 1)
    for k in range(NSA_KV_HEADS):
        s = jnp.where(vis, _nt_dot(q[k * gt:(k + 1) * gt], kc_ref[k]), NEG_INF)
        m = jnp.max(s, axis=-1, keepdims=True)
        e = jnp.where(vis, jnp.exp(s - m), 0.0)
        p = e / jnp.maximum(jnp.sum(e, axis=-1, keepdims=True), 1e-30)
        o = jnp.dot(p.astype(BF16), vc_ref[k], preferred_element_type=F32).astype(BF16)
        imp = p[0:ts]
        for g in range(NSA_GROUP):
            oc_ref[k * NSA_GROUP + g] = o[g * ts:(g + 1) * ts]
            if g:
                imp = imp + p[g * ts:(g + 1) * ts]
        if nbp > nc:
            imp = jnp.concatenate([imp, jnp.zeros((ts, nbp - nc), F32)], axis=1)
        score = jnp.where(forced, NSA_FORCE_SCORE, imp)
        score = jnp.where((jblk <= cur) & (jblk < nb), score, NEG_INF)
        rank = jnp.zeros((ts, nbp), F32)
        for r in range(nb):
            col = score[:, r:r + 1]
            ahead = (col > score) | ((col == score) & (r < jblk))
            rank = rank + jnp.where(ahead, 1.0, 0.0)
        sel_ref[k] = jnp.where((rank < NSA_TOPN) & (score > 0.5 * NEG_INF), 1.0, 0.0)


def _cmp_sel_sample(q_pl, kc, vc, past_len):
    db, _, ts, hd = q_pl.shape
    nc = kc.shape[2]
    nb = -(-(past_len + ts) // NSA_BLOCK)
    nbp = -(-nb // LANES) * LANES
    seqspec = lambda a: pl.BlockSpec((None,) + a.shape[1:], lambda s: (s, 0, 0, 0))
    return pl.pallas_call(
        functools.partial(_cmp_sel_sample_kernel, ts=ts, past_len=past_len),
        grid=(db,),
        in_specs=[seqspec(q_pl), seqspec(kc), seqspec(vc)],
        out_specs=[pl.BlockSpec((None, NSA_HEADS, ts, hd), lambda s: (s, 0, 0, 0)),
                   pl.BlockSpec((None, NSA_KV_HEADS, ts, nbp), lambda s: (s, 0, 0, 0))],
        out_shape=[jax.ShapeDtypeStruct((db, NSA_HEADS, ts, hd), BF16),
                   jax.ShapeDtypeStruct((db, NSA_KV_HEADS, ts, nbp), F32)],
        compiler_params=_cparams("parallel"),
        name="nsa_cmp_select_sample",
    )(q_pl, kc, vc)


def _head_block_diag(q, ts):
    rows, hd = q.shape
    tiled = jnp.concatenate([q] * NSA_KV_HEADS, axis=1)
    r = lax.broadcasted_iota(jnp.int32, tiled.shape, 0) // (NSA_GROUP * ts)
    c = lax.broadcasted_iota(jnp.int32, tiled.shape, 1) // hd
    return jnp.where(r == c, tiled, jnp.zeros_like(tiled))


def _store_head_diag(acc, o_ref, ts):
    hd = NSA_HEAD_DIM
    gt = NSA_GROUP * ts
    for h in range(NSA_HEADS):
        k = h // NSA_GROUP
        o_ref[h] = acc[h * ts:(h + 1) * ts, k * hd:(k + 1) * hd].astype(BF16)


def _sel_sample_kernel(pt_ref, q_ref, sel_ref, *rest, n_pg, ts, past_len):
    page_refs = rest[:n_pg]
    kvn_ref, o_ref, qbd_sc, flag_sc, m_sc, l_sc, acc_sc = rest[n_pg:]
    j = pl.program_id(1)
    rows = NSA_HEADS * ts
    kw = NSA_KV_HEADS * NSA_HEAD_DIM
    page = page_refs[0].shape[0]
    nbp = sel_ref.shape[2]

    @pl.when(j == 0)
    def _():
        qbd_sc[...] = _head_block_diag(_rows_by_head(q_ref, ts), ts)
        flag_sc[...] = jnp.concatenate(
            [sel_ref[h // NSA_GROUP] for h in range(NSA_HEADS)], axis=0).astype(BF16)
        m_sc[...] = jnp.full(m_sc.shape, NEG_INF, F32)
        l_sc[...] = jnp.zeros(l_sc.shape, F32)
        acc_sc[...] = jnp.zeros(acc_sc.shape, F32)

    qbd, flags = qbd_sc[...], flag_sc[...]
    blk_row = lax.broadcasted_iota(jnp.int32, (nbp, page), 0)
    key_blk = lax.broadcasted_iota(jnp.int32, (nbp, page), 1) // NSA_BLOCK

    def chosen_keys(first_block):
        expand = jnp.where(blk_row == key_blk + first_block, 1.0, 0.0).astype(BF16)
        return jnp.dot(flags, expand, preferred_element_type=F32) > 0.5

    m, l, acc = m_sc[...], l_sc[...], acc_sc[...]
    for u, ref in enumerate(page_refs):
        kv = ref[...]
        k = kv[:, :kw].astype(BF16)
        v = kv[:, kw:].astype(BF16)
        mask = chosen_keys((j * n_pg + u) * (page // NSA_BLOCK))
        m, l, acc = _online_update(_nt_dot(qbd, k), mask, v, m, l, acc)
    m_sc[...], l_sc[...], acc_sc[...] = m, l, acc

    @pl.when(j == pl.num_programs(1) - 1)
    def _():
        kvn = kvn_ref[...]
        pad = jnp.zeros((page - ts, kw), F32)
        kn = jnp.concatenate([kvn[:, :kw], pad], axis=0).astype(BF16)
        vn = jnp.concatenate([kvn[:, kw:], pad], axis=0).astype(BF16)
        qi = lax.broadcasted_iota(jnp.int32, (rows, page), 0) % ts
        col = lax.broadcasted_iota(jnp.int32, (rows, page), 1)
        mask = chosen_keys(past_len // NSA_BLOCK) & (col <= qi)
        mf, lf, af = _online_update(_nt_dot(qbd, kn), mask, vn, m_sc[...], l_sc[...], acc_sc[...])
        _store_head_diag(af * (1.0 / jnp.maximum(lf, 1e-30)), o_ref, ts)


def _sel_sample(q_rt, sel, cache, layer, page_table, kv_new, past_len):
    db, _, ts, hd = q_rt.shape
    n_pages = page_table.shape[1]
    page = cache.shape[2]
    kw = NSA_KV_HEADS * hd
    nbp = sel.shape[3]
    cache2 = cache.reshape(cache.shape[0], cache.shape[1], page, 2 * kw)
    n_pg = 8 if n_pages % 8 == 0 else 1
    rows = NSA_HEADS * ts

    def page_spec(u):
        return pl.BlockSpec((None, None, page, 2 * kw), lambda s, j, pt: (layer, pt[s, j * n_pg + u], 0, 0))

    gs = pltpu.PrefetchScalarGridSpec(
        num_scalar_prefetch=1,
        grid=(db, n_pages // n_pg),
        in_specs=[pl.BlockSpec((None, NSA_HEADS, ts, hd), lambda s, j, pt: (s, 0, 0, 0)),
                  pl.BlockSpec((None, NSA_KV_HEADS, ts, nbp), lambda s, j, pt: (s, 0, 0, 0))]
                 + [page_spec(u) for u in range(n_pg)]
                 + [pl.BlockSpec((None, ts, 2 * kw), lambda s, j, pt: (s, 0, 0))],
        out_specs=pl.BlockSpec((None, NSA_HEADS, ts, hd), lambda s, j, pt: (s, 0, 0, 0)),
        scratch_shapes=[pltpu.VMEM((rows, kw), BF16), pltpu.VMEM((rows, nbp), BF16),
                        pltpu.VMEM((rows, 1), F32), pltpu.VMEM((rows, 1), F32), pltpu.VMEM((rows, kw), F32)],
    )
    return pl.pallas_call(
        functools.partial(_sel_sample_kernel, n_pg=n_pg, ts=ts, past_len=past_len),
        grid_spec=gs,
        out_shape=jax.ShapeDtypeStruct((db, NSA_HEADS, ts, hd), BF16),
        compiler_params=_cparams("parallel", "arbitrary"),
        name="nsa_sel_sample",
    )(page_table, q_rt, sel, *([cache2] * n_pg), kv_new)


def _win_sample_kernel(q_ref, win_ref, kvn_ref, o_ref, *, ts, past_len):
    rows = NSA_HEADS * ts
    kw = NSA_KV_HEADS * NSA_HEAD_DIM
    w_buf = win_ref.shape[0]
    qbd = _head_block_diag(_rows_by_head(q_ref, ts), ts)
    qpos = past_len + lax.broadcasted_iota(jnp.int32, (rows, w_buf), 0) % ts
    kpos = past_len - w_buf + lax.broadcasted_iota(jnp.int32, (rows, w_buf), 1)
    dist = qpos - kpos
    win = win_ref[...]
    m = jnp.full((rows, 1), NEG_INF, F32)
    l = jnp.zeros((rows, 1), F32)
    acc = jnp.zeros((rows, kw), F32)
    mask = (dist >= 0) & (dist < NSA_WINDOW) & (kpos >= 0)
    m, l, acc = _online_update(_nt_dot(qbd, win[:, :kw].astype(BF16)), mask, win[:, kw:].astype(BF16), m, l, acc)
    kvn = kvn_ref[...]
    pad = jnp.zeros((LANES - ts, kw), F32)
    kn = jnp.concatenate([kvn[:, :kw], pad], axis=0).astype(BF16)
    vn = jnp.concatenate([kvn[:, kw:], pad], axis=0).astype(BF16)
    qi = lax.broadcasted_iota(jnp.int32, (rows, LANES), 0) % ts
    col = lax.broadcasted_iota(jnp.int32, (rows, LANES), 1)
    m, l, acc = _online_update(_nt_dot(qbd, kn), col <= qi, vn, m, l, acc)
    _store_head_diag(acc * (1.0 / jnp.maximum(l, 1e-30)), o_ref, ts)


def _win_sample(q_rt, win_state, layer, kv_new, past_len):
    db, _, ts, hd = q_rt.shape
    kw = NSA_KV_HEADS * hd
    w_buf = win_state.shape[2]
    win2 = win_state.reshape(win_state.shape[0], db, w_buf, 2 * kw)
    return pl.pallas_call(
        functools.partial(_win_sample_kernel, ts=ts, past_len=past_len),
        grid=(db,),
        in_specs=[pl.BlockSpec((None, NSA_HEADS, ts, hd), lambda s: (s, 0, 0, 0)),
                  pl.BlockSpec((None, None, w_buf, 2 * kw), lambda s: (layer, s, 0, 0)),
                  pl.BlockSpec((None, ts, 2 * kw), lambda s: (s, 0, 0))],
        out_specs=pl.BlockSpec((None, NSA_HEADS, ts, hd), lambda s: (s, 0, 0, 0)),
        out_shape=jax.ShapeDtypeStruct((db, NSA_HEADS, ts, hd), BF16),
        compiler_params=_cparams("parallel"),
        name="nsa_win_sample",
    )(q_rt, win2, kv_new)


def _nsa_out_kernel(h_ref, oc_ref, os_ref, ow_ref, gate_ref, w_ref, out_ref):
    gates = gate_ref[...]
    acc = h_ref[...]
    for h in range(NSA_HEADS):
        g = gates[:, 3 * h:3 * h + 3]
        o = (g[:, 0:1] * oc_ref[h].astype(F32) + g[:, 1:2] * os_ref[h].astype(F32)
             + g[:, 2:3] * ow_ref[h].astype(F32))
        acc = acc + jnp.dot(o.astype(BF16), w_ref[h], preferred_element_type=F32)
    out_ref[...] = acc


def _nsa_out(h, o_c, o_s, o_w, gates, w_out_bf):
    rows, d = h.shape
    hd = NSA_HEAD_DIM
    tm = min(512, rows)
    row = lambda i: (i, 0)
    hm = pl.BlockSpec((NSA_HEADS, tm, hd), lambda i: (0, i, 0))
    w3 = w_out_bf.reshape(NSA_HEADS, hd, d)
    return pl.pallas_call(
        _nsa_out_kernel,
        grid=(rows // tm,),
        in_specs=[pl.BlockSpec((tm, d), row), hm, hm, hm, pl.BlockSpec((tm, LANES), row),
                  pl.BlockSpec(w3.shape, lambda i: (0, 0, 0))],
        out_specs=pl.BlockSpec((tm, d), row),
        out_shape=jax.ShapeDtypeStruct((rows, d), F32),
        compiler_params=_cparams("parallel"),
        name="nsa_out",
    )(h, o_c, o_s, o_w, gates, w3)


def _top_desc(s, n):
    vals = []
    cur = s
    for _ in range(n):
        m = jnp.max(cur, axis=0, keepdims=True)
        vals.append(m)
        cur = jnp.where(cur == m, MASKED, cur)
    return jnp.concatenate(vals, axis=0)


def _peer_score_kernel(h_ref, g_ref, wq_ref, sk_ref, xt_ref, s1_ref, s2_ref, f1_ref, e2_ref, tau_ref):
    n_keys = sk_ref.shape[1]
    half = sk_ref.shape[2]
    xn = _rmsnorm_rows(h_ref[...], g_ref[...])
    xt_ref[...] = xn.T.astype(BF16)
    q = jnp.dot(xn.astype(BF16), wq_ref[...], preferred_element_type=F32)
    sk1, sk2 = sk_ref[0], sk_ref[1]
    kk = PEER_TOPK
    for h in range(PEER_HEADS):
        q1 = q[:, (2 * h) * half:(2 * h + 1) * half].astype(BF16)
        q2 = q[:, (2 * h + 1) * half:(2 * h + 2) * half].astype(BF16)
        s1 = _nt_dot(sk1, q1)
        s2 = _nt_dot(sk2, q2)
        t1 = _top_desc(s1, kk)
        t2 = _top_desc(s2, kk)
        cand = [t1[0:1] + t2]
        for i in range(1, kk // 2):
            cand.append(t1[i:i + 1] + t2[0:kk // 2])
        cand.append(t1[kk // 2:] + t2[0:1])
        top = _top_desc(jnp.concatenate(cand, axis=0), kk)
        z = jnp.sum(jnp.exp(top - top[0:1]), axis=0, keepdims=True)
        f1 = jnp.exp(s1 - t1[0:1]) / z
        e2 = jnp.exp(s2 - t2[0:1])
        for tc in range(s1.shape[1] // LANES):
            lanes = slice(tc * LANES, (tc + 1) * LANES)
            s1_ref[h, tc] = s1[:, lanes]
            s2_ref[h, tc] = s2[:, lanes]
            f1_ref[h, tc] = f1[:, lanes]
            e2_ref[h, tc] = e2[:, lanes]
        tau_ref[h:h + 1, :] = top[kk - 1:kk]


def _peer_scores(h, g, wq_bf, sk_bf, tt):
    rows, d = h.shape
    n_keys = sk_bf.shape[1]
    nt = rows // tt
    tab = jax.ShapeDtypeStruct((PEER_HEADS, rows // LANES, n_keys, LANES), F32)
    tspec = pl.BlockSpec((PEER_HEADS, tt // LANES, n_keys, LANES), lambda i: (0, i, 0, 0))
    return pl.pallas_call(
        _peer_score_kernel,
        grid=(nt,),
        in_specs=[pl.BlockSpec((tt, d), lambda i: (i, 0)), pl.BlockSpec((1, d), lambda i: (0, 0)),
                  pl.BlockSpec(wq_bf.shape, lambda i: (0, 0)), pl.BlockSpec(sk_bf.shape, lambda i: (0, 0, 0))],
        out_specs=[pl.BlockSpec((d, tt), lambda i: (0, i)), tspec, tspec, tspec, tspec,
                   pl.BlockSpec((PEER_HEADS, tt), lambda i: (0, i))],
        out_shape=[jax.ShapeDtypeStruct((d, rows), BF16), tab, tab, tab, tab,
                   jax.ShapeDtypeStruct((PEER_HEADS, rows), F32)],
        compiler_params=_cparams("parallel"),
        name="peer_scores",
    )(h, g.reshape(1, d), wq_bf, sk_bf)


def _peer_expert_kernel(xt_ref, u_ref, vt_ref, s1_ref, s2_ref, f1_ref, e2_ref, tau_ref, h_ref, out_ref,
                        g_sc, acc_sc):
    c = pl.program_id(1)
    ec = u_ref.shape[0]
    n_keys = s2_ref.shape[2]
    tt = xt_ref.shape[1]
    a_per_step = ec // n_keys

    @pl.when(c == 0)
    def _():
        acc_sc[...] = jnp.zeros(acc_sc.shape, F32)

    act = _gelu(jnp.dot(u_ref[...], xt_ref[...], preferred_element_type=F32))
    for ai in range(a_per_step):
        a = c * a_per_step + ai
        for tc in range(tt // LANES):
            t0 = tc * LANES
            w = jnp.zeros((n_keys, LANES), F32)
            for h in range(PEER_HEADS):
                s1 = s1_ref[h, tc, pl.ds(a, 1), :]
                f1 = f1_ref[h, tc, pl.ds(a, 1), :]
                keep = (s1 + s2_ref[h, tc]) >= tau_ref[h:h + 1, t0:t0 + LANES]
                w = w + jnp.where(keep, f1 * e2_ref[h, tc], 0.0)
            g_sc[ai * n_keys:(ai + 1) * n_keys, t0:t0 + LANES] = (
                w * act[ai * n_keys:(ai + 1) * n_keys, t0:t0 + LANES]).astype(BF16)
    acc_sc[...] += jnp.dot(vt_ref[...], g_sc[...], preferred_element_type=F32)

    @pl.when(c == pl.num_programs(1) - 1)
    def _():
        out_ref[...] = h_ref[...] + acc_sc[...].T


def _peer_experts(h, xt, u_bf, vt_bf, s1, s2, f1, e2, tau, tt):
    rows, d = h.shape
    n_exp = u_bf.shape[0]
    n_keys = s1.shape[2]
    ec = 4 * n_keys
    tspec = pl.BlockSpec((PEER_HEADS, tt // LANES, n_keys, LANES), lambda i, c: (0, i, 0, 0))
    return pl.pallas_call(
        _peer_expert_kernel,
        grid=(rows // tt, n_exp // ec),
        in_specs=[pl.BlockSpec((d, tt), lambda i, c: (0, i)),
                  pl.BlockSpec((ec, d), lambda i, c: (c, 0)),
                  pl.BlockSpec((d, ec), lambda i, c: (0, c)),
                  tspec, tspec, tspec, tspec,
                  pl.BlockSpec((PEER_HEADS, tt), lambda i, c: (0, i)),
                  pl.BlockSpec((tt, d), lambda i, c: (i, 0))],
        out_specs=pl.BlockSpec((tt, d), lambda i, c: (i, 0)),
        out_shape=jax.ShapeDtypeStruct((rows, d), F32),
        scratch_shapes=[pltpu.VMEM((ec, tt), BF16), pltpu.VMEM((d, tt), F32)],
        compiler_params=_cparams("parallel", "arbitrary"),
        name="peer_experts",
    )(xt, u_bf, vt_bf, s1, s2, f1, e2, tau, h)


def _peer(h, g, wq_bf, sk_bf, u_bf, vt_bf):
    rows = h.shape[0]
    tt = min(512, rows)
    xt, s1, s2, f1, e2, tau = _peer_scores(h, g, wq_bf, sk_bf, tt)
    return _peer_experts(h, xt, u_bf, vt_bf, s1, s2, f1, e2, tau, tt)


def _final_norm_kernel(h_ref, g_ref, o_ref):
    o_ref[...] = _rmsnorm_rows(h_ref[...], g_ref[...])


def _final_norm(h, g):
    rows, d = h.shape
    tm = min(512, rows)
    return pl.pallas_call(
        _final_norm_kernel,
        grid=(rows // tm,),
        in_specs=[pl.BlockSpec((tm, d), lambda i: (i, 0)), pl.BlockSpec((1, d), lambda i: (0, 0))],
        out_specs=pl.BlockSpec((tm, d), lambda i: (i, 0)),
        out_shape=jax.ShapeDtypeStruct((rows, d), F32),
        compiler_params=_cparams("parallel"),
        name="final_norm",
    )(h, g.reshape(1, d))


def _to_seq_major(x_hm, db, ts):
    heads, _, hd = x_hm.shape
    return x_hm.reshape(heads, db, ts, hd).transpose(1, 0, 2, 3)


def _to_head_major(x_sm):
    db, heads, ts, hd = x_sm.shape
    return x_sm.transpose(1, 0, 2, 3).reshape(heads, db * ts, hd)


def kernel(x_prompt, x_sample, cache_diff_kv, cache_nsa_cmp_kv, cache_nsa_sel_kv, state_nsa_win_kv, page_table, norm_mix_g, diff_w_in, diff_lambda, diff_subln_g, diff_w_out, nsa_w_in, nsa_cmp_pos, nsa_cmp_w1, nsa_cmp_b1, nsa_cmp_w2, nsa_cmp_b2, nsa_w_out, norm_ffn_g, peer_wq, peer_subkeys, peer_u, peer_v, final_norm_g):
    batch, seq, d = x_prompt.shape
    db, ts, _ = x_sample.shape
    depth = norm_mix_g.shape[0]
    past_len = page_table.shape[1] * cache_diff_kv.shape[2]
    assert past_len % NSA_BLOCK == 0 and ts < NSA_BLOCK and seq % NSA_BLOCK == 0
    assert state_nsa_win_kv.shape[2] == min(NSA_WINDOW, past_len)

    pos_p = jnp.arange(seq)
    pos_s = jnp.tile(past_len + jnp.arange(ts), db)
    tab_p = _rope_tables(pos_p, DA_HEAD_DIM)
    tab_s = _rope_tables(pos_s, DA_HEAD_DIM)

    hp = x_prompt.reshape(batch * seq, d)
    hs = x_sample.reshape(db * ts, d)
    outs = {k: [] for k in ("diff_p", "diff_s", "cmp_p", "cmp_s", "sel_p", "sel_s", "win_p", "win_s")}

    for i in range(depth):
        g_mix = norm_mix_g[i]
        if i % 2 == 0:
            a = i // 2
            lam_init = 0.8 - 0.6 * math.exp(-0.3 * i)
            w_in = diff_w_in[a].astype(BF16)
            w_out = diff_w_out[a].astype(BF16)
            qp, kvp, kbp, vbp = _diff_proj(hp, g_mix, w_in, tab_p, seq)
            qs, kvs, _, _ = _diff_proj(hs, g_mix, w_in, tab_s, db * ts)
            op = _diff_attn_prompt(qp, kbp, vbp, diff_lambda[a], diff_subln_g[a], batch, seq, lam_init)
            os_ = _diff_attn_sample(qs.reshape(db, ts, d), cache_diff_kv, a, page_table,
                                    kvs.reshape(db, ts, 2 * d), diff_lambda[a], diff_subln_g[a], lam_init)
            hp = _outproj(hp, op, w_out)
            hs = _outproj(hs, os_.reshape(db * ts, d), w_out)
            outs["diff_p"].append(kvp.reshape(batch, seq, 2, DA_HEADS, 2 * DA_HEAD_DIM))
            outs["diff_s"].append(kvs.reshape(db, ts, 2, DA_HEADS, 2 * DA_HEAD_DIM))
        else:
            b = i // 2
            n_main = d + 3 * 2 * NSA_KV_HEADS * NSA_HEAD_DIM
            w_main = nsa_w_in[b][:, :n_main].astype(BF16)
            n_gate = nsa_w_in.shape[2] - n_main
            w_gate = jnp.pad(nsa_w_in[b][:, n_main:], ((0, 0), (0, LANES - n_gate))).astype(BF16)
            w_out = nsa_w_out[b].astype(BF16)
            cw = _compress_weights(nsa_cmp_pos[b], nsa_cmp_w1[b], nsa_cmp_b1[b], nsa_cmp_w2[b], nsa_cmp_b2[b])
            kv_shape = (2, NSA_KV_HEADS, NSA_HEAD_DIM)
            (qpl, qrt, kvc, kvs_, kvw, ks, vs, kw, vw, gates) = _nsa_proj(hp, g_mix, w_main, w_gate, tab_p, seq)
            kc, vc = _compress_prompt(kvc, cw, batch, seq)
            o_c, sel = _cmp_sel_prompt(qpl, kc, vc, batch, seq)
            o_s, o_w = _sel_win_prompt(qrt, ks, vs, kw, vw, sel, batch, seq)
            hp = _nsa_out(hp, o_c, o_s, o_w, gates, w_out)
            outs["cmp_p"].append(kvc.reshape((batch, seq) + kv_shape))
            outs["sel_p"].append(kvs_.reshape((batch, seq) + kv_shape))
            w_keep = min(NSA_WINDOW, seq)
            outs["win_p"].append(kvw.reshape((batch, seq) + kv_shape)[:, seq - w_keep:])
            (qpl, qrt, kvc, kvs_, kvw, _, _, _, _, gates) = _nsa_proj(hs, g_mix, w_main, w_gate, tab_s, db * ts)
            kc, vc = _compress_sample(cache_nsa_cmp_kv, b, page_table, cw)
            o_c, sel = _cmp_sel_sample(_to_seq_major(qpl, db, ts), kc, vc, past_len)
            qrt_sm = _to_seq_major(qrt, db, ts)
            kvw3 = kvw.reshape(db, ts, -1)
            o_s = _sel_sample(qrt_sm, sel, cache_nsa_sel_kv, b, page_table, kvs_.reshape(db, ts, -1), past_len)
            o_w = _win_sample(qrt_sm, state_nsa_win_kv, b, kvw3, past_len)
            hs = _nsa_out(hs, _to_head_major(o_c), _to_head_major(o_s), _to_head_major(o_w), gates, w_out)
            outs["cmp_s"].append(kvc.reshape((db, ts) + kv_shape))
            outs["sel_s"].append(kvs_.reshape((db, ts) + kv_shape))
            win_all = jnp.concatenate([state_nsa_win_kv[b], kvw.reshape((db, ts) + kv_shape)], axis=1)
            outs["win_s"].append(win_all[:, win_all.shape[1] - state_nsa_win_kv.shape[2]:])
        g_ffn = norm_ffn_g[i]
        wq = peer_wq[i].astype(BF16)
        sk = peer_subkeys[i].astype(BF16)
        u_bf = peer_u[i].astype(BF16)
        vt_bf = peer_v[i].T.astype(BF16)
        hp = _peer(hp, g_ffn, wq, sk, u_bf, vt_bf)
        hs = _peer(hs, g_ffn, wq, sk, u_bf, vt_bf)

    y_prompt = _final_norm(hp, final_norm_g).reshape(batch, seq, d)
    y_sample = _final_norm(hs, final_norm_g).reshape(db, ts, d)
    stack = lambda k: jnp.stack(outs[k])
    return (y_prompt, y_sample, stack("diff_p"), stack("diff_s"), stack("cmp_p"), stack("cmp_s"),
            stack("sel_p"), stack("sel_s"), stack("win_p"), stack("win_s"))
```

```python
import functools
import math

import jax
import jax.numpy as jnp
from jax import lax
from jax.experimental import pallas as pl
from jax.experimental.pallas import tpu as pltpu

F32 = jnp.float32
BF16 = jnp.bfloat16

NORM_EPS = 1e-6
ROPE_THETA = 500000.0
ROPE_FRACTION = 4
NEG_INF = -1e30
MASKED = -3.0e38
NO_KEEP = 3.0e38

DA_HEADS = 8
DA_HEAD_DIM = 64
NSA_HEADS = 16
NSA_KV_HEADS = 4
NSA_GROUP = NSA_HEADS // NSA_KV_HEADS
NSA_HEAD_DIM = 64
NSA_BLOCK = 64
NSA_TOPN = 16
NSA_WINDOW = 512
NSA_FORCE_SCORE = 1e4
NSA_V_CHUNK = 256
PEER_HEADS = 8
PEER_TOPK = 16

LANES = 128
VMEM_LIMIT_BYTES = 56 * 1024 * 1024


def _cparams(*sem):
    return pltpu.CompilerParams(dimension_semantics=tuple(sem), vmem_limit_bytes=VMEM_LIMIT_BYTES)


def _dot(a, b):
    return jnp.dot(a, b, preferred_element_type=F32)


def _nt_dot(a, b):
    return lax.dot_general(a, b, (((1,), (1,)), ((), ())), preferred_element_type=F32)


def _tn_dot(a, b):
    return lax.dot_general(a, b, (((0,), (0,)), ((), ())), preferred_element_type=F32)


def _rmsnorm_rows(x, g):
    ms = jnp.mean(x * x, axis=-1, keepdims=True)
    return x * lax.rsqrt(ms + NORM_EPS) * g


def _gelu(x):
    return 0.5 * x * (1.0 + lax.erf(x * (2.0 ** -0.5)))


def _rope_tables(pos, head_dim):
    d_rot = head_dim // ROPE_FRACTION
    half = d_rot // 2
    inv_freq = ROPE_THETA ** (-jnp.arange(half, dtype=F32) / half)
    ang = pos.astype(F32)[:, None] * inv_freq[None, :]
    cos, sin = jnp.cos(ang), jnp.sin(ang)
    n = pos.shape[0]
    zeros = lambda w: jnp.zeros((n, w), F32)
    c = jnp.concatenate([cos, cos, jnp.ones((n, head_dim - d_rot), F32)], axis=1)
    sa = jnp.concatenate([-sin, zeros(head_dim - half)], axis=1)
    sb = jnp.concatenate([zeros(half), sin, zeros(head_dim - d_rot)], axis=1)
    rep = LANES // head_dim
    return jnp.tile(c, (1, rep)), jnp.tile(sa, (1, rep)), jnp.tile(sb, (1, rep)), half


def _rope_cols(y, c, sa, sb, half):
    outs = []
    for j in range(y.shape[1] // LANES):
        ch = y[:, j * LANES:(j + 1) * LANES]
        outs.append(ch * c + pltpu.roll(ch, LANES - half, 1) * sa + pltpu.roll(ch, half, 1) * sb)
    return outs[0] if len(outs) == 1 else jnp.concatenate(outs, axis=1)


def _store_lane_chunks(ref, xt):
    width = ref.shape[2]
    for c in range(xt.shape[1] // width):
        ref[c] = xt[:, c * width:(c + 1) * width]


def _diff_proj_kernel(x_ref, g_ref, w_ref, c_ref, sa_ref, sb_ref,
                      qt_ref, kv_ref, kb_ref, vt_ref, *, half, scale):
    d = x_ref.shape[1]
    xn = _rmsnorm_rows(x_ref[...], g_ref[...]).astype(BF16)
    y = _dot(xn, w_ref[...])
    c, sa, sb = c_ref[...], sa_ref[...], sb_ref[...]
    q = _rope_cols(y[:, :d], c, sa, sb, half) * scale
    k = _rope_cols(y[:, d:2 * d], c, sa, sb, half)
    v = y[:, 2 * d:]
    qt_ref[...] = q.T.astype(BF16)
    kv_ref[:, :d] = k
    kv_ref[:, d:] = v
    kb_ref[...] = k.astype(BF16)
    _store_lane_chunks(vt_ref, v.T.astype(BF16))


def _key_chunk(rows):
    return min(512, rows)


def _diff_proj(x, g, w_bf, tables, period_rows):
    rows, d = x.shape
    c, sa, sb, half = tables
    tm = min(512, rows)
    nper = period_rows // tm
    cw = _key_chunk(tm)
    row = lambda i: (i, 0)
    tab = pl.BlockSpec((tm, LANES), lambda i: (i % nper, 0))
    return pl.pallas_call(
        functools.partial(_diff_proj_kernel, half=half, scale=DA_HEAD_DIM ** -0.5),
        grid=(rows // tm,),
        in_specs=[pl.BlockSpec((tm, d), row), pl.BlockSpec((1, d), lambda i: (0, 0)),
                  pl.BlockSpec((d, 3 * d), lambda i: (0, 0)), tab, tab, tab],
        out_specs=[pl.BlockSpec((d, tm), lambda i: (0, i)), pl.BlockSpec((tm, 2 * d), row),
                   pl.BlockSpec((tm, d), row), pl.BlockSpec((tm // cw, d, cw), lambda i: (i, 0, 0))],
        out_shape=[jax.ShapeDtypeStruct((d, rows), BF16), jax.ShapeDtypeStruct((rows, 2 * d), F32),
                   jax.ShapeDtypeStruct((rows, d), BF16), jax.ShapeDtypeStruct((rows // cw, d, cw), BF16)],
        compiler_params=_cparams("parallel"),
        name="diff_proj",
    )(x, g.reshape(1, d), w_bf, c, sa, sb)


def _diff_lambda(lam_ref, lam_init):
    lv = lam_ref[...]
    a = jnp.sum(lv[0:1, :] * lv[1:2, :], axis=-1, keepdims=True)
    b = jnp.sum(lv[2:3, :] * lv[3:4, :], axis=-1, keepdims=True)
    return jnp.exp(a) - jnp.exp(b) + lam_init


def _subln(o, g, lam_init):
    ms = jnp.mean(o * o, axis=-1, keepdims=True)
    return o * lax.rsqrt(ms + NORM_EPS) * g * (1.0 - lam_init)


def _online_update(s, mask, v, m, l, acc):
    if mask is not None:
        s = jnp.where(mask, s, NEG_INF)
    m_new = jnp.maximum(m, jnp.max(s, axis=-1, keepdims=True))
    alpha = jnp.exp(m - m_new)
    p = jnp.exp(s - m_new)
    if mask is not None:
        p = jnp.where(mask, p, 0.0)
    l_new = alpha * l + jnp.sum(p, axis=-1, keepdims=True)
    acc_new = alpha * acc + _dot(p.astype(BF16), v)
    return m_new, l_new, acc_new


def _online_update_t(st, mask, vt, m, l, acc):
    if mask is not None:
        st = jnp.where(mask, st, NEG_INF)
    m_new = jnp.maximum(m, jnp.max(st, axis=0, keepdims=True))
    alpha = jnp.exp(m - m_new)
    p = jnp.exp(st - m_new)
    if mask is not None:
        p = jnp.where(mask, p, 0.0)
    l_new = alpha * l + jnp.sum(p, axis=0, keepdims=True)
    acc_new = alpha * acc + _dot(vt, p.astype(BF16))
    return m_new, l_new, acc_new


def _diff_attn_kernel(qt_ref, k_ref, vt_ref, lam_ref, g_ref, o_ref, *, tq, tk, lam_init):
    i = pl.program_id(2)
    hd2 = qt_ref.shape[0]
    qt = qt_ref[...]
    comp = lax.broadcasted_iota(jnp.int32, (hd2, tq), 0) // (hd2 // 2)
    zero = jnp.zeros_like(qt)
    qq = jnp.concatenate([jnp.where(comp == 0, qt, zero), jnp.where(comp == 1, qt, zero)], axis=1)
    qpos = i * tq + lax.broadcasted_iota(jnp.int32, (tk, 2 * tq), 1) % tq
    krow = lax.broadcasted_iota(jnp.int32, (tk, 2 * tq), 0)

    def step(j, carry, masked):
        start = pl.multiple_of(j * tk, tk)
        st = _dot(k_ref[pl.ds(start, tk), :], qq)
        mask = ((krow + j * tk) <= qpos) if masked else None
        return _online_update_t(st, mask, vt_ref[j], *carry)

    n_full = (i * tq) // tk
    n_kv = ((i + 1) * tq + tk - 1) // tk
    carry = (jnp.full((1, 2 * tq), NEG_INF, F32), jnp.zeros((1, 2 * tq), F32), jnp.zeros((hd2, 2 * tq), F32))
    carry = lax.fori_loop(0, n_full, lambda j, c: step(j, c, False), carry)
    m, l, acc = lax.fori_loop(n_full, n_kv, lambda j, c: step(j, c, True), carry)
    lam = _diff_lambda(lam_ref, lam_init)
    inv = 1.0 / jnp.maximum(l, 1e-30)
    o = acc[:, :tq] * inv[:, :tq] - lam * (acc[:, tq:] * inv[:, tq:])
    ms = jnp.mean(o * o, axis=0, keepdims=True)
    o_ref[...] = (o * lax.rsqrt(ms + NORM_EPS) * g_ref[...] * (1.0 - lam_init)).astype(BF16)


def _diff_attn_prompt(qt, kb, vtc, lam_vec, subln_g, batch, seq, lam_init):
    d, rows = qt.shape
    hd2 = 2 * DA_HEAD_DIM
    tq = min(512, seq)
    tk = vtc.shape[2]
    nq = seq // tq
    return pl.pallas_call(
        functools.partial(_diff_attn_kernel, tq=tq, tk=tk, lam_init=lam_init),
        grid=(batch, DA_HEADS, nq),
        in_specs=[pl.BlockSpec((hd2, tq), lambda b, h, i: (h, b * nq + i)),
                  pl.BlockSpec((seq, hd2), lambda b, h, i: (b, h)),
                  pl.BlockSpec((seq // tk, hd2, tk), lambda b, h, i: (b, h, 0)),
                  pl.BlockSpec(lam_vec.shape, lambda b, h, i: (0, 0)),
                  pl.BlockSpec((hd2, 1), lambda b, h, i: (0, 0))],
        out_specs=pl.BlockSpec((hd2, tq), lambda b, h, i: (h, b * nq + i)),
        out_shape=jax.ShapeDtypeStruct((d, rows), BF16),
        compiler_params=_cparams("parallel", "parallel", "parallel"),
        name="diff_attn_prompt",
    )(qt, kb, vtc, lam_vec, subln_g.reshape(hd2, 1))


def _diff_dec_kernel(pt_ref, q_ref, *rest, n_pg, ts, lam_init):
    page_refs = rest[:n_pg]
    kvn_ref, lam_ref, g_ref, o_ref, qq_sc, m_sc, l_sc, acc_sc = rest[n_pg:]
    j = pl.program_id(1)
    hd2 = 2 * DA_HEAD_DIM
    grp = 2 * ts
    page = page_refs[0].shape[0] // (2 * DA_HEADS)
    stride = 2 * DA_HEADS

    @pl.when(j == 0)
    def _():
        q = q_ref[...]
        comp = lax.broadcasted_iota(jnp.int32, (ts, hd2), 1) // DA_HEAD_DIM
        for h in range(DA_HEADS):
            qh = q[:, h * hd2:(h + 1) * hd2]
            zero = jnp.zeros_like(qh)
            qq_sc[h * grp:h * grp + ts, :] = jnp.where(comp == 0, qh, zero)
            qq_sc[h * grp + ts:(h + 1) * grp, :] = jnp.where(comp == 1, qh, zero)
        m_sc[...] = jnp.full(m_sc.shape, NEG_INF, F32)
        l_sc[...] = jnp.zeros(l_sc.shape, F32)
        acc_sc[...] = jnp.zeros(acc_sc.shape, F32)

    def attend(loaders, mask, m, l, acc):
        s = jnp.concatenate(
            [jnp.concatenate([_nt_dot(qq_sc[h * grp:(h + 1) * grp, :], load_k(h)) for h in range(DA_HEADS)], axis=0)
             for load_k, _ in loaders], axis=1)
        if mask is not None:
            s = jnp.where(mask, s, NEG_INF)
        m_new = jnp.maximum(m, jnp.max(s, axis=-1, keepdims=True))
        alpha = jnp.exp(m - m_new)
        p = jnp.exp(s - m_new)
        if mask is not None:
            p = jnp.where(mask, p, 0.0)
        l_new = alpha * l + jnp.sum(p, axis=-1, keepdims=True)
        pb = p.astype(BF16)
        pv = None
        for u, (_, load_v) in enumerate(loaders):
            part = jnp.concatenate([_dot(pb[h * grp:(h + 1) * grp, u * LANES:(u + 1) * LANES], load_v(h))
                                    for h in range(DA_HEADS)], axis=0)
            pv = part if pv is None else pv + part
        return m_new, l_new, alpha * acc + pv

    assert page == LANES
    loaders = [(lambda h, ref=ref: ref[pl.ds(h, page, stride=stride), :].astype(BF16),
                lambda h, ref=ref: ref[pl.ds(DA_HEADS + h, page, stride=stride), :].astype(BF16))
               for ref in page_refs]
    m, l, acc = attend(loaders, None, m_sc[...], l_sc[...], acc_sc[...])
    m_sc[...], l_sc[...], acc_sc[...] = m, l, acc

    @pl.when(j == pl.num_programs(1) - 1)
    def _():
        kvn = kvn_ref[...]
        d = DA_HEADS * hd2
        pad = jnp.zeros((LANES - ts, hd2), F32)
        load_k = lambda h: jnp.concatenate([kvn[:, h * hd2:(h + 1) * hd2], pad], axis=0).astype(BF16)
        load_v = lambda h: jnp.concatenate([kvn[:, d + h * hd2:d + (h + 1) * hd2], pad], axis=0).astype(BF16)
        rows = DA_HEADS * grp
        qi = lax.broadcasted_iota(jnp.int32, (rows, LANES), 0) % ts
        col = lax.broadcasted_iota(jnp.int32, (rows, LANES), 1)
        mf, lf, af = attend([(load_k, load_v)], col <= qi, m_sc[...], l_sc[...], acc_sc[...])
        lam = _diff_lambda(lam_ref, lam_init)
        af = af * (1.0 / jnp.maximum(lf, 1e-30))
        for h in range(DA_HEADS):
            o = af[h * grp:h * grp + ts] - lam * af[h * grp + ts:(h + 1) * grp]
            o_ref[:, h * hd2:(h + 1) * hd2] = _subln(o, g_ref[...], lam_init).astype(BF16)


def _diff_attn_sample(q, cache, layer, page_table, kv_new, lam_vec, subln_g, lam_init):
    db, ts, d = q.shape
    n_pages = page_table.shape[1]
    page = cache.shape[2]
    hd2 = 2 * DA_HEAD_DIM
    n_pg = 4 if n_pages % 4 == 0 else 1
    cache2 = cache.reshape(cache.shape[0], cache.shape[1], page * 2 * DA_HEADS, hd2)
    rows = 2 * DA_HEADS * ts

    def page_spec(u):
        return pl.BlockSpec((None, None, page * 2 * DA_HEADS, hd2),
                            lambda s, j, pt: (layer, pt[s, j * n_pg + u], 0, 0))

    gs = pltpu.PrefetchScalarGridSpec(
        num_scalar_prefetch=1,
        grid=(db, n_pages // n_pg),
        in_specs=[pl.BlockSpec((None, ts, d), lambda s, j, pt: (s, 0, 0))]
                 + [page_spec(u) for u in range(n_pg)]
                 + [pl.BlockSpec((None, ts, 2 * d), lambda s, j, pt: (s, 0, 0)),
                    pl.BlockSpec(lam_vec.shape, lambda s, j, pt: (0, 0)),
                    pl.BlockSpec((1, hd2), lambda s, j, pt: (0, 0))],
        out_specs=pl.BlockSpec((None, ts, d), lambda s, j, pt: (s, 0, 0)),
        scratch_shapes=[pltpu.VMEM((rows, hd2), BF16), pltpu.VMEM((rows, 1), F32),
                        pltpu.VMEM((rows, 1), F32), pltpu.VMEM((rows, hd2), F32)],
    )
    return pl.pallas_call(
        functools.partial(_diff_dec_kernel, n_pg=n_pg, ts=ts, lam_init=lam_init),
        grid_spec=gs,
        out_shape=jax.ShapeDtypeStruct((db, ts, d), BF16),
        compiler_params=_cparams("parallel", "arbitrary"),
        name="diff_attn_sample",
    )(page_table, q, *([cache2] * n_pg), kv_new, lam_vec, subln_g.reshape(1, -1))


def _outproj_kernel(h_ref, ot_ref, w_ref, out_ref):
    out_ref[...] = h_ref[...] + _tn_dot(ot_ref[...], w_ref[...])


def _outproj(h, ot, w_bf):
    rows, d = h.shape
    tm = min(512, rows)
    row = lambda i: (i, 0)
    return pl.pallas_call(
        _outproj_kernel,
        grid=(rows // tm,),
        in_specs=[pl.BlockSpec((tm, d), row), pl.BlockSpec((ot.shape[0], tm), lambda i: (0, i)),
                  pl.BlockSpec(w_bf.shape, lambda i: (0, 0))],
        out_specs=pl.BlockSpec((tm, d), row),
        out_shape=jax.ShapeDtypeStruct((rows, d), F32),
        compiler_params=_cparams("parallel"),
        name="outproj",
    )(h, ot, w_bf)


def _nsa_proj_kernel(x_ref, g_ref, w_ref, wg_ref, c_ref, sa_ref, sb_ref,
                     qpl_ref, qrt_ref, kvc_ref, kvs_ref, kvw_ref, ks_ref, vst_ref, kw_ref, vwt_ref, gate_ref,
                     *, half, scale):
    d = x_ref.shape[1]
    hd = NSA_HEAD_DIM
    kw = NSA_KV_HEADS * hd
    xn = _rmsnorm_rows(x_ref[...], g_ref[...]).astype(BF16)
    y = _dot(xn, w_ref[...])
    gl = _dot(xn, wg_ref[...])
    gate_ref[...] = (1.0 / (1.0 + jnp.exp(-gl))).T
    c, sa, sb = c_ref[...], sa_ref[...], sb_ref[...]
    q = y[:, :d]
    qpl_ref[...] = (q * scale).T.astype(BF16)
    qrt_ref[...] = (_rope_cols(q, c, sa, sb, half) * scale).T.astype(BF16)
    kvc_ref[...] = y[:, d:d + 2 * kw]
    off = d + 2 * kw
    for kv_ref, k_ref, vt_ref in ((kvs_ref, ks_ref, vst_ref), (kvw_ref, kw_ref, vwt_ref)):
        k = _rope_cols(y[:, off:off + kw], c, sa, sb, half)
        v = y[:, off + kw:off + 2 * kw]
        kv_ref[:, :kw] = k
        kv_ref[:, kw:] = v
        kb = k.astype(BF16)
        for h in range(NSA_KV_HEADS):
            k_ref[h] = kb[:, h * hd:(h + 1) * hd]
        _store_lane_chunks(vt_ref, v.T.astype(BF16))
        off += 2 * kw


def _nsa_proj(x, g, w_bf, wg_bf, tables, period_rows):
    rows, d = x.shape
    c, sa, sb, half = tables
    tm = min(512, rows)
    nper = period_rows // tm
    hd = NSA_HEAD_DIM
    kw = NSA_KV_HEADS * hd
    row = lambda i: (i, 0)
    col = lambda i: (0, i)
    tab = pl.BlockSpec((tm, LANES), lambda i: (i % nper, 0))
    const = lambda i: (0, 0)
    qt = jax.ShapeDtypeStruct((d, rows), BF16)
    khm = jax.ShapeDtypeStruct((NSA_KV_HEADS, rows, hd), BF16)
    cw = min(NSA_V_CHUNK, tm)
    vtc = jax.ShapeDtypeStruct((rows // cw, kw, cw), BF16)
    kvf = jax.ShapeDtypeStruct((rows, 2 * kw), F32)
    khm_spec = pl.BlockSpec((NSA_KV_HEADS, tm, hd), lambda i: (0, i, 0))
    vtc_spec = pl.BlockSpec((tm // cw, kw, cw), lambda i: (i, 0, 0))
    kv_spec = pl.BlockSpec((tm, 2 * kw), row)
    return pl.pallas_call(
        functools.partial(_nsa_proj_kernel, half=half, scale=hd ** -0.5),
        grid=(rows // tm,),
        in_specs=[pl.BlockSpec((tm, d), row), pl.BlockSpec((1, d), const),
                  pl.BlockSpec(w_bf.shape, const), pl.BlockSpec(wg_bf.shape, const), tab, tab, tab],
        out_specs=[pl.BlockSpec((d, tm), col), pl.BlockSpec((d, tm), col), kv_spec, kv_spec, kv_spec,
                   khm_spec, vtc_spec, khm_spec, vtc_spec, pl.BlockSpec((LANES, tm), col)],
        out_shape=[qt, qt, kvf, kvf, kvf, khm, vtc, khm, vtc, jax.ShapeDtypeStruct((LANES, rows), F32)],
        compiler_params=_cparams("parallel"),
        name="nsa_proj",
    )(x, g.reshape(1, d), w_bf, wg_bf, c, sa, sb)


def _compress_weights(cmp_pos, w1, b1, w2, b2):
    kvh, hd, blk = NSA_KV_HEADS, NSA_HEAD_DIM, NSA_BLOCK
    hid = w1.shape[-1]
    eye = jnp.eye(kvh, dtype=F32)
    eye2 = jnp.eye(2, dtype=F32)
    pos_rep = jnp.broadcast_to(cmp_pos[:, :, None, :], (blk, 2, kvh, hd)).reshape(blk, 2 * kvh * hd)
    w1bd = jnp.einsum('crde,kl->rckdle', w1, eye).reshape(blk, 2, kvh * hd, kvh * hid).astype(BF16)
    b1_rep = jnp.broadcast_to(b1[:, None, :], (2, kvh, hid)).reshape(1, 2 * kvh * hid)
    w2bd = jnp.einsum('aed,ab,kl->akebld', w2, eye2, eye).reshape(2 * kvh * hid, 2 * kvh * hd).astype(BF16)
    b2_rep = jnp.broadcast_to(b2[:, None, :], (2, kvh, hd)).reshape(1, 2 * kvh * hd)
    token_major = (pos_rep, w1bd, b1_rep, w2bd, b2_rep)
    pos_t = jnp.tile(cmp_pos.transpose(1, 2, 0), (1, 1, 2))
    w1_t = jnp.einsum('crde,ab->cdarbe', w1, eye2).reshape(2, hd, 2 * blk, 2 * hid).astype(BF16)
    b1_t = jnp.tile(b1, (1, 2)).reshape(2, 1, 2 * hid)
    w2_t = jnp.einsum('ced,ab->caebd', w2, eye2).reshape(2, 2 * hid, 2 * hd).astype(BF16)
    b2_t = jnp.tile(b2, (1, 2)).reshape(2, 1, 2 * hd)
    feature_major = (pos_t, w1_t, b1_t, w2_t, b2_t)
    return token_major, feature_major


def _strided_token_rows(ref, r, n_blk, chunks):
    parts = [ref[pl.ds(r * chunks + q, n_blk, stride=NSA_BLOCK * chunks), :] for q in range(chunks)]
    return jnp.concatenate(parts, axis=1)


def _compress_prompt_kernel(x_ref, pos_ref, w1_ref, b1_ref, w2_ref, b2_ref, kc_ref, vct_ref):
    chunks = pos_ref.shape[1] // LANES
    n_blk = x_ref.shape[0] // (NSA_BLOCK * chunks)
    half = w1_ref.shape[2]
    acc = [jnp.zeros((n_blk, half), F32), jnp.zeros((n_blk, half), F32)]
    for r in range(NSA_BLOCK):
        xr = (_strided_token_rows(x_ref, r, n_blk, chunks) + pos_ref[r:r + 1, :]).astype(BF16)
        for c in range(2):
            acc[c] = acc[c] + _dot(xr[:, c * half:(c + 1) * half], w1_ref[r, c])
    hid = _gelu(jnp.concatenate(acc, axis=1) + b1_ref[...]).astype(BF16)
    out = _dot(hid, w2_ref[...]) + b2_ref[...]
    hd = NSA_HEAD_DIM
    kb = out[:, :half].astype(BF16)
    for h in range(NSA_KV_HEADS):
        kc_ref[h] = kb[:, h * hd:(h + 1) * hd]
    vct_ref[...] = out[:, half:].T.astype(BF16)


def _compress_prompt(kv_c, cw, batch, seq):
    rows, w = kv_c.shape
    nc = seq // NSA_BLOCK
    chunks = w // LANES
    kw = NSA_KV_HEADS * NSA_HEAD_DIM
    full = lambda a: pl.BlockSpec(a.shape, lambda b: (0,) * a.ndim)
    return pl.pallas_call(
        _compress_prompt_kernel,
        grid=(batch,),
        in_specs=[pl.BlockSpec((seq * chunks, LANES), lambda b: (b, 0))] + [full(a) for a in cw],
        out_specs=[pl.BlockSpec((None, NSA_KV_HEADS, nc, NSA_HEAD_DIM), lambda b: (b, 0, 0, 0)),
                   pl.BlockSpec((None, kw, nc), lambda b: (b, 0, 0))],
        out_shape=[jax.ShapeDtypeStruct((batch, NSA_KV_HEADS, nc, NSA_HEAD_DIM), BF16),
                   jax.ShapeDtypeStruct((batch, kw, nc), BF16)],
        compiler_params=_cparams("parallel"),
        name="nsa_compress_prompt",
    )(kv_c.reshape(rows * chunks, LANES), *cw)


def _compress_sample_kernel(pt_ref, *rest, n_pg):
    page_refs = rest[:n_pg]
    pos_ref, w1_ref, b1_ref, w2_ref, b2_ref, kc_ref, vc_ref = rest[n_pg:]
    hd, kvh = NSA_HEAD_DIM, NSA_KV_HEADS
    lanes = page_refs[0].shape[1]
    outs = []
    for c in range(2):
        acc = jnp.zeros((n_pg * kvh, w1_ref.shape[3]), F32)
        for dd in range(hd):
            xr = jnp.concatenate([ref[pl.ds(c * kvh * hd + dd, kvh, stride=hd), :] for ref in page_refs], axis=0)
            xr = (xr + pos_ref[c, dd:dd + 1, :]).astype(BF16)
            acc = acc + _dot(xr, w1_ref[c, dd])
        hid = _gelu(acc + b1_ref[c]).astype(BF16)
        outs.append((_dot(hid, w2_ref[c]) + b2_ref[c]).astype(BF16))
    kc_ref[...] = outs[0].reshape(kc_ref.shape)
    vc_ref[...] = outs[1].reshape(vc_ref.shape)


def _feature_major_pages(cache):
    l, p, page = cache.shape[:3]
    return cache.transpose(0, 1, 3, 4, 5, 2).reshape(l, p, -1, page)


def _compress_sample(cache, layer, page_table, cw):
    db, n_pages = page_table.shape
    page = cache.shape[2]
    kvh, hd = NSA_KV_HEADS, NSA_HEAD_DIM
    bpp = page // NSA_BLOCK
    assert bpp == 2
    cache_t = _feature_major_pages(cache)
    n_pg = 16 if n_pages % 16 == 0 else 1
    full = lambda a: pl.BlockSpec(a.shape, lambda s, j, pt: (0,) * a.ndim)

    def page_spec(u):
        return pl.BlockSpec((None, None, cache_t.shape[2], page), lambda s, j, pt: (layer, pt[s, j * n_pg + u], 0, 0))

    out = jax.ShapeDtypeStruct((db, n_pages, kvh, bpp * hd), BF16)
    ospec = pl.BlockSpec((None, n_pg, kvh, bpp * hd), lambda s, j, pt: (s, j, 0, 0))
    gs = pltpu.PrefetchScalarGridSpec(
        num_scalar_prefetch=1,
        grid=(db, n_pages // n_pg),
        in_specs=[page_spec(u) for u in range(n_pg)] + [full(a) for a in cw],
        out_specs=[ospec, ospec],
    )
    kc, vc = pl.pallas_call(
        functools.partial(_compress_sample_kernel, n_pg=n_pg),
        grid_spec=gs,
        out_shape=[out, out],
        compiler_params=_cparams("parallel", "parallel"),
        name="nsa_compress_sample",
    )(page_table, *([cache_t] * n_pg), *cw)
    to_blocks = lambda a: a.reshape(db, n_pages, kvh, bpp, hd).transpose(0, 2, 1, 3, 4).reshape(db, kvh, -1, hd)
    return to_blocks(kc), to_blocks(vc)


def _block_scores(imp, jblk, cur, nb):
    forced = (jblk == 0) | (jblk == cur) | (jblk == cur - 1)
    score = jnp.where(forced, NSA_FORCE_SCORE, imp)
    return jnp.where((jblk <= cur) & (jblk < nb), score, NEG_INF)


def _cmp_sel_prompt_kernel(qt_ref, kc_ref, vct_ref, oc_ref, sel_ref, score_sc, *, tq, nb):
    i = pl.program_id(1)
    nc = kc_ref.shape[1]
    nbp = sel_ref.shape[1]
    hd = NSA_HEAD_DIM
    pos = i * tq + lax.broadcasted_iota(jnp.int32, (nbp, tq), 1)
    jblk = lax.broadcasted_iota(jnp.int32, (nbp, tq), 0)
    vis = (((lax.broadcasted_iota(jnp.int32, (nc, tq), 0) + 1) * NSA_BLOCK - 1)
           <= i * tq + lax.broadcasted_iota(jnp.int32, (nc, tq), 1))
    cur = pos // NSA_BLOCK
    for k in range(NSA_KV_HEADS):
        kc = kc_ref[k]
        vct = vct_ref[k * hd:(k + 1) * hd, :]
        imp = jnp.zeros((nc, tq), F32)
        for g in range(NSA_GROUP):
            h = k * NSA_GROUP + g
            s = jnp.where(vis, _dot(kc, qt_ref[h * hd:(h + 1) * hd, :]), NEG_INF)
            m = jnp.max(s, axis=0, keepdims=True)
            e = jnp.where(vis, jnp.exp(s - m), 0.0)
            p = e / jnp.maximum(jnp.sum(e, axis=0, keepdims=True), 1e-30)
            imp = imp + p
            oc_ref[h * hd:(h + 1) * hd, :] = _dot(vct, p.astype(BF16)).astype(BF16)
        if nbp > nc:
            imp = jnp.concatenate([imp, jnp.zeros((nbp - nc, tq), F32)], axis=0)
        score = _block_scores(imp, jblk, cur, nb)
        score_sc[...] = score

        def rank_body(r, rank):
            row = score_sc[pl.ds(r, 1), :]
            ahead = (row > score) | ((row == score) & (r < jblk))
            return rank + jnp.where(ahead, 1.0, 0.0)

        rank = lax.fori_loop(0, nb, rank_body, jnp.zeros((nbp, tq), F32))
        sel_ref[k] = jnp.where((rank < NSA_TOPN) & (score > 0.5 * NEG_INF), 1.0, 0.0)


def _cmp_sel_prompt(qplt, kc, vct, batch, seq):
    d, rows = qplt.shape
    hd = NSA_HEAD_DIM
    tq = min(256, seq)
    nq = seq // tq
    nc = kc.shape[2]
    nb = -(-seq // NSA_BLOCK)
    nbp = -(-nb // LANES) * LANES
    return pl.pallas_call(
        functools.partial(_cmp_sel_prompt_kernel, tq=tq, nb=nb),
        grid=(batch, nq),
        in_specs=[pl.BlockSpec((d, tq), lambda b, i: (0, b * nq + i)),
                  pl.BlockSpec((None, NSA_KV_HEADS, nc, hd), lambda b, i: (b, 0, 0, 0)),
                  pl.BlockSpec((None,) + vct.shape[1:], lambda b, i: (b, 0, 0))],
        out_specs=[pl.BlockSpec((d, tq), lambda b, i: (0, b * nq + i)),
                   pl.BlockSpec((NSA_KV_HEADS, nbp, tq), lambda b, i: (0, 0, b * nq + i))],
        out_shape=[jax.ShapeDtypeStruct((d, rows), BF16),
                   jax.ShapeDtypeStruct((NSA_KV_HEADS, nbp, rows), F32)],
        scratch_shapes=[pltpu.VMEM((nbp, tq), F32)],
        compiler_params=_cparams("parallel", "parallel"),
        name="nsa_cmp_select_prompt",
    )(qplt, kc, vct)


def _sel_win_prompt_kernel(qt_ref, ks_ref, vst_ref, kw_ref, vwt_ref, sel_ref, os_ref, ow_ref, *, tq, tk):
    i = pl.program_id(2)
    hd = NSA_HEAD_DIM
    g = NSA_GROUP
    nbp = sel_ref.shape[0]
    tw = vwt_ref.shape[2]
    cps = tk // vst_ref.shape[2]
    q4 = jnp.concatenate([qt_ref[a * hd:(a + 1) * hd, :] for a in range(g)], axis=1)
    flags = sel_ref[...].astype(BF16)
    blk_col = lax.broadcasted_iota(jnp.int32, (tk, nbp), 1)
    blk_of_row = lax.broadcasted_iota(jnp.int32, (tk, nbp), 0) // NSA_BLOCK

    def sel_step(j, carry, diagonal):
        start = pl.multiple_of(j * tk, tk)
        st = _dot(ks_ref[pl.ds(start, tk), :], q4)
        expand = jnp.where(blk_col == blk_of_row + j * (tk // NSA_BLOCK), 1.0, 0.0).astype(BF16)
        chosen = _dot(expand, flags) > 0.5
        if diagonal:
            qpos = i * tq + lax.broadcasted_iota(jnp.int32, (tk, tq), 1)
            chosen = chosen & ((lax.broadcasted_iota(jnp.int32, (tk, tq), 0) + j * tk) <= qpos)
        mask = jnp.concatenate([chosen] * g, axis=1)
        vt = jnp.concatenate([vst_ref[j * cps + u] for u in range(cps)], axis=1)
        return _online_update_t(st, mask, vt, *carry)

    def win_step(j, carry):
        start = pl.multiple_of(j * tw, tw)
        st = _dot(kw_ref[pl.ds(start, tw), :], q4)
        dist = (i * tq + lax.broadcasted_iota(jnp.int32, (tw, tq), 1)
                - (lax.broadcasted_iota(jnp.int32, (tw, tq), 0) + j * tw))
        inside = (dist >= 0) & (dist < NSA_WINDOW)
        return _online_update_t(st, jnp.concatenate([inside] * g, axis=1), vwt_ref[j], *carry)

    def finish(carry, o_ref):
        m, l, acc = carry
        o = (acc * (1.0 / jnp.maximum(l, 1e-30))).astype(BF16)
        for a in range(g):
            o_ref[a * hd:(a + 1) * hd, :] = o[:, a * tq:(a + 1) * tq]

    init = (jnp.full((1, g * tq), NEG_INF, F32), jnp.zeros((1, g * tq), F32), jnp.zeros((hd, g * tq), F32))
    n_full = (i * tq) // tk
    n_kv = ((i + 1) * tq + tk - 1) // tk
    carry = lax.fori_loop(0, n_full, lambda j, c: sel_step(j, c, False), init)
    finish(lax.fori_loop(n_full, n_kv, lambda j, c: sel_step(j, c, True), carry), os_ref)
    first_win = jnp.maximum(i * tq - (NSA_WINDOW - 1), 0) // tw
    finish(lax.fori_loop(first_win, ((i + 1) * tq + tw - 1) // tw, win_step, init), ow_ref)


def _sel_win_prompt(qrtt, ks, vst, kw, vwt, sel, batch, seq):
    d, rows = qrtt.shape
    hd = NSA_HEAD_DIM
    cw = vst.shape[2]
    tq = min(256, seq)
    tk = _key_chunk(seq)
    assert tk % cw == 0 and seq % tk == 0
    nq = seq // tq
    nbp = sel.shape[1]
    gw = NSA_GROUP * hd
    qspec = pl.BlockSpec((gw, tq), lambda b, k, i: (k, b * nq + i))
    kspec = pl.BlockSpec((None, seq, hd), lambda b, k, i: (k, b, 0))
    vspec = pl.BlockSpec((seq // cw, hd, cw), lambda b, k, i: (b, k, 0))
    out = jax.ShapeDtypeStruct((d, rows), BF16)
    return pl.pallas_call(
        functools.partial(_sel_win_prompt_kernel, tq=tq, tk=tk),
        grid=(batch, NSA_KV_HEADS, nq),
        in_specs=[qspec, kspec, vspec, kspec, vspec,
                  pl.BlockSpec((None, nbp, tq), lambda b, k, i: (k, 0, b * nq + i))],
        out_specs=[qspec, qspec],
        out_shape=[out, out],
        compiler_params=_cparams("parallel", "parallel", "parallel"),
        name="nsa_sel_win_prompt",
    )(qrtt, ks, vst, kw, vwt, sel)


def _rows_by_head(q_ref):
    return jnp.concatenate([q_ref[h] for h in range(q_ref.shape[0])], axis=0)


def _cmp_sel_sample_kernel(q_ref, kc_ref, vc_ref, oc_ref, sel_ref, *, ts, past_len):
    nc = kc_ref.shape[1]
    nbp = sel_ref.shape[2]
    nb = -(-(past_len + ts) // NSA_BLOCK)
    gt = NSA_GROUP * ts
    q = _rows_by_head(q_ref)
    pos_g = past_len + lax.broadcasted_iota(jnp.int32, (gt, nc), 0) % ts
    vis = ((lax.broadcasted_iota(jnp.int32, (gt, nc), 1) + 1) * NSA_BLOCK - 1) <= pos_g
    pos = past_len + lax.broadcasted_iota(jnp.int32, (ts, nbp), 0)
    jblk = lax.broadcasted_iota(jnp.int32, (ts, nbp), 1)
    cur = pos // NSA_BLOCK
    for k in range(NSA_KV_HEADS):
        s = jnp.where(vis, _nt_dot(q[k * gt:(k + 1) * gt], kc_ref[k]), NEG_INF)
        m = jnp.max(s, axis=-1, keepdims=True)
        e = jnp.where(vis, jnp.exp(s - m), 0.0)
        p = e / jnp.maximum(jnp.sum(e, axis=-1, keepdims=True), 1e-30)
        o = _dot(p.astype(BF16), vc_ref[k]).astype(BF16)
        imp = p[0:ts]
        for g in range(NSA_GROUP):
            oc_ref[k * NSA_GROUP + g] = o[g * ts:(g + 1) * ts]
            if g:
                imp = imp + p[g * ts:(g + 1) * ts]
        if nbp > nc:
            imp = jnp.concatenate([imp, jnp.zeros((ts, nbp - nc), F32)], axis=1)
        score = _block_scores(imp, jblk, cur, nb)
        rank = jnp.zeros((ts, nbp), F32)
        for r in range(nb):
            col = score[:, r:r + 1]
            ahead = (col > score) | ((col == score) & (r < jblk))
            rank = rank + jnp.where(ahead, 1.0, 0.0)
        sel_ref[k] = jnp.where((rank < NSA_TOPN) & (score > 0.5 * NEG_INF), 1.0, 0.0)


def _cmp_sel_sample(q_pl, kc, vc, past_len):
    db, _, ts, hd = q_pl.shape
    nb = -(-(past_len + ts) // NSA_BLOCK)
    nbp = -(-nb // LANES) * LANES
    seqspec = lambda a: pl.BlockSpec((None,) + a.shape[1:], lambda s: (s, 0, 0, 0))
    return pl.pallas_call(
        functools.partial(_cmp_sel_sample_kernel, ts=ts, past_len=past_len),
        grid=(db,),
        in_specs=[seqspec(q_pl), seqspec(kc), seqspec(vc)],
        out_specs=[pl.BlockSpec((None, NSA_HEADS, ts, hd), lambda s: (s, 0, 0, 0)),
                   pl.BlockSpec((None, NSA_KV_HEADS, ts, nbp), lambda s: (s, 0, 0, 0))],
        out_shape=[jax.ShapeDtypeStruct((db, NSA_HEADS, ts, hd), BF16),
                   jax.ShapeDtypeStruct((db, NSA_KV_HEADS, ts, nbp), F32)],
        compiler_params=_cparams("parallel"),
        name="nsa_cmp_select_sample",
    )(q_pl, kc, vc)


def _new_token_kv(kvn, k, ts, n_rows):
    hd = NSA_HEAD_DIM
    kw = NSA_KV_HEADS * hd
    pad = jnp.zeros((n_rows - ts, hd), F32)
    kn = jnp.concatenate([kvn[:, k * hd:(k + 1) * hd], pad], axis=0).astype(BF16)
    vn = jnp.concatenate([kvn[:, kw + k * hd:kw + (k + 1) * hd], pad], axis=0).astype(BF16)
    return kn, vn


def _softmax_step(s, mask, m, l):
    s = jnp.where(mask, s, NEG_INF)
    m_new = jnp.maximum(m, jnp.max(s, axis=-1, keepdims=True))
    alpha = jnp.exp(m - m_new)
    p = jnp.where(mask, jnp.exp(s - m_new), 0.0)
    return m_new, alpha, alpha * l + jnp.sum(p, axis=-1, keepdims=True), p.astype(BF16)


def _sel_sample_kernel(pt_ref, q_ref, sel_ref, *rest, n_pg, ts, past_len):
    page_refs = rest[:n_pg]
    kvn_ref, o_ref, flag_sc, m_sc, l_sc, acc_sc = rest[n_pg:]
    j = pl.program_id(1)
    rows = NSA_HEADS * ts
    gt = NSA_GROUP * ts
    hd, kvh = NSA_HEAD_DIM, NSA_KV_HEADS
    page = page_refs[0].shape[1]
    nbp = sel_ref.shape[2]

    @pl.when(j == 0)
    def _():
        flag_sc[...] = jnp.concatenate([sel_ref[h // NSA_GROUP] for h in range(NSA_HEADS)], axis=0).astype(BF16)
        m_sc[...] = jnp.full(m_sc.shape, NEG_INF, F32)
        l_sc[...] = jnp.zeros(l_sc.shape, F32)
        acc_sc[...] = jnp.zeros(acc_sc.shape, F32)

    q = _rows_by_head(q_ref)
    flags = flag_sc[...]
    def chosen_keys(first_block, n_keys):
        blk_row = lax.broadcasted_iota(jnp.int32, (nbp, n_keys), 0)
        key_blk = lax.broadcasted_iota(jnp.int32, (nbp, n_keys), 1) // NSA_BLOCK
        expand = jnp.where(blk_row == key_blk + first_block, 1.0, 0.0).astype(BF16)
        return _dot(flags, expand) > 0.5

    s = jnp.concatenate(
        [jnp.concatenate([_dot(q[k * gt:(k + 1) * gt], ref[k * hd:(k + 1) * hd, :].astype(BF16))
                          for k in range(kvh)], axis=0) for ref in page_refs], axis=1)
    mask = chosen_keys(j * n_pg * (page // NSA_BLOCK), n_pg * page)
    m, alpha, l, p = _softmax_step(s, mask, m_sc[...], l_sc[...])
    acc = alpha * acc_sc[...]
    for u, ref in enumerate(page_refs):
        acc = acc + jnp.concatenate(
            [_nt_dot(p[k * gt:(k + 1) * gt, u * page:(u + 1) * page], ref[(kvh + k) * hd:(kvh + k + 1) * hd, :].astype(BF16))
             for k in range(kvh)], axis=0)
    m_sc[...], l_sc[...], acc_sc[...] = m, l, acc

    @pl.when(j == pl.num_programs(1) - 1)
    def _():
        kvn = kvn_ref[...]
        new = [_new_token_kv(kvn, k, ts, page) for k in range(kvh)]
        qi = lax.broadcasted_iota(jnp.int32, (rows, page), 0) % ts
        col = lax.broadcasted_iota(jnp.int32, (rows, page), 1)
        s = jnp.concatenate([_nt_dot(q[k * gt:(k + 1) * gt], new[k][0]) for k in range(kvh)], axis=0)
        mask = chosen_keys(past_len // NSA_BLOCK, page) & (col <= qi)
        mf, alpha, lf, p = _softmax_step(s, mask, m_sc[...], l_sc[...])
        pv = jnp.concatenate([_dot(p[k * gt:(k + 1) * gt], new[k][1]) for k in range(kvh)], axis=0)
        o = ((alpha * acc_sc[...] + pv) * (1.0 / jnp.maximum(lf, 1e-30))).astype(BF16)
        for h in range(NSA_HEADS):
            o_ref[h] = o[h * ts:(h + 1) * ts]


def _sel_sample(q_rt, sel, cache, layer, page_table, kv_new, past_len):
    db, _, ts, hd = q_rt.shape
    n_pages = page_table.shape[1]
    page = cache.shape[2]
    kw = NSA_KV_HEADS * hd
    nbp = sel.shape[3]
    cache_t = _feature_major_pages(cache)
    n_pg = 8 if n_pages % 8 == 0 else 1
    rows = NSA_HEADS * ts

    def page_spec(u):
        return pl.BlockSpec((None, None, 2 * kw, page), lambda s, j, pt: (layer, pt[s, j * n_pg + u], 0, 0))

    gs = pltpu.PrefetchScalarGridSpec(
        num_scalar_prefetch=1,
        grid=(db, n_pages // n_pg),
        in_specs=[pl.BlockSpec((None, NSA_HEADS, ts, hd), lambda s, j, pt: (s, 0, 0, 0)),
                  pl.BlockSpec((None, NSA_KV_HEADS, ts, nbp), lambda s, j, pt: (s, 0, 0, 0))]
                 + [page_spec(u) for u in range(n_pg)]
                 + [pl.BlockSpec((None, ts, 2 * kw), lambda s, j, pt: (s, 0, 0))],
        out_specs=pl.BlockSpec((None, NSA_HEADS, ts, hd), lambda s, j, pt: (s, 0, 0, 0)),
        scratch_shapes=[pltpu.VMEM((rows, nbp), BF16), pltpu.VMEM((rows, 1), F32),
                        pltpu.VMEM((rows, 1), F32), pltpu.VMEM((rows, hd), F32)],
    )
    return pl.pallas_call(
        functools.partial(_sel_sample_kernel, n_pg=n_pg, ts=ts, past_len=past_len),
        grid_spec=gs,
        out_shape=jax.ShapeDtypeStruct((db, NSA_HEADS, ts, hd), BF16),
        compiler_params=_cparams("parallel", "arbitrary"),
        name="nsa_sel_sample",
    )(page_table, q_rt, sel, *([cache_t] * n_pg), kv_new)


def _win_sample_kernel(q_ref, win_ref, kvn_ref, o_ref, *, ts, past_len):
    rows = NSA_HEADS * ts
    gt = NSA_GROUP * ts
    hd, kvh = NSA_HEAD_DIM, NSA_KV_HEADS
    kw = kvh * hd
    w_buf = win_ref.shape[0]
    q = _rows_by_head(q_ref)
    win = win_ref[...].astype(BF16)
    qpos = past_len + lax.broadcasted_iota(jnp.int32, (rows, w_buf), 0) % ts
    kpos = past_len - w_buf + lax.broadcasted_iota(jnp.int32, (rows, w_buf), 1)
    dist = qpos - kpos
    mask = (dist >= 0) & (dist < NSA_WINDOW) & (kpos >= 0)
    m = jnp.full((rows, 1), NEG_INF, F32)
    l = jnp.zeros((rows, 1), F32)
    s = jnp.concatenate([_nt_dot(q[k * gt:(k + 1) * gt], win[:, k * hd:(k + 1) * hd]) for k in range(kvh)], axis=0)
    m, alpha, l, p = _softmax_step(s, mask, m, l)
    acc = jnp.concatenate([_dot(p[k * gt:(k + 1) * gt], win[:, kw + k * hd:kw + (k + 1) * hd])
                           for k in range(kvh)], axis=0)
    new = [_new_token_kv(kvn_ref[...], k, ts, LANES) for k in range(kvh)]
    qi = lax.broadcasted_iota(jnp.int32, (rows, LANES), 0) % ts
    col = lax.broadcasted_iota(jnp.int32, (rows, LANES), 1)
    s = jnp.concatenate([_nt_dot(q[k * gt:(k + 1) * gt], new[k][0]) for k in range(kvh)], axis=0)
    m, alpha, l, p = _softmax_step(s, col <= qi, m, l)
    pv = jnp.concatenate([_dot(p[k * gt:(k + 1) * gt], new[k][1]) for k in range(kvh)], axis=0)
    o = ((alpha * acc + pv) * (1.0 / jnp.maximum(l, 1e-30))).astype(BF16)
    for h in range(NSA_HEADS):
        o_ref[h] = o[h * ts:(h + 1) * ts]


def _win_sample(q_rt, win_state, layer, kv_new, past_len):
    db, _, ts, hd = q_rt.shape
    kw = NSA_KV_HEADS * hd
    w_buf = win_state.shape[2]
    win2 = win_state.reshape(win_state.shape[0], db, w_buf, 2 * kw)
    return pl.pallas_call(
        functools.partial(_win_sample_kernel, ts=ts, past_len=past_len),
        grid=(db,),
        in_specs=[pl.BlockSpec((None, NSA_HEADS, ts, hd), lambda s: (s, 0, 0, 0)),
                  pl.BlockSpec((None, None, w_buf, 2 * kw), lambda s: (layer, s, 0, 0)),
                  pl.BlockSpec((None, ts, 2 * kw), lambda s: (s, 0, 0))],
        out_specs=pl.BlockSpec((None, NSA_HEADS, ts, hd), lambda s: (s, 0, 0, 0)),
        out_shape=jax.ShapeDtypeStruct((db, NSA_HEADS, ts, hd), BF16),
        compiler_params=_cparams("parallel"),
        name="nsa_win_sample",
    )(q_rt, win2, kv_new)


def _nsa_out_kernel(h_ref, oc_ref, os_ref, ow_ref, gate_ref, w_ref, out_ref, o_sc):
    hd = NSA_HEAD_DIM
    for h in range(NSA_HEADS):
        rows = slice(h * hd, (h + 1) * hd)
        o = (gate_ref[3 * h:3 * h + 1, :] * oc_ref[rows, :].astype(F32)
             + gate_ref[3 * h + 1:3 * h + 2, :] * os_ref[rows, :].astype(F32)
             + gate_ref[3 * h + 2:3 * h + 3, :] * ow_ref[rows, :].astype(F32))
        o_sc[rows, :] = o.astype(BF16)
    out_ref[...] = h_ref[...] + _tn_dot(o_sc[...], w_ref[...])


def _nsa_out(h, oct_, ost, owt, gates_t, w_out_bf):
    rows, d = h.shape
    tm = min(512, rows)
    col = lambda i: (0, i)
    ot = pl.BlockSpec((d, tm), col)
    return pl.pallas_call(
        _nsa_out_kernel,
        grid=(rows // tm,),
        in_specs=[pl.BlockSpec((tm, d), lambda i: (i, 0)), ot, ot, ot, pl.BlockSpec((LANES, tm), col),
                  pl.BlockSpec(w_out_bf.shape, lambda i: (0, 0))],
        out_specs=pl.BlockSpec((tm, d), lambda i: (i, 0)),
        out_shape=jax.ShapeDtypeStruct((rows, d), F32),
        scratch_shapes=[pltpu.VMEM((d, tm), BF16)],
        compiler_params=_cparams("parallel"),
        name="nsa_out",
    )(h, oct_, ost, owt, gates_t, w_out_bf)


def _top_desc(s, n):
    vals = []
    cur = s
    for _ in range(n):
        m = jnp.max(cur, axis=0, keepdims=True)
        vals.append(m)
        cur = jnp.where(cur == m, MASKED, cur)
    return jnp.concatenate(vals, axis=0)


def _peer_score_kernel(h_ref, g_ref, wq_ref, sk_ref, xt_ref, th_ref, f1_ref, e2_ref):
    half = sk_ref.shape[2]
    xn = _rmsnorm_rows(h_ref[...], g_ref[...])
    xt_ref[...] = xn.T.astype(BF16)
    q = _dot(xn.astype(BF16), wq_ref[...])
    sk1, sk2 = sk_ref[0], sk_ref[1]
    kk = PEER_TOPK
    for h in range(PEER_HEADS):
        q1 = q[:, (2 * h) * half:(2 * h + 1) * half].astype(BF16)
        q2 = q[:, (2 * h + 1) * half:(2 * h + 2) * half].astype(BF16)
        s1 = _nt_dot(sk1, q1)
        s2 = _nt_dot(sk2, q2)
        t1 = _top_desc(s1, kk)
        t2 = _top_desc(s2, kk)
        cand = [t1[0:1] + t2]
        for i in range(1, kk // 2):
            cand.append(t1[i:i + 1] + t2[0:kk // 2])
        cand.append(t1[kk // 2:] + t2[0:1])
        top = _top_desc(jnp.concatenate(cand, axis=0), kk)
        tau = top[kk - 1:kk]
        z = jnp.sum(jnp.exp(top - top[0:1]), axis=0, keepdims=True)
        thr = jnp.full(s1.shape, NO_KEEP, F32)
        for j in range(kk):
            t2j = t2[j:j + 1]
            thr = jnp.where((s1 + t2j) >= tau, t2j, thr)
        m2 = t2[0:1]
        th = jnp.exp(jnp.minimum(thr - m2, 1.0))
        f1 = jnp.exp(s1 - t1[0:1]) / z
        e2 = jnp.exp(s2 - m2)
        for tc in range(s1.shape[1] // LANES):
            lanes = slice(tc * LANES, (tc + 1) * LANES)
            th_ref[h, tc] = th[:, lanes]
            f1_ref[h, tc] = f1[:, lanes]
            e2_ref[h, tc] = e2[:, lanes]


def _peer_scores(h, g, wq_bf, sk_bf, tt):
    rows, d = h.shape
    n_keys = sk_bf.shape[1]
    nt = rows // tt
    tab = jax.ShapeDtypeStruct((PEER_HEADS, rows // LANES, n_keys, LANES), F32)
    tspec = pl.BlockSpec((PEER_HEADS, tt // LANES, n_keys, LANES), lambda i: (0, i, 0, 0))
    return pl.pallas_call(
        _peer_score_kernel,
        grid=(nt,),
        in_specs=[pl.BlockSpec((tt, d), lambda i: (i, 0)), pl.BlockSpec((1, d), lambda i: (0, 0)),
                  pl.BlockSpec(wq_bf.shape, lambda i: (0, 0)), pl.BlockSpec(sk_bf.shape, lambda i: (0, 0, 0))],
        out_specs=[pl.BlockSpec((d, tt), lambda i: (0, i)), tspec, tspec, tspec],
        out_shape=[jax.ShapeDtypeStruct((d, rows), BF16), tab, tab, tab],
        compiler_params=_cparams("parallel"),
        name="peer_scores",
    )(h, g.reshape(1, d), wq_bf, sk_bf)


PEER_A_PER_STEP = 8
PEER_A_PER_SUB = 2


def _peer_expert_kernel(xt_ref, u_ref, vt_ref, th_ref, f1_ref, e2_ref, h_ref, out_ref, acc_sc):
    c = pl.program_id(1)
    n_keys = e2_ref.shape[2]
    tt = xt_ref.shape[1]
    a_per_step = th_ref.shape[2]
    sub = PEER_A_PER_SUB * n_keys

    @pl.when(c == 0)
    def _():
        acc_sc[...] = jnp.zeros(acc_sc.shape, F32)

    xt = xt_ref[...]
    total = None
    for sc in range(a_per_step // PEER_A_PER_SUB):
        rows = slice(sc * sub, (sc + 1) * sub)
        act = _gelu(_dot(u_ref[rows, :], xt))
        g_parts = []
        for tc in range(tt // LANES):
            w = [jnp.zeros((n_keys, LANES), F32) for _ in range(PEER_A_PER_SUB)]
            for h in range(PEER_HEADS):
                e2 = e2_ref[h, tc]
                for u in range(PEER_A_PER_SUB):
                    a = sc * PEER_A_PER_SUB + u
                    keep = e2 >= th_ref[h, tc, a:a + 1, :]
                    w[u] = w[u] + jnp.where(keep, f1_ref[h, tc, a:a + 1, :] * e2, 0.0)
            g_parts.append(jnp.concatenate(w, axis=0) * act[:, tc * LANES:(tc + 1) * LANES])
        g = jnp.concatenate(g_parts, axis=1).astype(BF16)
        part = _dot(vt_ref[:, rows], g)
        total = part if total is None else total + part
    acc_sc[...] += total

    @pl.when(c == pl.num_programs(1) - 1)
    def _():
        out_ref[...] = h_ref[...] + acc_sc[...].T


def _peer_experts(h, xt, u_bf, vt_bf, th, f1, e2, tt):
    rows, d = h.shape
    n_exp = u_bf.shape[0]
    n_keys = e2.shape[2]
    ec = PEER_A_PER_STEP * n_keys
    e2spec = pl.BlockSpec((PEER_HEADS, tt // LANES, n_keys, LANES), lambda i, c: (0, i, 0, 0))
    aspec = pl.BlockSpec((PEER_HEADS, tt // LANES, PEER_A_PER_STEP, LANES), lambda i, c: (0, i, c, 0))
    return pl.pallas_call(
        _peer_expert_kernel,
        grid=(rows // tt, n_exp // ec),
        in_specs=[pl.BlockSpec((d, tt), lambda i, c: (0, i)),
                  pl.BlockSpec((ec, d), lambda i, c: (c, 0)),
                  pl.BlockSpec((d, ec), lambda i, c: (0, c)),
                  aspec, aspec, e2spec,
                  pl.BlockSpec((tt, d), lambda i, c: (i, 0))],
        out_specs=pl.BlockSpec((tt, d), lambda i, c: (i, 0)),
        out_shape=jax.ShapeDtypeStruct((rows, d), F32),
        scratch_shapes=[pltpu.VMEM((d, tt), F32)],
        compiler_params=_cparams("parallel", "arbitrary"),
        name="peer_experts",
    )(xt, u_bf, vt_bf, th, f1, e2, h)


def _peer(h, g, wq_bf, sk_bf, u_bf, vt_bf):
    rows = h.shape[0]
    tt = min(512, rows)
    xt, th, f1, e2 = _peer_scores(h, g, wq_bf, sk_bf, tt)
    return _peer_experts(h, xt, u_bf, vt_bf, th, f1, e2, tt)


def _final_norm_kernel(h_ref, g_ref, o_ref):
    o_ref[...] = _rmsnorm_rows(h_ref[...], g_ref[...])


def _final_norm(h, g):
    rows, d = h.shape
    tm = min(512, rows)
    return pl.pallas_call(
        _final_norm_kernel,
        grid=(rows // tm,),
        in_specs=[pl.BlockSpec((tm, d), lambda i: (i, 0)), pl.BlockSpec((1, d), lambda i: (0, 0))],
        out_specs=pl.BlockSpec((tm, d), lambda i: (i, 0)),
        out_shape=jax.ShapeDtypeStruct((rows, d), F32),
        compiler_params=_cparams("parallel"),
        name="final_norm",
    )(h, g.reshape(1, d))


def _heads_from_t(xt, db, ts, hd):
    return xt.reshape(-1, hd, db, ts).transpose(2, 0, 3, 1)


def _heads_to_t(x):
    db, heads, ts, hd = x.shape
    return x.transpose(1, 3, 0, 2).reshape(heads * hd, db * ts)


def kernel(x_prompt, x_sample, cache_diff_kv, cache_nsa_cmp_kv, cache_nsa_sel_kv, state_nsa_win_kv, page_table, norm_mix_g, diff_w_in, diff_lambda, diff_subln_g, diff_w_out, nsa_w_in, nsa_cmp_pos, nsa_cmp_w1, nsa_cmp_b1, nsa_cmp_w2, nsa_cmp_b2, nsa_w_out, norm_ffn_g, peer_wq, peer_subkeys, peer_u, peer_v, final_norm_g):
    batch, seq, d = x_prompt.shape
    db, ts, _ = x_sample.shape
    depth = norm_mix_g.shape[0]
    past_len = page_table.shape[1] * cache_diff_kv.shape[2]
    assert past_len % NSA_BLOCK == 0 and ts < NSA_BLOCK and seq % LANES == 0 and (db * ts) % LANES == 0
    assert state_nsa_win_kv.shape[2] == min(NSA_WINDOW, past_len)

    pos_p = jnp.arange(seq)
    pos_s = jnp.tile(past_len + jnp.arange(ts), db)
    tab_p = _rope_tables(pos_p, DA_HEAD_DIM)
    tab_s = _rope_tables(pos_s, DA_HEAD_DIM)

    hp = x_prompt.reshape(batch * seq, d)
    hs = x_sample.reshape(db * ts, d)
    outs = {k: [] for k in ("diff_p", "diff_s", "cmp_p", "cmp_s", "sel_p", "sel_s", "win_p", "win_s")}

    for i in range(depth):
        g_mix = norm_mix_g[i]
        if i % 2 == 0:
            a = i // 2
            lam_init = 0.8 - 0.6 * math.exp(-0.3 * i)
            w_in = diff_w_in[a].astype(BF16)
            w_out = diff_w_out[a].astype(BF16)
            qt, kvp, kb, vtc = _diff_proj(hp, g_mix, w_in, tab_p, seq)
            ot = _diff_attn_prompt(qt, kb, vtc, diff_lambda[a], diff_subln_g[a], batch, seq, lam_init)
            hp = _outproj(hp, ot, w_out)
            qt_s, kvs, _, _ = _diff_proj(hs, g_mix, w_in, tab_s, db * ts)
            os_ = _diff_attn_sample(qt_s.T.reshape(db, ts, d), cache_diff_kv, a, page_table,
                                    kvs.reshape(db, ts, 2 * d), diff_lambda[a], diff_subln_g[a], lam_init)
            hs = _outproj(hs, os_.reshape(db * ts, d).T, w_out)
            outs["diff_p"].append(kvp.reshape(batch, seq, 2, DA_HEADS, 2 * DA_HEAD_DIM))
            outs["diff_s"].append(kvs.reshape(db, ts, 2, DA_HEADS, 2 * DA_HEAD_DIM))
        else:
            b = i // 2
            hd = NSA_HEAD_DIM
            n_main = d + 3 * 2 * NSA_KV_HEADS * hd
            w_main = nsa_w_in[b][:, :n_main].astype(BF16)
            n_gate = nsa_w_in.shape[2] - n_main
            w_gate = jnp.pad(nsa_w_in[b][:, n_main:], ((0, 0), (0, LANES - n_gate))).astype(BF16)
            w_out = nsa_w_out[b].astype(BF16)
            cw_tok, cw_feat = _compress_weights(nsa_cmp_pos[b], nsa_cmp_w1[b], nsa_cmp_b1[b],
                                                nsa_cmp_w2[b], nsa_cmp_b2[b])
            kv_shape = (2, NSA_KV_HEADS, hd)
            (qplt, qrtt, kvc, kvs_, kvw, ks, vst, kw, vwt, gates_t) = _nsa_proj(hp, g_mix, w_main, w_gate, tab_p, seq)
            kc, vct = _compress_prompt(kvc, cw_tok, batch, seq)
            oct_, sel = _cmp_sel_prompt(qplt, kc, vct, batch, seq)
            ost, owt = _sel_win_prompt(qrtt, ks, vst, kw, vwt, sel, batch, seq)
            hp = _nsa_out(hp, oct_, ost, owt, gates_t, w_out)
            outs["cmp_p"].append(kvc.reshape((batch, seq) + kv_shape))
            outs["sel_p"].append(kvs_.reshape((batch, seq) + kv_shape))
            w_keep = min(NSA_WINDOW, seq)
            outs["win_p"].append(kvw.reshape((batch, seq) + kv_shape)[:, seq - w_keep:])
            (qplt, qrtt, kvc, kvs_, kvw, _, _, _, _, gates_t) = _nsa_proj(hs, g_mix, w_main, w_gate, tab_s, db * ts)
            kc, vc = _compress_sample(cache_nsa_cmp_kv, b, page_table, cw_feat)
            o_c, sel = _cmp_sel_sample(_heads_from_t(qplt, db, ts, hd), kc, vc, past_len)
            qrt_sm = _heads_from_t(qrtt, db, ts, hd)
            o_s = _sel_sample(qrt_sm, sel, cache_nsa_sel_kv, b, page_table, kvs_.reshape(db, ts, -1), past_len)
            o_w = _win_sample(qrt_sm, state_nsa_win_kv, b, kvw.reshape(db, ts, -1), past_len)
            hs = _nsa_out(hs, _heads_to_t(o_c), _heads_to_t(o_s), _heads_to_t(o_w), gates_t, w_out)
            outs["cmp_s"].append(kvc.reshape((db, ts) + kv_shape))
            outs["sel_s"].append(kvs_.reshape((db, ts) + kv_shape))
            win_all = jnp.concatenate([state_nsa_win_kv[b], kvw.reshape((db, ts) + kv_shape)], axis=1)
            outs["win_s"].append(win_all[:, win_all.shape[1] - state_nsa_win_kv.shape[2]:])
        g_ffn = norm_ffn_g[i]
        wq = peer_wq[i].astype(BF16)
        sk = peer_subkeys[i].astype(BF16)
        u_bf = peer_u[i].astype(BF16)
        vt_bf = peer_v[i].T.astype(BF16)
        hp = _peer(hp, g_ffn, wq, sk, u_bf, vt_bf)
        hs = _peer(hs, g_ffn, wq, sk, u_bf, vt_bf)

    y_prompt = _final_norm(hp, final_norm_g).reshape(batch, seq, d)
    y_sample = _final_norm(hs, final_norm_g).reshape(db, ts, d)
    stack = lambda k: jnp.stack(outs[k])
    return (y_prompt, y_sample, stack("diff_p"), stack("diff_s"), stack("cmp_p"), stack("cmp_s"),
            stack("sel_p"), stack("sel_s"), stack("win_p"), stack("win_s"))
```

```python
import functools
import math

import jax
import jax.numpy as jnp
from jax import lax
from jax.experimental import pallas as pl
from jax.experimental.pallas import tpu as pltpu

F32 = jnp.float32
BF16 = jnp.bfloat16

NORM_EPS = 1e-6
ROPE_THETA = 500000.0
ROPE_FRACTION = 4
NEG_INF = -1e30
MASKED = -3.0e38
NO_KEEP = 3.0e38

DA_HEADS = 8
DA_HEAD_DIM = 64
NSA_HEADS = 16
NSA_KV_HEADS = 4
NSA_GROUP = NSA_HEADS // NSA_KV_HEADS
NSA_HEAD_DIM = 64
NSA_BLOCK = 64
NSA_TOPN = 16
NSA_WINDOW = 512
NSA_FORCE_SCORE = 1e4
NSA_V_CHUNK = 256
PEER_HEADS = 8
PEER_TOPK = 16

LANES = 128
VMEM_LIMIT_BYTES = 56 * 1024 * 1024


def _cparams(*sem):
    return pltpu.CompilerParams(dimension_semantics=tuple(sem), vmem_limit_bytes=VMEM_LIMIT_BYTES)


def _dot(a, b):
    return jnp.dot(a, b, preferred_element_type=F32)


def _nt_dot(a, b):
    return lax.dot_general(a, b, (((1,), (1,)), ((), ())), preferred_element_type=F32)


def _tn_dot(a, b):
    return lax.dot_general(a, b, (((0,), (0,)), ((), ())), preferred_element_type=F32)


def _rmsnorm_rows(x, g):
    ms = jnp.mean(x * x, axis=-1, keepdims=True)
    return x * lax.rsqrt(ms + NORM_EPS) * g


def _gelu(x):
    return 0.5 * x * (1.0 + lax.erf(x * (2.0 ** -0.5)))


def _rope_tables(pos, head_dim):
    d_rot = head_dim // ROPE_FRACTION
    half = d_rot // 2
    inv_freq = ROPE_THETA ** (-jnp.arange(half, dtype=F32) / half)
    ang = pos.astype(F32)[:, None] * inv_freq[None, :]
    cos, sin = jnp.cos(ang), jnp.sin(ang)
    n = pos.shape[0]
    zeros = lambda w: jnp.zeros((n, w), F32)
    c = jnp.concatenate([cos, cos, jnp.ones((n, head_dim - d_rot), F32)], axis=1)
    sa = jnp.concatenate([-sin, zeros(head_dim - half)], axis=1)
    sb = jnp.concatenate([zeros(half), sin, zeros(head_dim - d_rot)], axis=1)
    rep = LANES // head_dim
    return jnp.tile(c, (1, rep)), jnp.tile(sa, (1, rep)), jnp.tile(sb, (1, rep)), half


def _rope_cols(y, c, sa, sb, half):
    outs = []
    for j in range(y.shape[1] // LANES):
        ch = y[:, j * LANES:(j + 1) * LANES]
        outs.append(ch * c + pltpu.roll(ch, LANES - half, 1) * sa + pltpu.roll(ch, half, 1) * sb)
    return outs[0] if len(outs) == 1 else jnp.concatenate(outs, axis=1)


def _store_lane_chunks(ref, xt):
    width = ref.shape[2]
    for c in range(xt.shape[1] // width):
        ref[c] = xt[:, c * width:(c + 1) * width]


def _diff_proj_kernel(x_ref, g_ref, w_ref, c_ref, sa_ref, sb_ref,
                      qt_ref, kv_ref, kb_ref, vt_ref, *, half, scale):
    d = x_ref.shape[1]
    xn = _rmsnorm_rows(x_ref[...], g_ref[...]).astype(BF16)
    y = _dot(xn, w_ref[...])
    c, sa, sb = c_ref[...], sa_ref[...], sb_ref[...]
    q = _rope_cols(y[:, :d], c, sa, sb, half) * scale
    k = _rope_cols(y[:, d:2 * d], c, sa, sb, half)
    v = y[:, 2 * d:]
    qt_ref[...] = q.T.astype(BF16)
    kv_ref[:, :d] = k
    kv_ref[:, d:] = v
    kb_ref[...] = k.astype(BF16)
    _store_lane_chunks(vt_ref, v.T.astype(BF16))


def _key_chunk(rows):
    return min(512, rows)


def _diff_proj(x, g, w_bf, tables, period_rows):
    rows, d = x.shape
    c, sa, sb, half = tables
    tm = min(512, rows)
    nper = period_rows // tm
    cw = _key_chunk(tm)
    row = lambda i: (i, 0)
    tab = pl.BlockSpec((tm, LANES), lambda i: (i % nper, 0))
    return pl.pallas_call(
        functools.partial(_diff_proj_kernel, half=half, scale=DA_HEAD_DIM ** -0.5),
        grid=(rows // tm,),
        in_specs=[pl.BlockSpec((tm, d), row), pl.BlockSpec((1, d), lambda i: (0, 0)),
                  pl.BlockSpec((d, 3 * d), lambda i: (0, 0)), tab, tab, tab],
        out_specs=[pl.BlockSpec((d, tm), lambda i: (0, i)), pl.BlockSpec((tm, 2 * d), row),
                   pl.BlockSpec((tm, d), row), pl.BlockSpec((tm // cw, d, cw), lambda i: (i, 0, 0))],
        out_shape=[jax.ShapeDtypeStruct((d, rows), BF16), jax.ShapeDtypeStruct((rows, 2 * d), F32),
                   jax.ShapeDtypeStruct((rows, d), BF16), jax.ShapeDtypeStruct((rows // cw, d, cw), BF16)],
        compiler_params=_cparams("parallel"),
        name="diff_proj",
    )(x, g.reshape(1, d), w_bf, c, sa, sb)


def _diff_lambda(lam_ref, lam_init):
    lv = lam_ref[...]
    a = jnp.sum(lv[0:1, :] * lv[1:2, :], axis=-1, keepdims=True)
    b = jnp.sum(lv[2:3, :] * lv[3:4, :], axis=-1, keepdims=True)
    return jnp.exp(a) - jnp.exp(b) + lam_init


def _subln(o, g, lam_init):
    ms = jnp.mean(o * o, axis=-1, keepdims=True)
    return o * lax.rsqrt(ms + NORM_EPS) * g * (1.0 - lam_init)


def _online_update(s, mask, v, m, l, acc):
    if mask is not None:
        s = jnp.where(mask, s, NEG_INF)
    m_new = jnp.maximum(m, jnp.max(s, axis=-1, keepdims=True))
    alpha = jnp.exp(m - m_new)
    p = jnp.exp(s - m_new)
    if mask is not None:
        p = jnp.where(mask, p, 0.0)
    l_new = alpha * l + jnp.sum(p, axis=-1, keepdims=True)
    acc_new = alpha * acc + _dot(p.astype(BF16), v)
    return m_new, l_new, acc_new


def _online_update_t(st, mask, vt, m, l, acc, some_key_visible=False):
    if mask is not None:
        st = jnp.where(mask, st, NEG_INF)
    m_new = jnp.maximum(m, jnp.max(st, axis=0, keepdims=True))
    alpha = jnp.exp(m - m_new)
    p = jnp.exp(st - m_new)
    if mask is not None and not some_key_visible:
        p = jnp.where(mask, p, 0.0)
    l_new = alpha * l + jnp.sum(p, axis=0, keepdims=True)
    acc_new = alpha * acc + _dot(vt, p.astype(BF16))
    return m_new, l_new, acc_new


def _diff_attn_kernel(qt_ref, k_ref, vt_ref, lam_ref, g_ref, o_ref, *, tq, tk, lam_init):
    i = pl.program_id(2)
    hd2 = qt_ref.shape[0]
    qt = qt_ref[...]
    comp = lax.broadcasted_iota(jnp.int32, (hd2, tq), 0) // (hd2 // 2)
    zero = jnp.zeros_like(qt)
    qq = jnp.concatenate([jnp.where(comp == 0, qt, zero), jnp.where(comp == 1, qt, zero)], axis=1)
    qpos = i * tq + lax.broadcasted_iota(jnp.int32, (tk, 2 * tq), 1) % tq
    krow = lax.broadcasted_iota(jnp.int32, (tk, 2 * tq), 0)

    def step(j, carry, masked):
        start = pl.multiple_of(j * tk, tk)
        st = _dot(k_ref[pl.ds(start, tk), :], qq)
        mask = ((krow + j * tk) <= qpos) if masked else None
        return _online_update_t(st, mask, vt_ref[j], *carry, some_key_visible=True)

    n_full = (i * tq) // tk
    n_kv = ((i + 1) * tq + tk - 1) // tk
    carry = (jnp.full((1, 2 * tq), NEG_INF, F32), jnp.zeros((1, 2 * tq), F32), jnp.zeros((hd2, 2 * tq), F32))
    carry = lax.fori_loop(0, n_full, lambda j, c: step(j, c, False), carry)
    m, l, acc = lax.fori_loop(n_full, n_kv, lambda j, c: step(j, c, True), carry)
    lam = _diff_lambda(lam_ref, lam_init)
    inv = 1.0 / jnp.maximum(l, 1e-30)
    o = acc[:, :tq] * inv[:, :tq] - lam * (acc[:, tq:] * inv[:, tq:])
    ms = jnp.mean(o * o, axis=0, keepdims=True)
    o_ref[...] = (o * lax.rsqrt(ms + NORM_EPS) * g_ref[...] * (1.0 - lam_init)).astype(BF16)


def _diff_attn_prompt(qt, kb, vtc, lam_vec, subln_g, batch, seq, lam_init):
    d, rows = qt.shape
    hd2 = 2 * DA_HEAD_DIM
    tq = min(512, seq)
    tk = vtc.shape[2]
    nq = seq // tq
    return pl.pallas_call(
        functools.partial(_diff_attn_kernel, tq=tq, tk=tk, lam_init=lam_init),
        grid=(batch, DA_HEADS, nq),
        in_specs=[pl.BlockSpec((hd2, tq), lambda b, h, i: (h, b * nq + i)),
                  pl.BlockSpec((seq, hd2), lambda b, h, i: (b, h)),
                  pl.BlockSpec((seq // tk, hd2, tk), lambda b, h, i: (b, h, 0)),
                  pl.BlockSpec(lam_vec.shape, lambda b, h, i: (0, 0)),
                  pl.BlockSpec((hd2, 1), lambda b, h, i: (0, 0))],
        out_specs=pl.BlockSpec((hd2, tq), lambda b, h, i: (h, b * nq + i)),
        out_shape=jax.ShapeDtypeStruct((d, rows), BF16),
        compiler_params=_cparams("parallel", "parallel", "parallel"),
        name="diff_attn_prompt",
    )(qt, kb, vtc, lam_vec, subln_g.reshape(hd2, 1))


def _diff_dec_kernel(pt_ref, q_ref, *rest, n_pg, ts, lam_init):
    page_refs = rest[:n_pg]
    kvn_ref, lam_ref, g_ref, o_ref, qq_sc, m_sc, l_sc, acc_sc = rest[n_pg:]
    j = pl.program_id(1)
    hd2 = 2 * DA_HEAD_DIM
    grp = 2 * ts
    page = page_refs[0].shape[0] // (2 * DA_HEADS)
    stride = 2 * DA_HEADS

    @pl.when(j == 0)
    def _():
        q = q_ref[...]
        comp = lax.broadcasted_iota(jnp.int32, (ts, hd2), 1) // DA_HEAD_DIM
        for h in range(DA_HEADS):
            qh = q[:, h * hd2:(h + 1) * hd2]
            zero = jnp.zeros_like(qh)
            qq_sc[h * grp:h * grp + ts, :] = jnp.where(comp == 0, qh, zero)
            qq_sc[h * grp + ts:(h + 1) * grp, :] = jnp.where(comp == 1, qh, zero)
        m_sc[...] = jnp.full(m_sc.shape, NEG_INF, F32)
        l_sc[...] = jnp.zeros(l_sc.shape, F32)
        acc_sc[...] = jnp.zeros(acc_sc.shape, F32)

    def attend(loaders, mask, m, l, acc):
        s = jnp.concatenate(
            [jnp.concatenate([_nt_dot(qq_sc[h * grp:(h + 1) * grp, :], load_k(h)) for h in range(DA_HEADS)], axis=0)
             for load_k, _ in loaders], axis=1)
        if mask is not None:
            s = jnp.where(mask, s, NEG_INF)
        m_new = jnp.maximum(m, jnp.max(s, axis=-1, keepdims=True))
        alpha = jnp.exp(m - m_new)
        p = jnp.exp(s - m_new)
        if mask is not None:
            p = jnp.where(mask, p, 0.0)
        l_new = alpha * l + jnp.sum(p, axis=-1, keepdims=True)
        pb = p.astype(BF16)
        pv = None
        for u, (_, load_v) in enumerate(loaders):
            part = jnp.concatenate([_dot(pb[h * grp:(h + 1) * grp, u * LANES:(u + 1) * LANES], load_v(h))
                                    for h in range(DA_HEADS)], axis=0)
            pv = part if pv is None else pv + part
        return m_new, l_new, alpha * acc + pv

    assert page == LANES
    loaders = [(lambda h, ref=ref: ref[pl.ds(h, page, stride=stride), :].astype(BF16),
                lambda h, ref=ref: ref[pl.ds(DA_HEADS + h, page, stride=stride), :].astype(BF16))
               for ref in page_refs]
    m, l, acc = attend(loaders, None, m_sc[...], l_sc[...], acc_sc[...])
    m_sc[...], l_sc[...], acc_sc[...] = m, l, acc

    @pl.when(j == pl.num_programs(1) - 1)
    def _():
        kvn = kvn_ref[...]
        d = DA_HEADS * hd2
        pad = jnp.zeros((LANES - ts, hd2), F32)
        load_k = lambda h: jnp.concatenate([kvn[:, h * hd2:(h + 1) * hd2], pad], axis=0).astype(BF16)
        load_v = lambda h: jnp.concatenate([kvn[:, d + h * hd2:d + (h + 1) * hd2], pad], axis=0).astype(BF16)
        rows = DA_HEADS * grp
        qi = lax.broadcasted_iota(jnp.int32, (rows, LANES), 0) % ts
        col = lax.broadcasted_iota(jnp.int32, (rows, LANES), 1)
        mf, lf, af = attend([(load_k, load_v)], col <= qi, m_sc[...], l_sc[...], acc_sc[...])
        lam = _diff_lambda(lam_ref, lam_init)
        af = af * (1.0 / jnp.maximum(lf, 1e-30))
        for h in range(DA_HEADS):
            o = af[h * grp:h * grp + ts] - lam * af[h * grp + ts:(h + 1) * grp]
            o_ref[:, h * hd2:(h + 1) * hd2] = _subln(o, g_ref[...], lam_init).astype(BF16)


def _diff_attn_sample(q, cache, layer, page_table, kv_new, lam_vec, subln_g, lam_init):
    db, ts, d = q.shape
    n_pages = page_table.shape[1]
    page = cache.shape[2]
    hd2 = 2 * DA_HEAD_DIM
    n_pg = 4 if n_pages % 4 == 0 else 1
    cache2 = cache.reshape(cache.shape[0], cache.shape[1], page * 2 * DA_HEADS, hd2)
    rows = 2 * DA_HEADS * ts

    def page_spec(u):
        return pl.BlockSpec((None, None, page * 2 * DA_HEADS, hd2),
                            lambda s, j, pt: (layer, pt[s, j * n_pg + u], 0, 0))

    gs = pltpu.PrefetchScalarGridSpec(
        num_scalar_prefetch=1,
        grid=(db, n_pages // n_pg),
        in_specs=[pl.BlockSpec((None, ts, d), lambda s, j, pt: (s, 0, 0))]
                 + [page_spec(u) for u in range(n_pg)]
                 + [pl.BlockSpec((None, ts, 2 * d), lambda s, j, pt: (s, 0, 0)),
                    pl.BlockSpec(lam_vec.shape, lambda s, j, pt: (0, 0)),
                    pl.BlockSpec((1, hd2), lambda s, j, pt: (0, 0))],
        out_specs=pl.BlockSpec((None, ts, d), lambda s, j, pt: (s, 0, 0)),
        scratch_shapes=[pltpu.VMEM((rows, hd2), BF16), pltpu.VMEM((rows, 1), F32),
                        pltpu.VMEM((rows, 1), F32), pltpu.VMEM((rows, hd2), F32)],
    )
    return pl.pallas_call(
        functools.partial(_diff_dec_kernel, n_pg=n_pg, ts=ts, lam_init=lam_init),
        grid_spec=gs,
        out_shape=jax.ShapeDtypeStruct((db, ts, d), BF16),
        compiler_params=_cparams("parallel", "arbitrary"),
        name="diff_attn_sample",
    )(page_table, q, *([cache2] * n_pg), kv_new, lam_vec, subln_g.reshape(1, -1))


def _outproj_kernel(h_ref, ot_ref, w_ref, out_ref):
    out_ref[...] = h_ref[...] + _tn_dot(ot_ref[...], w_ref[...])


def _outproj(h, ot, w_bf):
    rows, d = h.shape
    tm = min(512, rows)
    row = lambda i: (i, 0)
    return pl.pallas_call(
        _outproj_kernel,
        grid=(rows // tm,),
        in_specs=[pl.BlockSpec((tm, d), row), pl.BlockSpec((ot.shape[0], tm), lambda i: (0, i)),
                  pl.BlockSpec(w_bf.shape, lambda i: (0, 0))],
        out_specs=pl.BlockSpec((tm, d), row),
        out_shape=jax.ShapeDtypeStruct((rows, d), F32),
        compiler_params=_cparams("parallel"),
        name="outproj",
    )(h, ot, w_bf)


def _nsa_proj_kernel(x_ref, g_ref, w_ref, wg_ref, c_ref, sa_ref, sb_ref,
                     qpl_ref, qrt_ref, kvc_ref, kvs_ref, kvw_ref, ks_ref, vst_ref, kw_ref, vwt_ref, gate_ref,
                     *, half, scale):
    d = x_ref.shape[1]
    hd = NSA_HEAD_DIM
    kw = NSA_KV_HEADS * hd
    xn = _rmsnorm_rows(x_ref[...], g_ref[...]).astype(BF16)
    y = _dot(xn, w_ref[...])
    gl = _dot(xn, wg_ref[...])
    gate_ref[...] = (1.0 / (1.0 + jnp.exp(-gl))).T
    c, sa, sb = c_ref[...], sa_ref[...], sb_ref[...]
    q = y[:, :d]
    qpl_ref[...] = (q * scale).T.astype(BF16)
    qrt_ref[...] = (_rope_cols(q, c, sa, sb, half) * scale).T.astype(BF16)
    kvc_ref[...] = y[:, d:d + 2 * kw]
    off = d + 2 * kw
    for kv_ref, k_ref, vt_ref in ((kvs_ref, ks_ref, vst_ref), (kvw_ref, kw_ref, vwt_ref)):
        k = _rope_cols(y[:, off:off + kw], c, sa, sb, half)
        v = y[:, off + kw:off + 2 * kw]
        kv_ref[:, :kw] = k
        kv_ref[:, kw:] = v
        kb = k.astype(BF16)
        for h in range(NSA_KV_HEADS):
            k_ref[h] = kb[:, h * hd:(h + 1) * hd]
        _store_lane_chunks(vt_ref, v.T.astype(BF16))
        off += 2 * kw


def _nsa_proj(x, g, w_bf, wg_bf, tables, period_rows):
    rows, d = x.shape
    c, sa, sb, half = tables
    tm = min(512, rows)
    nper = period_rows // tm
    hd = NSA_HEAD_DIM
    kw = NSA_KV_HEADS * hd
    row = lambda i: (i, 0)
    col = lambda i: (0, i)
    tab = pl.BlockSpec((tm, LANES), lambda i: (i % nper, 0))
    const = lambda i: (0, 0)
    qt = jax.ShapeDtypeStruct((d, rows), BF16)
    khm = jax.ShapeDtypeStruct((NSA_KV_HEADS, rows, hd), BF16)
    cw = min(NSA_V_CHUNK, tm)
    vtc = jax.ShapeDtypeStruct((rows // cw, kw, cw), BF16)
    kvf = jax.ShapeDtypeStruct((rows, 2 * kw), F32)
    khm_spec = pl.BlockSpec((NSA_KV_HEADS, tm, hd), lambda i: (0, i, 0))
    vtc_spec = pl.BlockSpec((tm // cw, kw, cw), lambda i: (i, 0, 0))
    kv_spec = pl.BlockSpec((tm, 2 * kw), row)
    return pl.pallas_call(
        functools.partial(_nsa_proj_kernel, half=half, scale=hd ** -0.5),
        grid=(rows // tm,),
        in_specs=[pl.BlockSpec((tm, d), row), pl.BlockSpec((1, d), const),
                  pl.BlockSpec(w_bf.shape, const), pl.BlockSpec(wg_bf.shape, const), tab, tab, tab],
        out_specs=[pl.BlockSpec((d, tm), col), pl.BlockSpec((d, tm), col), kv_spec, kv_spec, kv_spec,
                   khm_spec, vtc_spec, khm_spec, vtc_spec, pl.BlockSpec((LANES, tm), col)],
        out_shape=[qt, qt, kvf, kvf, kvf, khm, vtc, khm, vtc, jax.ShapeDtypeStruct((LANES, rows), F32)],
        compiler_params=_cparams("parallel"),
        name="nsa_proj",
    )(x, g.reshape(1, d), w_bf, wg_bf, c, sa, sb)


def _compress_weights(cmp_pos, w1, b1, w2, b2):
    kvh, hd, blk = NSA_KV_HEADS, NSA_HEAD_DIM, NSA_BLOCK
    hid = w1.shape[-1]
    eye = jnp.eye(kvh, dtype=F32)
    eye2 = jnp.eye(2, dtype=F32)
    pos_rep = jnp.broadcast_to(cmp_pos[:, :, None, :], (blk, 2, kvh, hd)).reshape(blk, 2 * kvh * hd)
    w1bd = jnp.einsum('crde,kl->rckdle', w1, eye).reshape(blk, 2, kvh * hd, kvh * hid).astype(BF16)
    b1_rep = jnp.broadcast_to(b1[:, None, :], (2, kvh, hid)).reshape(1, 2 * kvh * hid)
    w2bd = jnp.einsum('aed,ab,kl->akebld', w2, eye2, eye).reshape(2 * kvh * hid, 2 * kvh * hd).astype(BF16)
    b2_rep = jnp.broadcast_to(b2[:, None, :], (2, kvh, hd)).reshape(1, 2 * kvh * hd)
    token_major = (pos_rep, w1bd, b1_rep, w2bd, b2_rep)
    pos_t = jnp.tile(cmp_pos.transpose(1, 2, 0), (1, 1, 2))
    w1_t = jnp.einsum('crde,ab->cdarbe', w1, eye2).reshape(2, hd, 2 * blk, 2 * hid).astype(BF16)
    b1_t = jnp.tile(b1, (1, 2)).reshape(2, 1, 2 * hid)
    w2_t = jnp.einsum('ced,ab->caebd', w2, eye2).reshape(2, 2 * hid, 2 * hd).astype(BF16)
    b2_t = jnp.tile(b2, (1, 2)).reshape(2, 1, 2 * hd)
    feature_major = (pos_t, w1_t, b1_t, w2_t, b2_t)
    return token_major, feature_major


def _strided_token_rows(ref, r, n_blk, chunks):
    parts = [ref[pl.ds(r * chunks + q, n_blk, stride=NSA_BLOCK * chunks), :] for q in range(chunks)]
    return jnp.concatenate(parts, axis=1)


def _compress_prompt_kernel(x_ref, pos_ref, w1_ref, b1_ref, w2_ref, b2_ref, kc_ref, vct_ref):
    chunks = pos_ref.shape[1] // LANES
    n_blk = x_ref.shape[0] // (NSA_BLOCK * chunks)
    half = w1_ref.shape[2]
    acc = [jnp.zeros((n_blk, half), F32), jnp.zeros((n_blk, half), F32)]
    for r in range(NSA_BLOCK):
        xr = (_strided_token_rows(x_ref, r, n_blk, chunks) + pos_ref[r:r + 1, :]).astype(BF16)
        for c in range(2):
            acc[c] = acc[c] + _dot(xr[:, c * half:(c + 1) * half], w1_ref[r, c])
    hid = _gelu(jnp.concatenate(acc, axis=1) + b1_ref[...]).astype(BF16)
    out = _dot(hid, w2_ref[...]) + b2_ref[...]
    hd = NSA_HEAD_DIM
    kb = out[:, :half].astype(BF16)
    for h in range(NSA_KV_HEADS):
        kc_ref[h] = kb[:, h * hd:(h + 1) * hd]
    vct_ref[...] = out[:, half:].T.astype(BF16)


def _compress_prompt(kv_c, cw, batch, seq):
    rows, w = kv_c.shape
    nc = seq // NSA_BLOCK
    chunks = w // LANES
    kw = NSA_KV_HEADS * NSA_HEAD_DIM
    full = lambda a: pl.BlockSpec(a.shape, lambda b: (0,) * a.ndim)
    return pl.pallas_call(
        _compress_prompt_kernel,
        grid=(batch,),
        in_specs=[pl.BlockSpec((seq * chunks, LANES), lambda b: (b, 0))] + [full(a) for a in cw],
        out_specs=[pl.BlockSpec((None, NSA_KV_HEADS, nc, NSA_HEAD_DIM), lambda b: (b, 0, 0, 0)),
                   pl.BlockSpec((None, kw, nc), lambda b: (b, 0, 0))],
        out_shape=[jax.ShapeDtypeStruct((batch, NSA_KV_HEADS, nc, NSA_HEAD_DIM), BF16),
                   jax.ShapeDtypeStruct((batch, kw, nc), BF16)],
        compiler_params=_cparams("parallel"),
        name="nsa_compress_prompt",
    )(kv_c.reshape(rows * chunks, LANES), *cw)


def _compress_sample_kernel(pt_ref, *rest, n_pg):
    page_refs = rest[:n_pg]
    pos_ref, w1_ref, b1_ref, w2_ref, b2_ref, kc_ref, vc_ref = rest[n_pg:]
    hd, kvh = NSA_HEAD_DIM, NSA_KV_HEADS
    lanes = page_refs[0].shape[1]
    outs = []
    for c in range(2):
        acc = jnp.zeros((n_pg * kvh, w1_ref.shape[3]), F32)
        for dd in range(hd):
            xr = jnp.concatenate([ref[pl.ds(c * kvh * hd + dd, kvh, stride=hd), :] for ref in page_refs], axis=0)
            xr = (xr + pos_ref[c, dd:dd + 1, :]).astype(BF16)
            acc = acc + _dot(xr, w1_ref[c, dd])
        hid = _gelu(acc + b1_ref[c]).astype(BF16)
        outs.append((_dot(hid, w2_ref[c]) + b2_ref[c]).astype(BF16))
    kc_ref[...] = outs[0].reshape(kc_ref.shape)
    vc_ref[...] = outs[1].reshape(vc_ref.shape)


def _feature_major_pages(cache):
    l, p, page = cache.shape[:3]
    return cache.transpose(0, 1, 3, 4, 5, 2).reshape(l, p, -1, page)


def _compress_sample(cache, layer, page_table, cw):
    db, n_pages = page_table.shape
    page = cache.shape[2]
    kvh, hd = NSA_KV_HEADS, NSA_HEAD_DIM
    bpp = page // NSA_BLOCK
    assert bpp == 2
    cache_t = _feature_major_pages(cache)
    n_pg = 16 if n_pages % 16 == 0 else 1
    full = lambda a: pl.BlockSpec(a.shape, lambda s, j, pt: (0,) * a.ndim)

    def page_spec(u):
        return pl.BlockSpec((None, None, cache_t.shape[2], page), lambda s, j, pt: (layer, pt[s, j * n_pg + u], 0, 0))

    out = jax.ShapeDtypeStruct((db, n_pages, kvh, bpp * hd), BF16)
    ospec = pl.BlockSpec((None, n_pg, kvh, bpp * hd), lambda s, j, pt: (s, j, 0, 0))
    gs = pltpu.PrefetchScalarGridSpec(
        num_scalar_prefetch=1,
        grid=(db, n_pages // n_pg),
        in_specs=[page_spec(u) for u in range(n_pg)] + [full(a) for a in cw],
        out_specs=[ospec, ospec],
    )
    kc, vc = pl.pallas_call(
        functools.partial(_compress_sample_kernel, n_pg=n_pg),
        grid_spec=gs,
        out_shape=[out, out],
        compiler_params=_cparams("parallel", "parallel"),
        name="nsa_compress_sample",
    )(page_table, *([cache_t] * n_pg), *cw)
    to_blocks = lambda a: a.reshape(db, n_pages, kvh, bpp, hd).transpose(0, 2, 1, 3, 4).reshape(db, kvh, -1, hd)
    return to_blocks(kc), to_blocks(vc)


def _block_scores(imp, jblk, cur, nb):
    forced = (jblk == 0) | (jblk == cur) | (jblk == cur - 1)
    score = jnp.where(forced, NSA_FORCE_SCORE, imp)
    return jnp.where((jblk <= cur) & (jblk < nb), score, NEG_INF)


def _cmp_sel_prompt_kernel(qt_ref, kc_ref, vct_ref, oc_ref, sel_ref, score_sc, *, tq, nb):
    i = pl.program_id(1)
    nc = kc_ref.shape[1]
    nbp = sel_ref.shape[1]
    hd = NSA_HEAD_DIM
    pos = i * tq + lax.broadcasted_iota(jnp.int32, (nbp, tq), 1)
    jblk = lax.broadcasted_iota(jnp.int32, (nbp, tq), 0)
    vis = (((lax.broadcasted_iota(jnp.int32, (nc, tq), 0) + 1) * NSA_BLOCK - 1)
           <= i * tq + lax.broadcasted_iota(jnp.int32, (nc, tq), 1))
    cur = pos // NSA_BLOCK
    for k in range(NSA_KV_HEADS):
        kc = kc_ref[k]
        vct = vct_ref[k * hd:(k + 1) * hd, :]
        imp = jnp.zeros((nc, tq), F32)
        for g in range(NSA_GROUP):
            h = k * NSA_GROUP + g
            s = jnp.where(vis, _dot(kc, qt_ref[h * hd:(h + 1) * hd, :]), NEG_INF)
            m = jnp.max(s, axis=0, keepdims=True)
            e = jnp.where(vis, jnp.exp(s - m), 0.0)
            p = e / jnp.maximum(jnp.sum(e, axis=0, keepdims=True), 1e-30)
            imp = imp + p
            oc_ref[h * hd:(h + 1) * hd, :] = _dot(vct, p.astype(BF16)).astype(BF16)
        if nbp > nc:
            imp = jnp.concatenate([imp, jnp.zeros((nbp - nc, tq), F32)], axis=0)
        score = _block_scores(imp, jblk, cur, nb)
        score_sc[...] = score

        def rank_body(r, rank):
            row = score_sc[pl.ds(r, 1), :]
            ahead = (row > score) | ((row == score) & (r < jblk))
            return rank + jnp.where(ahead, 1.0, 0.0)

        rank = lax.fori_loop(0, nb, rank_body, jnp.zeros((nbp, tq), F32))
        sel_ref[k] = jnp.where((rank < NSA_TOPN) & (score > 0.5 * NEG_INF), 1.0, 0.0)


def _cmp_sel_prompt(qplt, kc, vct, batch, seq):
    d, rows = qplt.shape
    hd = NSA_HEAD_DIM
    tq = min(256, seq)
    nq = seq // tq
    nc = kc.shape[2]
    nb = -(-seq // NSA_BLOCK)
    nbp = -(-nb // LANES) * LANES
    return pl.pallas_call(
        functools.partial(_cmp_sel_prompt_kernel, tq=tq, nb=nb),
        grid=(batch, nq),
        in_specs=[pl.BlockSpec((d, tq), lambda b, i: (0, b * nq + i)),
                  pl.BlockSpec((None, NSA_KV_HEADS, nc, hd), lambda b, i: (b, 0, 0, 0)),
                  pl.BlockSpec((None,) + vct.shape[1:], lambda b, i: (b, 0, 0))],
        out_specs=[pl.BlockSpec((d, tq), lambda b, i: (0, b * nq + i)),
                   pl.BlockSpec((NSA_KV_HEADS, nbp, tq), lambda b, i: (0, 0, b * nq + i))],
        out_shape=[jax.ShapeDtypeStruct((d, rows), BF16),
                   jax.ShapeDtypeStruct((NSA_KV_HEADS, nbp, rows), F32)],
        scratch_shapes=[pltpu.VMEM((nbp, tq), F32)],
        compiler_params=_cparams("parallel", "parallel"),
        name="nsa_cmp_select_prompt",
    )(qplt, kc, vct)


def _sel_win_prompt_kernel(qt_ref, ks_ref, vst_ref, kw_ref, vwt_ref, sel_ref, os_ref, ow_ref, *, tq, tk):
    i = pl.program_id(2)
    hd = NSA_HEAD_DIM
    g = NSA_GROUP
    nbp = sel_ref.shape[0]
    tw = vwt_ref.shape[2]
    cps = tk // vst_ref.shape[2]
    q4 = jnp.concatenate([qt_ref[a * hd:(a + 1) * hd, :] for a in range(g)], axis=1)
    flags = sel_ref[...].astype(BF16)
    blk_col = lax.broadcasted_iota(jnp.int32, (tk, nbp), 1)
    blk_of_row = lax.broadcasted_iota(jnp.int32, (tk, nbp), 0) // NSA_BLOCK

    def sel_step(j, carry, diagonal):
        start = pl.multiple_of(j * tk, tk)
        st = _dot(ks_ref[pl.ds(start, tk), :], q4)
        expand = jnp.where(blk_col == blk_of_row + j * (tk // NSA_BLOCK), 1.0, 0.0).astype(BF16)
        chosen = _dot(expand, flags) > 0.5
        if diagonal:
            qpos = i * tq + lax.broadcasted_iota(jnp.int32, (tk, tq), 1)
            chosen = chosen & ((lax.broadcasted_iota(jnp.int32, (tk, tq), 0) + j * tk) <= qpos)
        mask = jnp.concatenate([chosen] * g, axis=1)
        vt = jnp.concatenate([vst_ref[j * cps + u] for u in range(cps)], axis=1)
        return _online_update_t(st, mask, vt, *carry, some_key_visible=True)

    def win_step(j, carry):
        start = pl.multiple_of(j * tw, tw)
        st = _dot(kw_ref[pl.ds(start, tw), :], q4)
        dist = (i * tq + lax.broadcasted_iota(jnp.int32, (tw, tq), 1)
                - (lax.broadcasted_iota(jnp.int32, (tw, tq), 0) + j * tw))
        inside = (dist >= 0) & (dist < NSA_WINDOW)
        return _online_update_t(st, jnp.concatenate([inside] * g, axis=1), vwt_ref[j], *carry)

    def finish(carry, o_ref):
        m, l, acc = carry
        o = (acc * (1.0 / jnp.maximum(l, 1e-30))).astype(BF16)
        for a in range(g):
            o_ref[a * hd:(a + 1) * hd, :] = o[:, a * tq:(a + 1) * tq]

    init = (jnp.full((1, g * tq), NEG_INF, F32), jnp.zeros((1, g * tq), F32), jnp.zeros((hd, g * tq), F32))
    n_full = (i * tq) // tk
    n_kv = ((i + 1) * tq + tk - 1) // tk
    carry = lax.fori_loop(0, n_full, lambda j, c: sel_step(j, c, False), init)
    finish(lax.fori_loop(n_full, n_kv, lambda j, c: sel_step(j, c, True), carry), os_ref)
    first_win = jnp.maximum(i * tq - (NSA_WINDOW - 1), 0) // tw
    finish(lax.fori_loop(first_win, ((i + 1) * tq + tw - 1) // tw, win_step, init), ow_ref)


def _sel_win_prompt(qrtt, ks, vst, kw, vwt, sel, batch, seq):
    d, rows = qrtt.shape
    hd = NSA_HEAD_DIM
    cw = vst.shape[2]
    tq = min(256, seq)
    tk = _key_chunk(seq)
    assert tk % cw == 0 and seq % tk == 0
    nq = seq // tq
    nbp = sel.shape[1]
    gw = NSA_GROUP * hd
    qspec = pl.BlockSpec((gw, tq), lambda b, k, i: (k, b * nq + i))
    kspec = pl.BlockSpec((None, seq, hd), lambda b, k, i: (k, b, 0))
    vspec = pl.BlockSpec((seq // cw, hd, cw), lambda b, k, i: (b, k, 0))
    out = jax.ShapeDtypeStruct((d, rows), BF16)
    return pl.pallas_call(
        functools.partial(_sel_win_prompt_kernel, tq=tq, tk=tk),
        grid=(batch, NSA_KV_HEADS, nq),
        in_specs=[qspec, kspec, vspec, kspec, vspec,
                  pl.BlockSpec((None, nbp, tq), lambda b, k, i: (k, 0, b * nq + i))],
        out_specs=[qspec, qspec],
        out_shape=[out, out],
        compiler_params=_cparams("parallel", "parallel", "parallel"),
        name="nsa_sel_win_prompt",
    )(qrtt, ks, vst, kw, vwt, sel)


def _rows_by_head(q_ref):
    return jnp.concatenate([q_ref[h] for h in range(q_ref.shape[0])], axis=0)


def _cmp_sel_sample_kernel(q_ref, kc_ref, vc_ref, oc_ref, sel_ref, *, ts, past_len):
    nc = kc_ref.shape[1]
    nbp = sel_ref.shape[2]
    nb = -(-(past_len + ts) // NSA_BLOCK)
    gt = NSA_GROUP * ts
    q = _rows_by_head(q_ref)
    pos_g = past_len + lax.broadcasted_iota(jnp.int32, (gt, nc), 0) % ts
    vis = ((lax.broadcasted_iota(jnp.int32, (gt, nc), 1) + 1) * NSA_BLOCK - 1) <= pos_g
    pos = past_len + lax.broadcasted_iota(jnp.int32, (ts, nbp), 0)
    jblk = lax.broadcasted_iota(jnp.int32, (ts, nbp), 1)
    cur = pos // NSA_BLOCK
    for k in range(NSA_KV_HEADS):
        s = jnp.where(vis, _nt_dot(q[k * gt:(k + 1) * gt], kc_ref[k]), NEG_INF)
        m = jnp.max(s, axis=-1, keepdims=True)
        e = jnp.where(vis, jnp.exp(s - m), 0.0)
        p = e / jnp.maximum(jnp.sum(e, axis=-1, keepdims=True), 1e-30)
        o = _dot(p.astype(BF16), vc_ref[k]).astype(BF16)
        imp = p[0:ts]
        for g in range(NSA_GROUP):
            oc_ref[k * NSA_GROUP + g] = o[g * ts:(g + 1) * ts]
            if g:
                imp = imp + p[g * ts:(g + 1) * ts]
        if nbp > nc:
            imp = jnp.concatenate([imp, jnp.zeros((ts, nbp - nc), F32)], axis=1)
        score = _block_scores(imp, jblk, cur, nb)
        rank = jnp.zeros((ts, nbp), F32)
        for r in range(nb):
            col = score[:, r:r + 1]
            ahead = (col > score) | ((col == score) & (r < jblk))
            rank = rank + jnp.where(ahead, 1.0, 0.0)
        sel_ref[k] = jnp.where((rank < NSA_TOPN) & (score > 0.5 * NEG_INF), 1.0, 0.0)


def _cmp_sel_sample(q_pl, kc, vc, past_len):
    db, _, ts, hd = q_pl.shape
    nb = -(-(past_len + ts) // NSA_BLOCK)
    nbp = -(-nb // LANES) * LANES
    seqspec = lambda a: pl.BlockSpec((None,) + a.shape[1:], lambda s: (s, 0, 0, 0))
    return pl.pallas_call(
        functools.partial(_cmp_sel_sample_kernel, ts=ts, past_len=past_len),
        grid=(db,),
        in_specs=[seqspec(q_pl), seqspec(kc), seqspec(vc)],
        out_specs=[pl.BlockSpec((None, NSA_HEADS, ts, hd), lambda s: (s, 0, 0, 0)),
                   pl.BlockSpec((None, NSA_KV_HEADS, ts, nbp), lambda s: (s, 0, 0, 0))],
        out_shape=[jax.ShapeDtypeStruct((db, NSA_HEADS, ts, hd), BF16),
                   jax.ShapeDtypeStruct((db, NSA_KV_HEADS, ts, nbp), F32)],
        compiler_params=_cparams("parallel"),
        name="nsa_cmp_select_sample",
    )(q_pl, kc, vc)


def _new_token_kv(kvn, k, ts, n_rows):
    hd = NSA_HEAD_DIM
    kw = NSA_KV_HEADS * hd
    pad = jnp.zeros((n_rows - ts, hd), F32)
    kn = jnp.concatenate([kvn[:, k * hd:(k + 1) * hd], pad], axis=0).astype(BF16)
    vn = jnp.concatenate([kvn[:, kw + k * hd:kw + (k + 1) * hd], pad], axis=0).astype(BF16)
    return kn, vn


def _softmax_step(s, mask, m, l):
    s = jnp.where(mask, s, NEG_INF)
    m_new = jnp.maximum(m, jnp.max(s, axis=-1, keepdims=True))
    alpha = jnp.exp(m - m_new)
    p = jnp.where(mask, jnp.exp(s - m_new), 0.0)
    return m_new, alpha, alpha * l + jnp.sum(p, axis=-1, keepdims=True), p.astype(BF16)


def _sel_sample_kernel(pt_ref, q_ref, sel_ref, *rest, n_pg, ts, past_len):
    page_refs = rest[:n_pg]
    kvn_ref, o_ref, flag_sc, m_sc, l_sc, acc_sc = rest[n_pg:]
    j = pl.program_id(1)
    rows = NSA_HEADS * ts
    gt = NSA_GROUP * ts
    hd, kvh = NSA_HEAD_DIM, NSA_KV_HEADS
    page = page_refs[0].shape[1]
    nbp = sel_ref.shape[2]

    @pl.when(j == 0)
    def _():
        flag_sc[...] = jnp.concatenate([sel_ref[h // NSA_GROUP] for h in range(NSA_HEADS)], axis=0).astype(BF16)
        m_sc[...] = jnp.full(m_sc.shape, NEG_INF, F32)
        l_sc[...] = jnp.zeros(l_sc.shape, F32)
        acc_sc[...] = jnp.zeros(acc_sc.shape, F32)

    q = _rows_by_head(q_ref)
    flags = flag_sc[...]
    def chosen_keys(first_block, n_keys):
        blk_row = lax.broadcasted_iota(jnp.int32, (nbp, n_keys), 0)
        key_blk = lax.broadcasted_iota(jnp.int32, (nbp, n_keys), 1) // NSA_BLOCK
        expand = jnp.where(blk_row == key_blk + first_block, 1.0, 0.0).astype(BF16)
        return _dot(flags, expand) > 0.5

    s = jnp.concatenate(
        [jnp.concatenate([_dot(q[k * gt:(k + 1) * gt], ref[k * hd:(k + 1) * hd, :].astype(BF16))
                          for k in range(kvh)], axis=0) for ref in page_refs], axis=1)
    mask = chosen_keys(j * n_pg * (page // NSA_BLOCK), n_pg * page)
    m, alpha, l, p = _softmax_step(s, mask, m_sc[...], l_sc[...])
    acc = alpha * acc_sc[...]
    for u, ref in enumerate(page_refs):
        acc = acc + jnp.concatenate(
            [_nt_dot(p[k * gt:(k + 1) * gt, u * page:(u + 1) * page], ref[(kvh + k) * hd:(kvh + k + 1) * hd, :].astype(BF16))
             for k in range(kvh)], axis=0)
    m_sc[...], l_sc[...], acc_sc[...] = m, l, acc

    @pl.when(j == pl.num_programs(1) - 1)
    def _():
        kvn = kvn_ref[...]
        new = [_new_token_kv(kvn, k, ts, page) for k in range(kvh)]
        qi = lax.broadcasted_iota(jnp.int32, (rows, page), 0) % ts
        col = lax.broadcasted_iota(jnp.int32, (rows, page), 1)
        s = jnp.concatenate([_nt_dot(q[k * gt:(k + 1) * gt], new[k][0]) for k in range(kvh)], axis=0)
        mask = chosen_keys(past_len // NSA_BLOCK, page) & (col <= qi)
        mf, alpha, lf, p = _softmax_step(s, mask, m_sc[...], l_sc[...])
        pv = jnp.concatenate([_dot(p[k * gt:(k + 1) * gt], new[k][1]) for k in range(kvh)], axis=0)
        o = ((alpha * acc_sc[...] + pv) * (1.0 / jnp.maximum(lf, 1e-30))).astype(BF16)
        for h in range(NSA_HEADS):
            o_ref[h] = o[h * ts:(h + 1) * ts]


def _sel_sample(q_rt, sel, cache, layer, page_table, kv_new, past_len):
    db, _, ts, hd = q_rt.shape
    n_pages = page_table.shape[1]
    page = cache.shape[2]
    kw = NSA_KV_HEADS * hd
    nbp = sel.shape[3]
    cache_t = _feature_major_pages(cache)
    n_pg = 8 if n_pages % 8 == 0 else 1
    rows = NSA_HEADS * ts

    def page_spec(u):
        return pl.BlockSpec((None, None, 2 * kw, page), lambda s, j, pt: (layer, pt[s, j * n_pg + u], 0, 0))

    gs = pltpu.PrefetchScalarGridSpec(
        num_scalar_prefetch=1,
        grid=(db, n_pages // n_pg),
        in_specs=[pl.BlockSpec((None, NSA_HEADS, ts, hd), lambda s, j, pt: (s, 0, 0, 0)),
                  pl.BlockSpec((None, NSA_KV_HEADS, ts, nbp), lambda s, j, pt: (s, 0, 0, 0))]
                 + [page_spec(u) for u in range(n_pg)]
                 + [pl.BlockSpec((None, ts, 2 * kw), lambda s, j, pt: (s, 0, 0))],
        out_specs=pl.BlockSpec((None, NSA_HEADS, ts, hd), lambda s, j, pt: (s, 0, 0, 0)),
        scratch_shapes=[pltpu.VMEM((rows, nbp), BF16), pltpu.VMEM((rows, 1), F32),
                        pltpu.VMEM((rows, 1), F32), pltpu.VMEM((rows, hd), F32)],
    )
    return pl.pallas_call(
        functools.partial(_sel_sample_kernel, n_pg=n_pg, ts=ts, past_len=past_len),
        grid_spec=gs,
        out_shape=jax.ShapeDtypeStruct((db, NSA_HEADS, ts, hd), BF16),
        compiler_params=_cparams("parallel", "arbitrary"),
        name="nsa_sel_sample",
    )(page_table, q_rt, sel, *([cache_t] * n_pg), kv_new)


def _win_sample_kernel(q_ref, win_ref, kvn_ref, o_ref, *, ts, past_len):
    rows = NSA_HEADS * ts
    gt = NSA_GROUP * ts
    hd, kvh = NSA_HEAD_DIM, NSA_KV_HEADS
    kw = kvh * hd
    w_buf = win_ref.shape[0]
    q = _rows_by_head(q_ref)
    win = win_ref[...].astype(BF16)
    qpos = past_len + lax.broadcasted_iota(jnp.int32, (rows, w_buf), 0) % ts
    kpos = past_len - w_buf + lax.broadcasted_iota(jnp.int32, (rows, w_buf), 1)
    dist = qpos - kpos
    mask = (dist >= 0) & (dist < NSA_WINDOW) & (kpos >= 0)
    m = jnp.full((rows, 1), NEG_INF, F32)
    l = jnp.zeros((rows, 1), F32)
    s = jnp.concatenate([_nt_dot(q[k * gt:(k + 1) * gt], win[:, k * hd:(k + 1) * hd]) for k in range(kvh)], axis=0)
    m, alpha, l, p = _softmax_step(s, mask, m, l)
    acc = jnp.concatenate([_dot(p[k * gt:(k + 1) * gt], win[:, kw + k * hd:kw + (k + 1) * hd])
                           for k in range(kvh)], axis=0)
    new = [_new_token_kv(kvn_ref[...], k, ts, LANES) for k in range(kvh)]
    qi = lax.broadcasted_iota(jnp.int32, (rows, LANES), 0) % ts
    col = lax.broadcasted_iota(jnp.int32, (rows, LANES), 1)
    s = jnp.concatenate([_nt_dot(q[k * gt:(k + 1) * gt], new[k][0]) for k in range(kvh)], axis=0)
    m, alpha, l, p = _softmax_step(s, col <= qi, m, l)
    pv = jnp.concatenate([_dot(p[k * gt:(k + 1) * gt], new[k][1]) for k in range(kvh)], axis=0)
    o = ((alpha * acc + pv) * (1.0 / jnp.maximum(l, 1e-30))).astype(BF16)
    for h in range(NSA_HEADS):
        o_ref[h] = o[h * ts:(h + 1) * ts]


def _win_sample(q_rt, win_state, layer, kv_new, past_len):
    db, _, ts, hd = q_rt.shape
    kw = NSA_KV_HEADS * hd
    w_buf = win_state.shape[2]
    win2 = win_state.reshape(win_state.shape[0], db, w_buf, 2 * kw)
    return pl.pallas_call(
        functools.partial(_win_sample_kernel, ts=ts, past_len=past_len),
        grid=(db,),
        in_specs=[pl.BlockSpec((None, NSA_HEADS, ts, hd), lambda s: (s, 0, 0, 0)),
                  pl.BlockSpec((None, None, w_buf, 2 * kw), lambda s: (layer, s, 0, 0)),
                  pl.BlockSpec((None, ts, 2 * kw), lambda s: (s, 0, 0))],
        out_specs=pl.BlockSpec((None, NSA_HEADS, ts, hd), lambda s: (s, 0, 0, 0)),
        out_shape=jax.ShapeDtypeStruct((db, NSA_HEADS, ts, hd), BF16),
        compiler_params=_cparams("parallel"),
        name="nsa_win_sample",
    )(q_rt, win2, kv_new)


def _nsa_out_kernel(h_ref, oc_ref, os_ref, ow_ref, gate_ref, w_ref, out_ref, o_sc):
    hd = NSA_HEAD_DIM
    for h in range(NSA_HEADS):
        rows = slice(h * hd, (h + 1) * hd)
        o = (gate_ref[3 * h:3 * h + 1, :] * oc_ref[rows, :].astype(F32)
             + gate_ref[3 * h + 1:3 * h + 2, :] * os_ref[rows, :].astype(F32)
             + gate_ref[3 * h + 2:3 * h + 3, :] * ow_ref[rows, :].astype(F32))
        o_sc[rows, :] = o.astype(BF16)
    out_ref[...] = h_ref[...] + _tn_dot(o_sc[...], w_ref[...])


def _nsa_out(h, oct_, ost, owt, gates_t, w_out_bf):
    rows, d = h.shape
    tm = min(512, rows)
    col = lambda i: (0, i)
    ot = pl.BlockSpec((d, tm), col)
    return pl.pallas_call(
        _nsa_out_kernel,
        grid=(rows // tm,),
        in_specs=[pl.BlockSpec((tm, d), lambda i: (i, 0)), ot, ot, ot, pl.BlockSpec((LANES, tm), col),
                  pl.BlockSpec(w_out_bf.shape, lambda i: (0, 0))],
        out_specs=pl.BlockSpec((tm, d), lambda i: (i, 0)),
        out_shape=jax.ShapeDtypeStruct((rows, d), F32),
        scratch_shapes=[pltpu.VMEM((d, tm), BF16)],
        compiler_params=_cparams("parallel"),
        name="nsa_out",
    )(h, oct_, ost, owt, gates_t, w_out_bf)


def _top_desc(s, n):
    vals = []
    cur = s
    for _ in range(n):
        m = jnp.max(cur, axis=0, keepdims=True)
        vals.append(m)
        cur = jnp.where(cur == m, MASKED, cur)
    return jnp.concatenate(vals, axis=0)


def _oddeven_merge_sort_pairs(n):
    def merge(lo, hi, r):
        step = r * 2
        if step < hi - lo:
            yield from merge(lo, hi, step)
            yield from merge(lo + r, hi, step)
            yield from [(i, i + r) for i in range(lo + r, hi - r, step)]
        else:
            yield (lo, lo + r)

    def sort(lo, hi):
        if hi - lo >= 1:
            mid = lo + (hi - lo) // 2
            yield from sort(lo, mid)
            yield from sort(mid + 1, hi)
            yield from merge(lo, hi, 1)

    return list(sort(0, n - 1))


SUBLANES = 8


def _top_sorted(s, n):
    x = [s[v * SUBLANES:(v + 1) * SUBLANES, :] for v in range(n)]

    def exchange(i, j):
        x[i], x[j] = jnp.maximum(x[i], x[j]), jnp.minimum(x[i], x[j])

    for i, j in _oddeven_merge_sort_pairs(n):
        exchange(i, j)
    shift = SUBLANES // 2
    while shift:
        other = [pltpu.roll(v, shift, 0) for v in x]
        x = [jnp.maximum(x[v], other[n - 1 - v]) for v in range(n)]
        dist = n // 2
        while dist:
            for i in range(n):
                if not i & dist:
                    exchange(i, i + dist)
            dist //= 2
        shift //= 2
    return jnp.concatenate([v[0:1, :] for v in x], axis=0)


def _peer_score_kernel(h_ref, g_ref, wq_ref, sk_ref, xt_ref, th_ref, f1_ref, e2_ref):
    half = sk_ref.shape[2]
    xn = _rmsnorm_rows(h_ref[...], g_ref[...])
    xt_ref[...] = xn.T.astype(BF16)
    q = _dot(xn.astype(BF16), wq_ref[...])
    sk1, sk2 = sk_ref[0], sk_ref[1]
    kk = PEER_TOPK
    for h in range(PEER_HEADS):
        q1 = q[:, (2 * h) * half:(2 * h + 1) * half].astype(BF16)
        q2 = q[:, (2 * h + 1) * half:(2 * h + 2) * half].astype(BF16)
        s1 = _nt_dot(sk1, q1)
        s2 = _nt_dot(sk2, q2)
        top_of = _top_sorted if s1.shape[0] == kk * SUBLANES else _top_desc
        t1 = top_of(s1, kk)
        t2 = top_of(s2, kk)
        cand = [t1[0:1] + t2]
        for i in range(1, kk // 2):
            cand.append(t1[i:i + 1] + t2[0:kk // 2])
        cand.append(t1[kk // 2:] + t2[0:1])
        top = _top_desc(jnp.concatenate(cand, axis=0), kk)
        tau = top[kk - 1:kk]
        z = jnp.sum(jnp.exp(top - top[0:1]), axis=0, keepdims=True)
        thr = jnp.full(s1.shape, NO_KEEP, F32)
        for j in range(kk):
            t2j = t2[j:j + 1]
            thr = jnp.where((s1 + t2j) >= tau, t2j, thr)
        m2 = t2[0:1]
        th = jnp.exp(jnp.minimum(thr - m2, 1.0))
        f1 = jnp.exp(s1 - t1[0:1]) / z
        e2 = jnp.exp(s2 - m2)
        for tc in range(s1.shape[1] // LANES):
            lanes = slice(tc * LANES, (tc + 1) * LANES)
            th_ref[h, tc] = th[:, lanes]
            f1_ref[h, tc] = f1[:, lanes]
            e2_ref[h, tc] = e2[:, lanes]


def _peer_scores(h, g, wq_bf, sk_bf, tt):
    rows, d = h.shape
    n_keys = sk_bf.shape[1]
    nt = rows // tt
    tab = jax.ShapeDtypeStruct((PEER_HEADS, rows // LANES, n_keys, LANES), F32)
    tspec = pl.BlockSpec((PEER_HEADS, tt // LANES, n_keys, LANES), lambda i: (0, i, 0, 0))
    return pl.pallas_call(
        _peer_score_kernel,
        grid=(nt,),
        in_specs=[pl.BlockSpec((tt, d), lambda i: (i, 0)), pl.BlockSpec((1, d), lambda i: (0, 0)),
                  pl.BlockSpec(wq_bf.shape, lambda i: (0, 0)), pl.BlockSpec(sk_bf.shape, lambda i: (0, 0, 0))],
        out_specs=[pl.BlockSpec((d, tt), lambda i: (0, i)), tspec, tspec, tspec],
        out_shape=[jax.ShapeDtypeStruct((d, rows), BF16), tab, tab, tab],
        compiler_params=_cparams("parallel"),
        name="peer_scores",
    )(h, g.reshape(1, d), wq_bf, sk_bf)


PEER_A_PER_STEP = 16
PEER_A_PER_SUB = 8


def _peer_expert_kernel(xt_ref, u_ref, vt_ref, th_ref, f1_ref, e2_ref, h_ref, out_ref, acc_sc):
    c = pl.program_id(1)
    n_keys = e2_ref.shape[2]
    tt = xt_ref.shape[1]
    a_per_step = th_ref.shape[2]
    sub = PEER_A_PER_SUB * n_keys

    @pl.when(c == 0)
    def _():
        acc_sc[...] = jnp.zeros(acc_sc.shape, F32)

    xt = xt_ref[...]
    total = None
    for sc in range(a_per_step // PEER_A_PER_SUB):
        rows = slice(sc * sub, (sc + 1) * sub)
        act = _gelu(_dot(u_ref[rows, :], xt))
        g_parts = []
        for tc in range(tt // LANES):
            w = [jnp.zeros((n_keys, LANES), F32) for _ in range(PEER_A_PER_SUB)]
            for h in range(PEER_HEADS):
                e2 = e2_ref[h, tc]
                for u in range(PEER_A_PER_SUB):
                    a = sc * PEER_A_PER_SUB + u
                    keep = e2 >= th_ref[h, tc, a:a + 1, :]
                    w[u] = w[u] + jnp.where(keep, f1_ref[h, tc, a:a + 1, :] * e2, 0.0)
            g_parts.append(jnp.concatenate(w, axis=0) * act[:, tc * LANES:(tc + 1) * LANES])
        g = jnp.concatenate(g_parts, axis=1).astype(BF16)
        part = _dot(vt_ref[:, rows], g)
        total = part if total is None else total + part
    acc_sc[...] += total

    @pl.when(c == pl.num_programs(1) - 1)
    def _():
        out_ref[...] = h_ref[...] + acc_sc[...].T


def _peer_experts(h, xt, u_bf, vt_bf, th, f1, e2, tt):
    rows, d = h.shape
    n_exp = u_bf.shape[0]
    n_keys = e2.shape[2]
    ec = PEER_A_PER_STEP * n_keys
    e2spec = pl.BlockSpec((PEER_HEADS, tt // LANES, n_keys, LANES), lambda i, c: (0, i, 0, 0))
    aspec = pl.BlockSpec((PEER_HEADS, tt // LANES, PEER_A_PER_STEP, LANES), lambda i, c: (0, i, c, 0))
    return pl.pallas_call(
        _peer_expert_kernel,
        grid=(rows // tt, n_exp // ec),
        in_specs=[pl.BlockSpec((d, tt), lambda i, c: (0, i)),
                  pl.BlockSpec((ec, d), lambda i, c: (c, 0)),
                  pl.BlockSpec((d, ec), lambda i, c: (0, c)),
                  aspec, aspec, e2spec,
                  pl.BlockSpec((tt, d), lambda i, c: (i, 0))],
        out_specs=pl.BlockSpec((tt, d), lambda i, c: (i, 0)),
        out_shape=jax.ShapeDtypeStruct((rows, d), F32),
        scratch_shapes=[pltpu.VMEM((d, tt), F32)],
        compiler_params=_cparams("parallel", "arbitrary"),
        name="peer_experts",
    )(xt, u_bf, vt_bf, th, f1, e2, h)


def _peer(h, g, wq_bf, sk_bf, u_bf, vt_bf):
    rows = h.shape[0]
    tt = min(512, rows)
    xt, th, f1, e2 = _peer_scores(h, g, wq_bf, sk_bf, tt)
    return _peer_experts(h, xt, u_bf, vt_bf, th, f1, e2, tt)


def _final_norm_kernel(h_ref, g_ref, o_ref):
    o_ref[...] = _rmsnorm_rows(h_ref[...], g_ref[...])


def _final_norm(h, g):
    rows, d = h.shape
    tm = min(512, rows)
    return pl.pallas_call(
        _final_norm_kernel,
        grid=(rows // tm,),
        in_specs=[pl.BlockSpec((tm, d), lambda i: (i, 0)), pl.BlockSpec((1, d), lambda i: (0, 0))],
        out_specs=pl.BlockSpec((tm, d), lambda i: (i, 0)),
        out_shape=jax.ShapeDtypeStruct((rows, d), F32),
        compiler_params=_cparams("parallel"),
        name="final_norm",
    )(h, g.reshape(1, d))


def _heads_from_t(xt, db, ts, hd):
    return xt.reshape(-1, hd, db, ts).transpose(2, 0, 3, 1)


def _heads_to_t(x):
    db, heads, ts, hd = x.shape
    return x.transpose(1, 3, 0, 2).reshape(heads * hd, db * ts)


def kernel(x_prompt, x_sample, cache_diff_kv, cache_nsa_cmp_kv, cache_nsa_sel_kv, state_nsa_win_kv, page_table, norm_mix_g, diff_w_in, diff_lambda, diff_subln_g, diff_w_out, nsa_w_in, nsa_cmp_pos, nsa_cmp_w1, nsa_cmp_b1, nsa_cmp_w2, nsa_cmp_b2, nsa_w_out, norm_ffn_g, peer_wq, peer_subkeys, peer_u, peer_v, final_norm_g):
    batch, seq, d = x_prompt.shape
    db, ts, _ = x_sample.shape
    depth = norm_mix_g.shape[0]
    past_len = page_table.shape[1] * cache_diff_kv.shape[2]
    assert past_len % NSA_BLOCK == 0 and ts < NSA_BLOCK and seq % LANES == 0 and (db * ts) % LANES == 0
    assert state_nsa_win_kv.shape[2] == min(NSA_WINDOW, past_len)

    pos_p = jnp.arange(seq)
    pos_s = jnp.tile(past_len + jnp.arange(ts), db)
    tab_p = _rope_tables(pos_p, DA_HEAD_DIM)
    tab_s = _rope_tables(pos_s, DA_HEAD_DIM)

    hp = x_prompt.reshape(batch * seq, d)
    hs = x_sample.reshape(db * ts, d)
    outs = {k: [] for k in ("diff_p", "diff_s", "cmp_p", "cmp_s", "sel_p", "sel_s", "win_p", "win_s")}

    for i in range(depth):
        g_mix = norm_mix_g[i]
        if i % 2 == 0:
            a = i // 2
            lam_init = 0.8 - 0.6 * math.exp(-0.3 * i)
            w_in = diff_w_in[a].astype(BF16)
            w_out = diff_w_out[a].astype(BF16)
            qt, kvp, kb, vtc = _diff_proj(hp, g_mix, w_in, tab_p, seq)
            ot = _diff_attn_prompt(qt, kb, vtc, diff_lambda[a], diff_subln_g[a], batch, seq, lam_init)
            hp = _outproj(hp, ot, w_out)
            qt_s, kvs, _, _ = _diff_proj(hs, g_mix, w_in, tab_s, db * ts)
            os_ = _diff_attn_sample(qt_s.T.reshape(db, ts, d), cache_diff_kv, a, page_table,
                                    kvs.reshape(db, ts, 2 * d), diff_lambda[a], diff_subln_g[a], lam_init)
            hs = _outproj(hs, os_.reshape(db * ts, d).T, w_out)
            outs["diff_p"].append(kvp.reshape(batch, seq, 2, DA_HEADS, 2 * DA_HEAD_DIM))
            outs["diff_s"].append(kvs.reshape(db, ts, 2, DA_HEADS, 2 * DA_HEAD_DIM))
        else:
            b = i // 2
            hd = NSA_HEAD_DIM
            n_main = d + 3 * 2 * NSA_KV_HEADS * hd
            w_main = nsa_w_in[b][:, :n_main].astype(BF16)
            n_gate = nsa_w_in.shape[2] - n_main
            w_gate = jnp.pad(nsa_w_in[b][:, n_main:], ((0, 0), (0, LANES - n_gate))).astype(BF16)
            w_out = nsa_w_out[b].astype(BF16)
            cw_tok, cw_feat = _compress_weights(nsa_cmp_pos[b], nsa_cmp_w1[b], nsa_cmp_b1[b],
                                                nsa_cmp_w2[b], nsa_cmp_b2[b])
            kv_shape = (2, NSA_KV_HEADS, hd)
            (qplt, qrtt, kvc, kvs_, kvw, ks, vst, kw, vwt, gates_t) = _nsa_proj(hp, g_mix, w_main, w_gate, tab_p, seq)
            kc, vct = _compress_prompt(kvc, cw_tok, batch, seq)
            oct_, sel = _cmp_sel_prompt(qplt, kc, vct, batch, seq)
            ost, owt = _sel_win_prompt(qrtt, ks, vst, kw, vwt, sel, batch, seq)
            hp = _nsa_out(hp, oct_, ost, owt, gates_t, w_out)
            outs["cmp_p"].append(kvc.reshape((batch, seq) + kv_shape))
            outs["sel_p"].append(kvs_.reshape((batch, seq) + kv_shape))
            w_keep = min(NSA_WINDOW, seq)
            outs["win_p"].append(kvw.reshape((batch, seq) + kv_shape)[:, seq - w_keep:])
            (qplt, qrtt, kvc, kvs_, kvw, _, _, _, _, gates_t) = _nsa_proj(hs, g_mix, w_main, w_gate, tab_s, db * ts)
            kc, vc = _compress_sample(cache_nsa_cmp_kv, b, page_table, cw_feat)
            o_c, sel = _cmp_sel_sample(_heads_from_t(qplt, db, ts, hd), kc, vc, past_len)
            qrt_sm = _heads_from_t(qrtt, db, ts, hd)
            o_s = _sel_sample(qrt_sm, sel, cache_nsa_sel_kv, b, page_table, kvs_.reshape(db, ts, -1), past_len)
            o_w = _win_sample(qrt_sm, state_nsa_win_kv, b, kvw.reshape(db, ts, -1), past_len)
            hs = _nsa_out(hs, _heads_to_t(o_c), _heads_to_t(o_s), _heads_to_t(o_w), gates_t, w_out)
            outs["cmp_s"].append(kvc.reshape((db, ts) + kv_shape))
            outs["sel_s"].append(kvs_.reshape((db, ts) + kv_shape))
            win_all = jnp.concatenate([state_nsa_win_kv[b], kvw.reshape((db, ts) + kv_shape)], axis=1)
            outs["win_s"].append(win_all[:, win_all.shape[1] - state_nsa_win_kv.shape[2]:])
        g_ffn = norm_ffn_g[i]
        wq = peer_wq[i].astype(BF16)
        sk = peer_subkeys[i].astype(BF16)
        u_bf = peer_u[i].astype(BF16)
        vt_bf = peer_v[i].T.astype(BF16)
        hp = _peer(hp, g_ffn, wq, sk, u_bf, vt_bf)
        hs = _peer(hs, g_ffn, wq, sk, u_bf, vt_bf)

    y_prompt = _final_norm(hp, final_norm_g).reshape(batch, seq, d)
    y_sample = _final_norm(hs, final_norm_g).reshape(db, ts, d)
    stack = lambda k: jnp.stack(outs[k])
    return (y_prompt, y_sample, stack("diff_p"), stack("diff_s"), stack("cmp_p"), stack("cmp_s"),
            stack("sel_p"), stack("sel_s"), stack("win_p"), stack("win_s"))
```

```python
import functools
import math

import jax
import jax.numpy as jnp
from jax import lax
from jax.experimental import pallas as pl
from jax.experimental.pallas import tpu as pltpu

F32 = jnp.float32
BF16 = jnp.bfloat16

NORM_EPS = 1e-6
ROPE_THETA = 500000.0
ROPE_FRACTION = 4
NEG_INF = -1e30
MASKED = -3.0e38
NO_KEEP = 3.0e38

DA_HEADS = 8
DA_HEAD_DIM = 64
NSA_HEADS = 16
NSA_KV_HEADS = 4
NSA_GROUP = NSA_HEADS // NSA_KV_HEADS
NSA_HEAD_DIM = 64
NSA_BLOCK = 64
NSA_TOPN = 16
NSA_WINDOW = 512
NSA_FORCE_SCORE = 1e4
NSA_V_CHUNK = 256
PEER_HEADS = 8
PEER_TOPK = 16

LANES = 128
VMEM_LIMIT_BYTES = 56 * 1024 * 1024


def _cparams(*sem):
    return pltpu.CompilerParams(dimension_semantics=tuple(sem), vmem_limit_bytes=VMEM_LIMIT_BYTES)


def _dot(a, b):
    return jnp.dot(a, b, preferred_element_type=F32)


def _nt_dot(a, b):
    return lax.dot_general(a, b, (((1,), (1,)), ((), ())), preferred_element_type=F32)


def _tn_dot(a, b):
    return lax.dot_general(a, b, (((0,), (0,)), ((), ())), preferred_element_type=F32)


def _rmsnorm_rows(x, g):
    ms = jnp.mean(x * x, axis=-1, keepdims=True)
    return x * lax.rsqrt(ms + NORM_EPS) * g


def _gelu(x):
    return 0.5 * x * (1.0 + lax.erf(x * (2.0 ** -0.5)))


def _rope_tables(pos, head_dim):
    d_rot = head_dim // ROPE_FRACTION
    half = d_rot // 2
    inv_freq = ROPE_THETA ** (-jnp.arange(half, dtype=F32) / half)
    ang = pos.astype(F32)[:, None] * inv_freq[None, :]
    cos, sin = jnp.cos(ang), jnp.sin(ang)
    n = pos.shape[0]
    zeros = lambda w: jnp.zeros((n, w), F32)
    c = jnp.concatenate([cos, cos, jnp.ones((n, head_dim - d_rot), F32)], axis=1)
    sa = jnp.concatenate([-sin, zeros(head_dim - half)], axis=1)
    sb = jnp.concatenate([zeros(half), sin, zeros(head_dim - d_rot)], axis=1)
    rep = LANES // head_dim
    return jnp.tile(c, (1, rep)), jnp.tile(sa, (1, rep)), jnp.tile(sb, (1, rep)), half


def _rope_cols(y, c, sa, sb, half):
    outs = []
    for j in range(y.shape[1] // LANES):
        ch = y[:, j * LANES:(j + 1) * LANES]
        outs.append(ch * c + pltpu.roll(ch, LANES - half, 1) * sa + pltpu.roll(ch, half, 1) * sb)
    return outs[0] if len(outs) == 1 else jnp.concatenate(outs, axis=1)


def _store_lane_chunks(ref, xt):
    width = ref.shape[2]
    for c in range(xt.shape[1] // width):
        ref[c] = xt[:, c * width:(c + 1) * width]


def _diff_proj_kernel(x_ref, g_ref, w_ref, c_ref, sa_ref, sb_ref,
                      qt_ref, kv_ref, kb_ref, vt_ref, *, half, scale):
    d = x_ref.shape[1]
    xn = _rmsnorm_rows(x_ref[...], g_ref[...]).astype(BF16)
    y = _dot(xn, w_ref[...])
    c, sa, sb = c_ref[...], sa_ref[...], sb_ref[...]
    q = _rope_cols(y[:, :d], c, sa, sb, half) * scale
    k = _rope_cols(y[:, d:2 * d], c, sa, sb, half)
    v = y[:, 2 * d:]
    qt_ref[...] = q.T.astype(BF16)
    kv_ref[:, :d] = k
    kv_ref[:, d:] = v
    kb_ref[...] = k.astype(BF16)
    _store_lane_chunks(vt_ref, v.T.astype(BF16))


def _key_chunk(rows):
    return min(512, rows)


def _diff_proj(x, g, w_bf, tables, period_rows):
    rows, d = x.shape
    c, sa, sb, half = tables
    tm = min(512, rows)
    nper = period_rows // tm
    cw = _key_chunk(tm)
    row = lambda i: (i, 0)
    tab = pl.BlockSpec((tm, LANES), lambda i: (i % nper, 0))
    return pl.pallas_call(
        functools.partial(_diff_proj_kernel, half=half, scale=DA_HEAD_DIM ** -0.5),
        grid=(rows // tm,),
        in_specs=[pl.BlockSpec((tm, d), row), pl.BlockSpec((1, d), lambda i: (0, 0)),
                  pl.BlockSpec((d, 3 * d), lambda i: (0, 0)), tab, tab, tab],
        out_specs=[pl.BlockSpec((d, tm), lambda i: (0, i)), pl.BlockSpec((tm, 2 * d), row),
                   pl.BlockSpec((tm, d), row), pl.BlockSpec((tm // cw, d, cw), lambda i: (i, 0, 0))],
        out_shape=[jax.ShapeDtypeStruct((d, rows), BF16), jax.ShapeDtypeStruct((rows, 2 * d), F32),
                   jax.ShapeDtypeStruct((rows, d), BF16), jax.ShapeDtypeStruct((rows // cw, d, cw), BF16)],
        compiler_params=_cparams("parallel"),
        name="diff_proj",
    )(x, g.reshape(1, d), w_bf, c, sa, sb)


def _diff_lambda(lam_ref, lam_init):
    lv = lam_ref[...]
    a = jnp.sum(lv[0:1, :] * lv[1:2, :], axis=-1, keepdims=True)
    b = jnp.sum(lv[2:3, :] * lv[3:4, :], axis=-1, keepdims=True)
    return jnp.exp(a) - jnp.exp(b) + lam_init


def _subln(o, g, lam_init):
    ms = jnp.mean(o * o, axis=-1, keepdims=True)
    return o * lax.rsqrt(ms + NORM_EPS) * g * (1.0 - lam_init)


def _online_update(s, mask, v, m, l, acc):
    if mask is not None:
        s = jnp.where(mask, s, NEG_INF)
    m_new = jnp.maximum(m, jnp.max(s, axis=-1, keepdims=True))
    alpha = jnp.exp(m - m_new)
    p = jnp.exp(s - m_new)
    if mask is not None:
        p = jnp.where(mask, p, 0.0)
    l_new = alpha * l + jnp.sum(p, axis=-1, keepdims=True)
    acc_new = alpha * acc + _dot(p.astype(BF16), v)
    return m_new, l_new, acc_new


def _online_update_t(st, mask, vt, m, l, acc, some_key_visible=False):
    if mask is not None:
        st = jnp.where(mask, st, NEG_INF)
    m_new = jnp.maximum(m, jnp.max(st, axis=0, keepdims=True))
    alpha = jnp.exp(m - m_new)
    p = jnp.exp(st - m_new)
    if mask is not None and not some_key_visible:
        p = jnp.where(mask, p, 0.0)
    l_new = alpha * l + jnp.sum(p, axis=0, keepdims=True)
    acc_new = alpha * acc + _dot(vt, p.astype(BF16))
    return m_new, l_new, acc_new


def _diff_attn_kernel(qt_ref, k_ref, vt_ref, lam_ref, g_ref, o_ref, *, tq, tk, lam_init):
    i = pl.program_id(2)
    hd2 = qt_ref.shape[0]
    qt = qt_ref[...]
    comp = lax.broadcasted_iota(jnp.int32, (hd2, tq), 0) // (hd2 // 2)
    zero = jnp.zeros_like(qt)
    qq = jnp.concatenate([jnp.where(comp == 0, qt, zero), jnp.where(comp == 1, qt, zero)], axis=1)
    qpos = i * tq + lax.broadcasted_iota(jnp.int32, (tk, 2 * tq), 1) % tq
    krow = lax.broadcasted_iota(jnp.int32, (tk, 2 * tq), 0)

    def step(j, carry, masked):
        start = pl.multiple_of(j * tk, tk)
        st = _dot(k_ref[pl.ds(start, tk), :], qq)
        mask = ((krow + j * tk) <= qpos) if masked else None
        return _online_update_t(st, mask, vt_ref[j], *carry, some_key_visible=True)

    n_full = (i * tq) // tk
    n_kv = ((i + 1) * tq + tk - 1) // tk
    carry = (jnp.full((1, 2 * tq), NEG_INF, F32), jnp.zeros((1, 2 * tq), F32), jnp.zeros((hd2, 2 * tq), F32))
    carry = lax.fori_loop(0, n_full, lambda j, c: step(j, c, False), carry)
    m, l, acc = lax.fori_loop(n_full, n_kv, lambda j, c: step(j, c, True), carry)
    lam = _diff_lambda(lam_ref, lam_init)
    inv = 1.0 / jnp.maximum(l, 1e-30)
    o = acc[:, :tq] * inv[:, :tq] - lam * (acc[:, tq:] * inv[:, tq:])
    ms = jnp.mean(o * o, axis=0, keepdims=True)
    o_ref[...] = (o * lax.rsqrt(ms + NORM_EPS) * g_ref[...] * (1.0 - lam_init)).astype(BF16)


def _diff_attn_prompt(qt, kb, vtc, lam_vec, subln_g, batch, seq, lam_init):
    d, rows = qt.shape
    hd2 = 2 * DA_HEAD_DIM
    tq = min(512, seq)
    tk = vtc.shape[2]
    nq = seq // tq
    return pl.pallas_call(
        functools.partial(_diff_attn_kernel, tq=tq, tk=tk, lam_init=lam_init),
        grid=(batch, DA_HEADS, nq),
        in_specs=[pl.BlockSpec((hd2, tq), lambda b, h, i: (h, b * nq + i)),
                  pl.BlockSpec((seq, hd2), lambda b, h, i: (b, h)),
                  pl.BlockSpec((seq // tk, hd2, tk), lambda b, h, i: (b, h, 0)),
                  pl.BlockSpec(lam_vec.shape, lambda b, h, i: (0, 0)),
                  pl.BlockSpec((hd2, 1), lambda b, h, i: (0, 0))],
        out_specs=pl.BlockSpec((hd2, tq), lambda b, h, i: (h, b * nq + i)),
        out_shape=jax.ShapeDtypeStruct((d, rows), BF16),
        compiler_params=_cparams("parallel", "parallel", "parallel"),
        name="diff_attn_prompt",
    )(qt, kb, vtc, lam_vec, subln_g.reshape(hd2, 1))


def _diff_dec_kernel(pt_ref, q_ref, *rest, n_pg, ts, lam_init):
    page_refs = rest[:n_pg]
    kvn_ref, lam_ref, g_ref, o_ref, qq_sc, m_sc, l_sc, acc_sc = rest[n_pg:]
    j = pl.program_id(1)
    hd2 = 2 * DA_HEAD_DIM
    grp = 2 * ts
    page = page_refs[0].shape[0]
    rows = DA_HEADS * grp

    @pl.when(j == 0)
    def _():
        q = q_ref[...]
        comp = lax.broadcasted_iota(jnp.int32, (ts, hd2), 1) // DA_HEAD_DIM
        for h in range(DA_HEADS):
            qh = q[:, h * hd2:(h + 1) * hd2]
            zero = jnp.zeros_like(qh)
            qq_sc[h * grp:h * grp + ts, :] = jnp.where(comp == 0, qh, zero)
            qq_sc[h * grp + ts:(h + 1) * grp, :] = jnp.where(comp == 1, qh, zero)
        m_sc[...] = jnp.full(m_sc.shape, NEG_INF, F32)
        l_sc[...] = jnp.zeros(l_sc.shape, F32)
        acc_sc[...] = jnp.zeros(acc_sc.shape, F32)

    def attend(k_all, v_all, visible, m, l, acc):
        n_cols = k_all.shape[0]
        s = _nt_dot(qq_sc[...], k_all)
        keep = (lax.broadcasted_iota(jnp.int32, (rows, n_cols), 0) // grp
                == lax.broadcasted_iota(jnp.int32, (rows, n_cols), 1) % DA_HEADS)
        if visible is not None:
            keep = keep & visible
        s = jnp.where(keep, s, NEG_INF)
        m_new = jnp.maximum(m, jnp.max(s, axis=-1, keepdims=True))
        alpha = jnp.exp(m - m_new)
        p = jnp.where(keep, jnp.exp(s - m_new), 0.0)
        l_new = alpha * l + jnp.sum(p, axis=-1, keepdims=True)
        return m_new, l_new, alpha * acc + _dot(p.astype(BF16), v_all)

    def k_and_v(ref, n_tok):
        k = ref[:, 0:DA_HEADS, :].reshape(n_tok * DA_HEADS, hd2).astype(BF16)
        v = ref[:, DA_HEADS:2 * DA_HEADS, :].reshape(n_tok * DA_HEADS, hd2).astype(BF16)
        return k, v

    kv = [k_and_v(ref, page) for ref in page_refs]
    k_all = jnp.concatenate([k for k, _ in kv], axis=0)
    v_all = jnp.concatenate([v for _, v in kv], axis=0)
    m, l, acc = attend(k_all, v_all, None, m_sc[...], l_sc[...], acc_sc[...])
    m_sc[...], l_sc[...], acc_sc[...] = m, l, acc

    @pl.when(j == pl.num_programs(1) - 1)
    def _():
        kn, vn = k_and_v(kvn_ref, ts)
        pad = jnp.zeros((LANES - ts * DA_HEADS, hd2), BF16)
        qi = lax.broadcasted_iota(jnp.int32, (rows, LANES), 0) % ts
        tok = lax.broadcasted_iota(jnp.int32, (rows, LANES), 1) // DA_HEADS
        mf, lf, af = attend(jnp.concatenate([kn, pad], axis=0), jnp.concatenate([vn, pad], axis=0),
                            tok <= qi, m_sc[...], l_sc[...], acc_sc[...])
        lam = _diff_lambda(lam_ref, lam_init)
        af = af * (1.0 / jnp.maximum(lf, 1e-30))
        for h in range(DA_HEADS):
            o = af[h * grp:h * grp + ts] - lam * af[h * grp + ts:(h + 1) * grp]
            o_ref[:, h * hd2:(h + 1) * hd2] = _subln(o, g_ref[...], lam_init).astype(BF16)


def _diff_attn_sample(q, cache, layer, page_table, kv_new, lam_vec, subln_g, lam_init):
    db, ts, d = q.shape
    n_pages = page_table.shape[1]
    page = cache.shape[2]
    hd2 = 2 * DA_HEAD_DIM
    n_pg = 8 if n_pages % 8 == 0 else 1
    n_kvh = 2 * DA_HEADS
    assert ts * DA_HEADS <= LANES
    cache2 = cache.reshape(cache.shape[0], cache.shape[1], page, n_kvh, hd2)
    kv_new = kv_new.reshape(db, ts, n_kvh, hd2)
    rows = 2 * DA_HEADS * ts

    def page_spec(u):
        return pl.BlockSpec((None, None, page, n_kvh, hd2), lambda s, j, pt: (layer, pt[s, j * n_pg + u], 0, 0, 0))

    gs = pltpu.PrefetchScalarGridSpec(
        num_scalar_prefetch=1,
        grid=(db, n_pages // n_pg),
        in_specs=[pl.BlockSpec((None, ts, d), lambda s, j, pt: (s, 0, 0))]
                 + [page_spec(u) for u in range(n_pg)]
                 + [pl.BlockSpec((None, ts, n_kvh, hd2), lambda s, j, pt: (s, 0, 0, 0)),
                    pl.BlockSpec(lam_vec.shape, lambda s, j, pt: (0, 0)),
                    pl.BlockSpec((1, hd2), lambda s, j, pt: (0, 0))],
        out_specs=pl.BlockSpec((None, ts, d), lambda s, j, pt: (s, 0, 0)),
        scratch_shapes=[pltpu.VMEM((rows, hd2), BF16), pltpu.VMEM((rows, 1), F32),
                        pltpu.VMEM((rows, 1), F32), pltpu.VMEM((rows, hd2), F32)],
    )
    return pl.pallas_call(
        functools.partial(_diff_dec_kernel, n_pg=n_pg, ts=ts, lam_init=lam_init),
        grid_spec=gs,
        out_shape=jax.ShapeDtypeStruct((db, ts, d), BF16),
        compiler_params=_cparams("parallel", "arbitrary"),
        name="diff_attn_sample",
    )(page_table, q, *([cache2] * n_pg), kv_new, lam_vec, subln_g.reshape(1, -1))


def _outproj_kernel(h_ref, ot_ref, w_ref, out_ref):
    out_ref[...] = h_ref[...] + _tn_dot(ot_ref[...], w_ref[...])


def _outproj(h, ot, w_bf):
    rows, d = h.shape
    tm = min(512, rows)
    row = lambda i: (i, 0)
    return pl.pallas_call(
        _outproj_kernel,
        grid=(rows // tm,),
        in_specs=[pl.BlockSpec((tm, d), row), pl.BlockSpec((ot.shape[0], tm), lambda i: (0, i)),
                  pl.BlockSpec(w_bf.shape, lambda i: (0, 0))],
        out_specs=pl.BlockSpec((tm, d), row),
        out_shape=jax.ShapeDtypeStruct((rows, d), F32),
        compiler_params=_cparams("parallel"),
        name="outproj",
    )(h, ot, w_bf)


def _nsa_proj_kernel(x_ref, g_ref, w_ref, wg_ref, c_ref, sa_ref, sb_ref,
                     qpl_ref, qrt_ref, kvc_ref, kvs_ref, kvw_ref, ks_ref, vst_ref, kw_ref, vwt_ref, gate_ref,
                     *, half, scale):
    d = x_ref.shape[1]
    hd = NSA_HEAD_DIM
    kw = NSA_KV_HEADS * hd
    xn = _rmsnorm_rows(x_ref[...], g_ref[...]).astype(BF16)
    y = _dot(xn, w_ref[...])
    gl = _dot(xn, wg_ref[...])
    gate_ref[...] = (1.0 / (1.0 + jnp.exp(-gl))).T
    c, sa, sb = c_ref[...], sa_ref[...], sb_ref[...]
    q = y[:, :d]
    qpl_ref[...] = (q * scale).T.astype(BF16)
    qrt_ref[...] = (_rope_cols(q, c, sa, sb, half) * scale).T.astype(BF16)
    kvc_ref[...] = y[:, d:d + 2 * kw]
    off = d + 2 * kw
    for kv_ref, k_ref, vt_ref in ((kvs_ref, ks_ref, vst_ref), (kvw_ref, kw_ref, vwt_ref)):
        k = _rope_cols(y[:, off:off + kw], c, sa, sb, half)
        v = y[:, off + kw:off + 2 * kw]
        kv_ref[:, :kw] = k
        kv_ref[:, kw:] = v
        kb = k.astype(BF16)
        for h in range(NSA_KV_HEADS):
            k_ref[h] = kb[:, h * hd:(h + 1) * hd]
        _store_lane_chunks(vt_ref, v.T.astype(BF16))
        off += 2 * kw


def _nsa_proj(x, g, w_bf, wg_bf, tables, period_rows):
    rows, d = x.shape
    c, sa, sb, half = tables
    tm = min(512, rows)
    nper = period_rows // tm
    hd = NSA_HEAD_DIM
    kw = NSA_KV_HEADS * hd
    row = lambda i: (i, 0)
    col = lambda i: (0, i)
    tab = pl.BlockSpec((tm, LANES), lambda i: (i % nper, 0))
    const = lambda i: (0, 0)
    qt = jax.ShapeDtypeStruct((d, rows), BF16)
    khm = jax.ShapeDtypeStruct((NSA_KV_HEADS, rows, hd), BF16)
    cw = min(NSA_V_CHUNK, tm)
    vtc = jax.ShapeDtypeStruct((rows // cw, kw, cw), BF16)
    kvf = jax.ShapeDtypeStruct((rows, 2 * kw), F32)
    khm_spec = pl.BlockSpec((NSA_KV_HEADS, tm, hd), lambda i: (0, i, 0))
    vtc_spec = pl.BlockSpec((tm // cw, kw, cw), lambda i: (i, 0, 0))
    kv_spec = pl.BlockSpec((tm, 2 * kw), row)
    return pl.pallas_call(
        functools.partial(_nsa_proj_kernel, half=half, scale=hd ** -0.5),
        grid=(rows // tm,),
        in_specs=[pl.BlockSpec((tm, d), row), pl.BlockSpec((1, d), const),
                  pl.BlockSpec(w_bf.shape, const), pl.BlockSpec(wg_bf.shape, const), tab, tab, tab],
        out_specs=[pl.BlockSpec((d, tm), col), pl.BlockSpec((d, tm), col), kv_spec, kv_spec, kv_spec,
                   khm_spec, vtc_spec, khm_spec, vtc_spec, pl.BlockSpec((LANES, tm), col)],
        out_shape=[qt, qt, kvf, kvf, kvf, khm, vtc, khm, vtc, jax.ShapeDtypeStruct((LANES, rows), F32)],
        compiler_params=_cparams("parallel"),
        name="nsa_proj",
    )(x, g.reshape(1, d), w_bf, wg_bf, c, sa, sb)


def _compress_weights(cmp_pos, w1, b1, w2, b2):
    kvh, hd, blk = NSA_KV_HEADS, NSA_HEAD_DIM, NSA_BLOCK
    hid = w1.shape[-1]
    eye = jnp.eye(kvh, dtype=F32)
    eye2 = jnp.eye(2, dtype=F32)
    pos_rep = jnp.broadcast_to(cmp_pos[:, :, None, :], (blk, 2, kvh, hd)).reshape(blk, 2 * kvh * hd)
    w1bd = jnp.einsum('crde,kl->rckdle', w1, eye).reshape(blk, 2, kvh * hd, kvh * hid).astype(BF16)
    b1_rep = jnp.broadcast_to(b1[:, None, :], (2, kvh, hid)).reshape(1, 2 * kvh * hid)
    w2bd = jnp.einsum('aed,ab,kl->akebld', w2, eye2, eye).reshape(2 * kvh * hid, 2 * kvh * hd).astype(BF16)
    b2_rep = jnp.broadcast_to(b2[:, None, :], (2, kvh, hd)).reshape(1, 2 * kvh * hd)
    token_major = (pos_rep, w1bd, b1_rep, w2bd, b2_rep)
    pos_t = jnp.tile(cmp_pos.transpose(1, 2, 0), (1, 1, 2))
    w1_t = jnp.einsum('crde,ab->cdarbe', w1, eye2).reshape(2, hd, 2 * blk, 2 * hid).astype(BF16)
    b1_t = jnp.tile(b1, (1, 2)).reshape(2, 1, 2 * hid)
    w2_t = jnp.einsum('ced,ab->caebd', w2, eye2).reshape(2, 2 * hid, 2 * hd).astype(BF16)
    b2_t = jnp.tile(b2, (1, 2)).reshape(2, 1, 2 * hd)
    feature_major = (pos_t, w1_t, b1_t, w2_t, b2_t)
    return token_major, feature_major


def _strided_token_rows(ref, r, n_blk, chunks):
    parts = [ref[pl.ds(r * chunks + q, n_blk, stride=NSA_BLOCK * chunks), :] for q in range(chunks)]
    return jnp.concatenate(parts, axis=1)


def _compress_prompt_kernel(x_ref, pos_ref, w1_ref, b1_ref, w2_ref, b2_ref, kc_ref, vct_ref):
    chunks = pos_ref.shape[1] // LANES
    n_blk = x_ref.shape[0] // (NSA_BLOCK * chunks)
    half = w1_ref.shape[2]
    acc = [jnp.zeros((n_blk, half), F32), jnp.zeros((n_blk, half), F32)]
    for r in range(NSA_BLOCK):
        xr = (_strided_token_rows(x_ref, r, n_blk, chunks) + pos_ref[r:r + 1, :]).astype(BF16)
        for c in range(2):
            acc[c] = acc[c] + _dot(xr[:, c * half:(c + 1) * half], w1_ref[r, c])
    hid = _gelu(jnp.concatenate(acc, axis=1) + b1_ref[...]).astype(BF16)
    out = _dot(hid, w2_ref[...]) + b2_ref[...]
    hd = NSA_HEAD_DIM
    kb = out[:, :half].astype(BF16)
    for h in range(NSA_KV_HEADS):
        kc_ref[h] = kb[:, h * hd:(h + 1) * hd]
    vct_ref[...] = out[:, half:].T.astype(BF16)


def _compress_prompt(kv_c, cw, batch, seq):
    rows, w = kv_c.shape
    nc = seq // NSA_BLOCK
    chunks = w // LANES
    kw = NSA_KV_HEADS * NSA_HEAD_DIM
    full = lambda a: pl.BlockSpec(a.shape, lambda b: (0,) * a.ndim)
    return pl.pallas_call(
        _compress_prompt_kernel,
        grid=(batch,),
        in_specs=[pl.BlockSpec((seq * chunks, LANES), lambda b: (b, 0))] + [full(a) for a in cw],
        out_specs=[pl.BlockSpec((None, NSA_KV_HEADS, nc, NSA_HEAD_DIM), lambda b: (b, 0, 0, 0)),
                   pl.BlockSpec((None, kw, nc), lambda b: (b, 0, 0))],
        out_shape=[jax.ShapeDtypeStruct((batch, NSA_KV_HEADS, nc, NSA_HEAD_DIM), BF16),
                   jax.ShapeDtypeStruct((batch, kw, nc), BF16)],
        compiler_params=_cparams("parallel"),
        name="nsa_compress_prompt",
    )(kv_c.reshape(rows * chunks, LANES), *cw)


def _compress_sample_kernel(pt_ref, *rest, n_pg):
    page_refs = rest[:n_pg]
    pos_ref, w1_ref, b1_ref, w2_ref, b2_ref, kc_ref, vc_ref = rest[n_pg:]
    hd, kvh = NSA_HEAD_DIM, NSA_KV_HEADS
    lanes = page_refs[0].shape[1]
    outs = []
    for c in range(2):
        acc = jnp.zeros((n_pg * kvh, w1_ref.shape[3]), F32)
        for dd in range(hd):
            xr = jnp.concatenate([ref[pl.ds(c * kvh * hd + dd, kvh, stride=hd), :] for ref in page_refs], axis=0)
            xr = (xr + pos_ref[c, dd:dd + 1, :]).astype(BF16)
            acc = acc + _dot(xr, w1_ref[c, dd])
        hid = _gelu(acc + b1_ref[c]).astype(BF16)
        outs.append((_dot(hid, w2_ref[c]) + b2_ref[c]).astype(BF16))
    kc_ref[...] = outs[0].reshape(kc_ref.shape)
    vc_ref[...] = outs[1].reshape(vc_ref.shape)


def _feature_major_pages(cache):
    l, p, page = cache.shape[:3]
    return cache.transpose(0, 1, 3, 4, 5, 2).reshape(l, p, -1, page)


def _compress_sample(cache, layer, page_table, cw):
    db, n_pages = page_table.shape
    page = cache.shape[2]
    kvh, hd = NSA_KV_HEADS, NSA_HEAD_DIM
    bpp = page // NSA_BLOCK
    assert bpp == 2
    cache_t = _feature_major_pages(cache)
    n_pg = 16 if n_pages % 16 == 0 else 1
    full = lambda a: pl.BlockSpec(a.shape, lambda s, j, pt: (0,) * a.ndim)

    def page_spec(u):
        return pl.BlockSpec((None, None, cache_t.shape[2], page), lambda s, j, pt: (layer, pt[s, j * n_pg + u], 0, 0))

    out = jax.ShapeDtypeStruct((db, n_pages, kvh, bpp * hd), BF16)
    ospec = pl.BlockSpec((None, n_pg, kvh, bpp * hd), lambda s, j, pt: (s, j, 0, 0))
    gs = pltpu.PrefetchScalarGridSpec(
        num_scalar_prefetch=1,
        grid=(db, n_pages // n_pg),
        in_specs=[page_spec(u) for u in range(n_pg)] + [full(a) for a in cw],
        out_specs=[ospec, ospec],
    )
    kc, vc = pl.pallas_call(
        functools.partial(_compress_sample_kernel, n_pg=n_pg),
        grid_spec=gs,
        out_shape=[out, out],
        compiler_params=_cparams("parallel", "parallel"),
        name="nsa_compress_sample",
    )(page_table, *([cache_t] * n_pg), *cw)
    to_blocks = lambda a: a.reshape(db, n_pages, kvh, bpp, hd).transpose(0, 2, 1, 3, 4).reshape(db, kvh, -1, hd)
    return to_blocks(kc), to_blocks(vc)


def _block_scores(imp, jblk, cur, nb):
    forced = (jblk == 0) | (jblk == cur) | (jblk == cur - 1)
    score = jnp.where(forced, NSA_FORCE_SCORE, imp)
    return jnp.where((jblk <= cur) & (jblk < nb), score, NEG_INF)


def _cmp_sel_prompt_kernel(qt_ref, kc_ref, vct_ref, oc_ref, sel_ref, score_sc, *, tq, nb):
    i = pl.program_id(1)
    nc = kc_ref.shape[1]
    nbp = sel_ref.shape[1]
    hd = NSA_HEAD_DIM
    pos = i * tq + lax.broadcasted_iota(jnp.int32, (nbp, tq), 1)
    jblk = lax.broadcasted_iota(jnp.int32, (nbp, tq), 0)
    vis = (((lax.broadcasted_iota(jnp.int32, (nc, tq), 0) + 1) * NSA_BLOCK - 1)
           <= i * tq + lax.broadcasted_iota(jnp.int32, (nc, tq), 1))
    cur = pos // NSA_BLOCK
    for k in range(NSA_KV_HEADS):
        kc = kc_ref[k]
        vct = vct_ref[k * hd:(k + 1) * hd, :]
        imp = jnp.zeros((nc, tq), F32)
        for g in range(NSA_GROUP):
            h = k * NSA_GROUP + g
            s = jnp.where(vis, _dot(kc, qt_ref[h * hd:(h + 1) * hd, :]), NEG_INF)
            m = jnp.max(s, axis=0, keepdims=True)
            e = jnp.where(vis, jnp.exp(s - m), 0.0)
            p = e / jnp.maximum(jnp.sum(e, axis=0, keepdims=True), 1e-30)
            imp = imp + p
            oc_ref[h * hd:(h + 1) * hd, :] = _dot(vct, p.astype(BF16)).astype(BF16)
        if nbp > nc:
            imp = jnp.concatenate([imp, jnp.zeros((nbp - nc, tq), F32)], axis=0)
        score = _block_scores(imp, jblk, cur, nb)
        score_sc[...] = score

        def rank_body(r, rank):
            row = score_sc[pl.ds(r, 1), :]
            ahead = (row > score) | ((row == score) & (r < jblk))
            return rank + jnp.where(ahead, 1.0, 0.0)

        rank = lax.fori_loop(0, nb, rank_body, jnp.zeros((nbp, tq), F32))
        sel_ref[k] = jnp.where((rank < NSA_TOPN) & (score > 0.5 * NEG_INF), 1.0, 0.0)


def _cmp_sel_prompt(qplt, kc, vct, batch, seq):
    d, rows = qplt.shape
    hd = NSA_HEAD_DIM
    tq = min(256, seq)
    nq = seq // tq
    nc = kc.shape[2]
    nb = -(-seq // NSA_BLOCK)
    nbp = -(-nb // LANES) * LANES
    return pl.pallas_call(
        functools.partial(_cmp_sel_prompt_kernel, tq=tq, nb=nb),
        grid=(batch, nq),
        in_specs=[pl.BlockSpec((d, tq), lambda b, i: (0, b * nq + i)),
                  pl.BlockSpec((None, NSA_KV_HEADS, nc, hd), lambda b, i: (b, 0, 0, 0)),
                  pl.BlockSpec((None,) + vct.shape[1:], lambda b, i: (b, 0, 0))],
        out_specs=[pl.BlockSpec((d, tq), lambda b, i: (0, b * nq + i)),
                   pl.BlockSpec((NSA_KV_HEADS, nbp, tq), lambda b, i: (0, 0, b * nq + i))],
        out_shape=[jax.ShapeDtypeStruct((d, rows), BF16),
                   jax.ShapeDtypeStruct((NSA_KV_HEADS, nbp, rows), F32)],
        scratch_shapes=[pltpu.VMEM((nbp, tq), F32)],
        compiler_params=_cparams("parallel", "parallel"),
        name="nsa_cmp_select_prompt",
    )(qplt, kc, vct)


def _sel_win_prompt_kernel(qt_ref, ks_ref, vst_ref, kw_ref, vwt_ref, sel_ref, os_ref, ow_ref, *, tq, tk):
    i = pl.program_id(2)
    hd = NSA_HEAD_DIM
    g = NSA_GROUP
    nbp = sel_ref.shape[0]
    tw = vwt_ref.shape[2]
    cps = tk // vst_ref.shape[2]
    q4 = jnp.concatenate([qt_ref[a * hd:(a + 1) * hd, :] for a in range(g)], axis=1)
    flags = sel_ref[...].astype(BF16)
    blk_col = lax.broadcasted_iota(jnp.int32, (tk, nbp), 1)
    blk_of_row = lax.broadcasted_iota(jnp.int32, (tk, nbp), 0) // NSA_BLOCK

    def sel_step(j, carry, diagonal):
        start = pl.multiple_of(j * tk, tk)
        st = _dot(ks_ref[pl.ds(start, tk), :], q4)
        expand = jnp.where(blk_col == blk_of_row + j * (tk // NSA_BLOCK), 1.0, 0.0).astype(BF16)
        chosen = _dot(expand, flags) > 0.5
        if diagonal:
            qpos = i * tq + lax.broadcasted_iota(jnp.int32, (tk, tq), 1)
            chosen = chosen & ((lax.broadcasted_iota(jnp.int32, (tk, tq), 0) + j * tk) <= qpos)
        mask = jnp.concatenate([chosen] * g, axis=1)
        vt = jnp.concatenate([vst_ref[j * cps + u] for u in range(cps)], axis=1)
        return _online_update_t(st, mask, vt, *carry, some_key_visible=True)

    def win_step(j, carry):
        start = pl.multiple_of(j * tw, tw)
        st = _dot(kw_ref[pl.ds(start, tw), :], q4)
        dist = (i * tq + lax.broadcasted_iota(jnp.int32, (tw, tq), 1)
                - (lax.broadcasted_iota(jnp.int32, (tw, tq), 0) + j * tw))
        inside = (dist >= 0) & (dist < NSA_WINDOW)
        return _online_update_t(st, jnp.concatenate([inside] * g, axis=1), vwt_ref[j], *carry)

    def finish(carry, o_ref):
        m, l, acc = carry
        o = (acc * (1.0 / jnp.maximum(l, 1e-30))).astype(BF16)
        for a in range(g):
            o_ref[a * hd:(a + 1) * hd, :] = o[:, a * tq:(a + 1) * tq]

    init = (jnp.full((1, g * tq), NEG_INF, F32), jnp.zeros((1, g * tq), F32), jnp.zeros((hd, g * tq), F32))
    n_full = (i * tq) // tk
    n_kv = ((i + 1) * tq + tk - 1) // tk
    carry = lax.fori_loop(0, n_full, lambda j, c: sel_step(j, c, False), init)
    finish(lax.fori_loop(n_full, n_kv, lambda j, c: sel_step(j, c, True), carry), os_ref)
    first_win = jnp.maximum(i * tq - (NSA_WINDOW - 1), 0) // tw
    finish(lax.fori_loop(first_win, ((i + 1) * tq + tw - 1) // tw, win_step, init), ow_ref)


def _sel_win_prompt(qrtt, ks, vst, kw, vwt, sel, batch, seq):
    d, rows = qrtt.shape
    hd = NSA_HEAD_DIM
    cw = vst.shape[2]
    tq = min(256, seq)
    tk = _key_chunk(seq)
    assert tk % cw == 0 and seq % tk == 0
    nq = seq // tq
    nbp = sel.shape[1]
    gw = NSA_GROUP * hd
    qspec = pl.BlockSpec((gw, tq), lambda b, k, i: (k, b * nq + i))
    kspec = pl.BlockSpec((None, seq, hd), lambda b, k, i: (k, b, 0))
    vspec = pl.BlockSpec((seq // cw, hd, cw), lambda b, k, i: (b, k, 0))
    out = jax.ShapeDtypeStruct((d, rows), BF16)
    return pl.pallas_call(
        functools.partial(_sel_win_prompt_kernel, tq=tq, tk=tk),
        grid=(batch, NSA_KV_HEADS, nq),
        in_specs=[qspec, kspec, vspec, kspec, vspec,
                  pl.BlockSpec((None, nbp, tq), lambda b, k, i: (k, 0, b * nq + i))],
        out_specs=[qspec, qspec],
        out_shape=[out, out],
        compiler_params=_cparams("parallel", "parallel", "parallel"),
        name="nsa_sel_win_prompt",
    )(qrtt, ks, vst, kw, vwt, sel)


def _rows_by_head(q_ref):
    return jnp.concatenate([q_ref[h] for h in range(q_ref.shape[0])], axis=0)


def _cmp_sel_sample_kernel(q_ref, kc_ref, vc_ref, oc_ref, sel_ref, *, ts, past_len):
    nc = kc_ref.shape[1]
    nbp = sel_ref.shape[2]
    nb = -(-(past_len + ts) // NSA_BLOCK)
    gt = NSA_GROUP * ts
    q = _rows_by_head(q_ref)
    pos_g = past_len + lax.broadcasted_iota(jnp.int32, (gt, nc), 0) % ts
    vis = ((lax.broadcasted_iota(jnp.int32, (gt, nc), 1) + 1) * NSA_BLOCK - 1) <= pos_g
    pos = past_len + lax.broadcasted_iota(jnp.int32, (ts, nbp), 0)
    jblk = lax.broadcasted_iota(jnp.int32, (ts, nbp), 1)
    cur = pos // NSA_BLOCK
    for k in range(NSA_KV_HEADS):
        s = jnp.where(vis, _nt_dot(q[k * gt:(k + 1) * gt], kc_ref[k]), NEG_INF)
        m = jnp.max(s, axis=-1, keepdims=True)
        e = jnp.where(vis, jnp.exp(s - m), 0.0)
        p = e / jnp.maximum(jnp.sum(e, axis=-1, keepdims=True), 1e-30)
        o = _dot(p.astype(BF16), vc_ref[k]).astype(BF16)
        imp = p[0:ts]
        for g in range(NSA_GROUP):
            oc_ref[k * NSA_GROUP + g] = o[g * ts:(g + 1) * ts]
            if g:
                imp = imp + p[g * ts:(g + 1) * ts]
        if nbp > nc:
            imp = jnp.concatenate([imp, jnp.zeros((ts, nbp - nc), F32)], axis=1)
        score = _block_scores(imp, jblk, cur, nb)
        rank = jnp.zeros((ts, nbp), F32)
        for r in range(nb):
            col = score[:, r:r + 1]
            ahead = (col > score) | ((col == score) & (r < jblk))
            rank = rank + jnp.where(ahead, 1.0, 0.0)
        sel_ref[k] = jnp.where((rank < NSA_TOPN) & (score > 0.5 * NEG_INF), 1.0, 0.0)


def _cmp_sel_sample(q_pl, kc, vc, past_len):
    db, _, ts, hd = q_pl.shape
    nb = -(-(past_len + ts) // NSA_BLOCK)
    nbp = -(-nb // LANES) * LANES
    seqspec = lambda a: pl.BlockSpec((None,) + a.shape[1:], lambda s: (s, 0, 0, 0))
    return pl.pallas_call(
        functools.partial(_cmp_sel_sample_kernel, ts=ts, past_len=past_len),
        grid=(db,),
        in_specs=[seqspec(q_pl), seqspec(kc), seqspec(vc)],
        out_specs=[pl.BlockSpec((None, NSA_HEADS, ts, hd), lambda s: (s, 0, 0, 0)),
                   pl.BlockSpec((None, NSA_KV_HEADS, ts, nbp), lambda s: (s, 0, 0, 0))],
        out_shape=[jax.ShapeDtypeStruct((db, NSA_HEADS, ts, hd), BF16),
                   jax.ShapeDtypeStruct((db, NSA_KV_HEADS, ts, nbp), F32)],
        compiler_params=_cparams("parallel"),
        name="nsa_cmp_select_sample",
    )(q_pl, kc, vc)


def _new_token_kv(kvn, k, ts, n_rows):
    hd = NSA_HEAD_DIM
    kw = NSA_KV_HEADS * hd
    pad = jnp.zeros((n_rows - ts, hd), F32)
    kn = jnp.concatenate([kvn[:, k * hd:(k + 1) * hd], pad], axis=0).astype(BF16)
    vn = jnp.concatenate([kvn[:, kw + k * hd:kw + (k + 1) * hd], pad], axis=0).astype(BF16)
    return kn, vn


def _softmax_step(s, mask, m, l):
    s = jnp.where(mask, s, NEG_INF)
    m_new = jnp.maximum(m, jnp.max(s, axis=-1, keepdims=True))
    alpha = jnp.exp(m - m_new)
    p = jnp.where(mask, jnp.exp(s - m_new), 0.0)
    return m_new, alpha, alpha * l + jnp.sum(p, axis=-1, keepdims=True), p.astype(BF16)


def _sel_sample_kernel(pt_ref, q_ref, sel_ref, *rest, n_pg, ts, past_len):
    page_refs = rest[:n_pg]
    kvn_ref, o_ref, flag_sc, m_sc, l_sc, acc_sc = rest[n_pg:]
    j = pl.program_id(1)
    rows = NSA_HEADS * ts
    gt = NSA_GROUP * ts
    hd, kvh = NSA_HEAD_DIM, NSA_KV_HEADS
    page = page_refs[0].shape[1]
    nbp = sel_ref.shape[2]

    @pl.when(j == 0)
    def _():
        flag_sc[...] = jnp.concatenate([sel_ref[h // NSA_GROUP] for h in range(NSA_HEADS)], axis=0).astype(BF16)
        m_sc[...] = jnp.full(m_sc.shape, NEG_INF, F32)
        l_sc[...] = jnp.zeros(l_sc.shape, F32)
        acc_sc[...] = jnp.zeros(acc_sc.shape, F32)

    q = _rows_by_head(q_ref)
    flags = flag_sc[...]
    def chosen_keys(first_block, n_keys):
        blk_row = lax.broadcasted_iota(jnp.int32, (nbp, n_keys), 0)
        key_blk = lax.broadcasted_iota(jnp.int32, (nbp, n_keys), 1) // NSA_BLOCK
        expand = jnp.where(blk_row == key_blk + first_block, 1.0, 0.0).astype(BF16)
        return _dot(flags, expand) > 0.5

    s = jnp.concatenate(
        [jnp.concatenate([_dot(q[k * gt:(k + 1) * gt], ref[k * hd:(k + 1) * hd, :].astype(BF16))
                          for k in range(kvh)], axis=0) for ref in page_refs], axis=1)
    mask = chosen_keys(j * n_pg * (page // NSA_BLOCK), n_pg * page)
    m, alpha, l, p = _softmax_step(s, mask, m_sc[...], l_sc[...])
    acc = alpha * acc_sc[...]
    for u, ref in enumerate(page_refs):
        acc = acc + jnp.concatenate(
            [_nt_dot(p[k * gt:(k + 1) * gt, u * page:(u + 1) * page], ref[(kvh + k) * hd:(kvh + k + 1) * hd, :].astype(BF16))
             for k in range(kvh)], axis=0)
    m_sc[...], l_sc[...], acc_sc[...] = m, l, acc

    @pl.when(j == pl.num_programs(1) - 1)
    def _():
        kvn = kvn_ref[...]
        new = [_new_token_kv(kvn, k, ts, page) for k in range(kvh)]
        qi = lax.broadcasted_iota(jnp.int32, (rows, page), 0) % ts
        col = lax.broadcasted_iota(jnp.int32, (rows, page), 1)
        s = jnp.concatenate([_nt_dot(q[k * gt:(k + 1) * gt], new[k][0]) for k in range(kvh)], axis=0)
        mask = chosen_keys(past_len // NSA_BLOCK, page) & (col <= qi)
        mf, alpha, lf, p = _softmax_step(s, mask, m_sc[...], l_sc[...])
        pv = jnp.concatenate([_dot(p[k * gt:(k + 1) * gt], new[k][1]) for k in range(kvh)], axis=0)
        o = ((alpha * acc_sc[...] + pv) * (1.0 / jnp.maximum(lf, 1e-30))).astype(BF16)
        for h in range(NSA_HEADS):
            o_ref[h] = o[h * ts:(h + 1) * ts]


def _sel_sample(q_rt, sel, cache, layer, page_table, kv_new, past_len):
    db, _, ts, hd = q_rt.shape
    n_pages = page_table.shape[1]
    page = cache.shape[2]
    kw = NSA_KV_HEADS * hd
    nbp = sel.shape[3]
    cache_t = _feature_major_pages(cache)
    n_pg = 8 if n_pages % 8 == 0 else 1
    rows = NSA_HEADS * ts

    def page_spec(u):
        return pl.BlockSpec((None, None, 2 * kw, page), lambda s, j, pt: (layer, pt[s, j * n_pg + u], 0, 0))

    gs = pltpu.PrefetchScalarGridSpec(
        num_scalar_prefetch=1,
        grid=(db, n_pages // n_pg),
        in_specs=[pl.BlockSpec((None, NSA_HEADS, ts, hd), lambda s, j, pt: (s, 0, 0, 0)),
                  pl.BlockSpec((None, NSA_KV_HEADS, ts, nbp), lambda s, j, pt: (s, 0, 0, 0))]
                 + [page_spec(u) for u in range(n_pg)]
                 + [pl.BlockSpec((None, ts, 2 * kw), lambda s, j, pt: (s, 0, 0))],
        out_specs=pl.BlockSpec((None, NSA_HEADS, ts, hd), lambda s, j, pt: (s, 0, 0, 0)),
        scratch_shapes=[pltpu.VMEM((rows, nbp), BF16), pltpu.VMEM((rows, 1), F32),
                        pltpu.VMEM((rows, 1), F32), pltpu.VMEM((rows, hd), F32)],
    )
    return pl.pallas_call(
        functools.partial(_sel_sample_kernel, n_pg=n_pg, ts=ts, past_len=past_len),
        grid_spec=gs,
        out_shape=jax.ShapeDtypeStruct((db, NSA_HEADS, ts, hd), BF16),
        compiler_params=_cparams("parallel", "arbitrary"),
        name="nsa_sel_sample",
    )(page_table, q_rt, sel, *([cache_t] * n_pg), kv_new)


def _win_sample_kernel(q_ref, win_ref, kvn_ref, o_ref, *, ts, past_len):
    rows = NSA_HEADS * ts
    gt = NSA_GROUP * ts
    hd, kvh = NSA_HEAD_DIM, NSA_KV_HEADS
    kw = kvh * hd
    w_buf = win_ref.shape[0]
    q = _rows_by_head(q_ref)
    win = win_ref[...].astype(BF16)
    qpos = past_len + lax.broadcasted_iota(jnp.int32, (rows, w_buf), 0) % ts
    kpos = past_len - w_buf + lax.broadcasted_iota(jnp.int32, (rows, w_buf), 1)
    dist = qpos - kpos
    mask = (dist >= 0) & (dist < NSA_WINDOW) & (kpos >= 0)
    m = jnp.full((rows, 1), NEG_INF, F32)
    l = jnp.zeros((rows, 1), F32)
    s = jnp.concatenate([_nt_dot(q[k * gt:(k + 1) * gt], win[:, k * hd:(k + 1) * hd]) for k in range(kvh)], axis=0)
    m, alpha, l, p = _softmax_step(s, mask, m, l)
    acc = jnp.concatenate([_dot(p[k * gt:(k + 1) * gt], win[:, kw + k * hd:kw + (k + 1) * hd])
                           for k in range(kvh)], axis=0)
    new = [_new_token_kv(kvn_ref[...], k, ts, LANES) for k in range(kvh)]
    qi = lax.broadcasted_iota(jnp.int32, (rows, LANES), 0) % ts
    col = lax.broadcasted_iota(jnp.int32, (rows, LANES), 1)
    s = jnp.concatenate([_nt_dot(q[k * gt:(k + 1) * gt], new[k][0]) for k in range(kvh)], axis=0)
    m, alpha, l, p = _softmax_step(s, col <= qi, m, l)
    pv = jnp.concatenate([_dot(p[k * gt:(k + 1) * gt], new[k][1]) for k in range(kvh)], axis=0)
    o = ((alpha * acc + pv) * (1.0 / jnp.maximum(l, 1e-30))).astype(BF16)
    for h in range(NSA_HEADS):
        o_ref[h] = o[h * ts:(h + 1) * ts]


def _win_sample(q_rt, win_state, layer, kv_new, past_len):
    db, _, ts, hd = q_rt.shape
    kw = NSA_KV_HEADS * hd
    w_buf = win_state.shape[2]
    win2 = win_state.reshape(win_state.shape[0], db, w_buf, 2 * kw)
    return pl.pallas_call(
        functools.partial(_win_sample_kernel, ts=ts, past_len=past_len),
        grid=(db,),
        in_specs=[pl.BlockSpec((None, NSA_HEADS, ts, hd), lambda s: (s, 0, 0, 0)),
                  pl.BlockSpec((None, None, w_buf, 2 * kw), lambda s: (layer, s, 0, 0)),
                  pl.BlockSpec((None, ts, 2 * kw), lambda s: (s, 0, 0))],
        out_specs=pl.BlockSpec((None, NSA_HEADS, ts, hd), lambda s: (s, 0, 0, 0)),
        out_shape=jax.ShapeDtypeStruct((db, NSA_HEADS, ts, hd), BF16),
        compiler_params=_cparams("parallel"),
        name="nsa_win_sample",
    )(q_rt, win2, kv_new)


def _nsa_out_kernel(h_ref, oc_ref, os_ref, ow_ref, gate_ref, w_ref, out_ref, o_sc):
    hd = NSA_HEAD_DIM
    for h in range(NSA_HEADS):
        rows = slice(h * hd, (h + 1) * hd)
        o = (gate_ref[3 * h:3 * h + 1, :] * oc_ref[rows, :].astype(F32)
             + gate_ref[3 * h + 1:3 * h + 2, :] * os_ref[rows, :].astype(F32)
             + gate_ref[3 * h + 2:3 * h + 3, :] * ow_ref[rows, :].astype(F32))
        o_sc[rows, :] = o.astype(BF16)
    out_ref[...] = h_ref[...] + _tn_dot(o_sc[...], w_ref[...])


def _nsa_out(h, oct_, ost, owt, gates_t, w_out_bf):
    rows, d = h.shape
    tm = min(512, rows)
    col = lambda i: (0, i)
    ot = pl.BlockSpec((d, tm), col)
    return pl.pallas_call(
        _nsa_out_kernel,
        grid=(rows // tm,),
        in_specs=[pl.BlockSpec((tm, d), lambda i: (i, 0)), ot, ot, ot, pl.BlockSpec((LANES, tm), col),
                  pl.BlockSpec(w_out_bf.shape, lambda i: (0, 0))],
        out_specs=pl.BlockSpec((tm, d), lambda i: (i, 0)),
        out_shape=jax.ShapeDtypeStruct((rows, d), F32),
        scratch_shapes=[pltpu.VMEM((d, tm), BF16)],
        compiler_params=_cparams("parallel"),
        name="nsa_out",
    )(h, oct_, ost, owt, gates_t, w_out_bf)


def _top_desc(s, n):
    vals = []
    cur = s
    for _ in range(n):
        m = jnp.max(cur, axis=0, keepdims=True)
        vals.append(m)
        cur = jnp.where(cur == m, MASKED, cur)
    return jnp.concatenate(vals, axis=0)


def _oddeven_merge_sort_pairs(n):
    def merge(lo, hi, r):
        step = r * 2
        if step < hi - lo:
            yield from merge(lo, hi, step)
            yield from merge(lo + r, hi, step)
            yield from [(i, i + r) for i in range(lo + r, hi - r, step)]
        else:
            yield (lo, lo + r)

    def sort(lo, hi):
        if hi - lo >= 1:
            mid = lo + (hi - lo) // 2
            yield from sort(lo, mid)
            yield from sort(mid + 1, hi)
            yield from merge(lo, hi, 1)

    return list(sort(0, n - 1))


SUBLANES = 8


def _top_sorted(s, n):
    x = [s[v * SUBLANES:(v + 1) * SUBLANES, :] for v in range(n)]

    def exchange(i, j):
        x[i], x[j] = jnp.maximum(x[i], x[j]), jnp.minimum(x[i], x[j])

    for i, j in _oddeven_merge_sort_pairs(n):
        exchange(i, j)
    shift = SUBLANES // 2
    while shift:
        other = [pltpu.roll(v, shift, 0) for v in x]
        x = [jnp.maximum(x[v], other[n - 1 - v]) for v in range(n)]
        dist = n // 2
        while dist:
            for i in range(n):
                if not i & dist:
                    exchange(i, i + dist)
            dist //= 2
        shift //= 2
    return jnp.concatenate([v[0:1, :] for v in x], axis=0)


def _peer_score_kernel(h_ref, g_ref, wq_ref, sk_ref, xt_ref, th_ref, f1_ref, e2_ref):
    half = sk_ref.shape[2]
    xn = _rmsnorm_rows(h_ref[...], g_ref[...])
    xt_ref[...] = xn.T.astype(BF16)
    q = _dot(xn.astype(BF16), wq_ref[...])
    sk1, sk2 = sk_ref[0], sk_ref[1]
    kk = PEER_TOPK
    for h in range(PEER_HEADS):
        q1 = q[:, (2 * h) * half:(2 * h + 1) * half].astype(BF16)
        q2 = q[:, (2 * h + 1) * half:(2 * h + 2) * half].astype(BF16)
        s1 = _nt_dot(sk1, q1)
        s2 = _nt_dot(sk2, q2)
        top_of = _top_sorted if s1.shape[0] == kk * SUBLANES else _top_desc
        t1 = top_of(s1, kk)
        t2 = top_of(s2, kk)
        cand = [t1[0:1] + t2]
        for i in range(1, kk // 2):
            cand.append(t1[i:i + 1] + t2[0:kk // 2])
        cand.append(t1[kk // 2:] + t2[0:1])
        top = _top_desc(jnp.concatenate(cand, axis=0), kk)
        tau = top[kk - 1:kk]
        z = jnp.sum(jnp.exp(top - top[0:1]), axis=0, keepdims=True)
        thr = jnp.full(s1.shape, NO_KEEP, F32)
        for j in range(kk):
            t2j = t2[j:j + 1]
            thr = jnp.where((s1 + t2j) >= tau, t2j, thr)
        m2 = t2[0:1]
        th = jnp.exp(jnp.minimum(thr - m2, 1.0))
        f1 = jnp.exp(s1 - t1[0:1]) * (0.5 / z)
        e2 = jnp.exp(s2 - m2)
        for tc in range(s1.shape[1] // LANES):
            lanes = slice(tc * LANES, (tc + 1) * LANES)
            th_ref[h, tc] = th[:, lanes]
            f1_ref[h, tc] = f1[:, lanes]
            e2_ref[h, tc] = e2[:, lanes]


def _peer_scores(h, g, wq_bf, sk_bf, tt):
    rows, d = h.shape
    n_keys = sk_bf.shape[1]
    nt = rows // tt
    tab = jax.ShapeDtypeStruct((PEER_HEADS, rows // LANES, n_keys, LANES), F32)
    tspec = pl.BlockSpec((PEER_HEADS, tt // LANES, n_keys, LANES), lambda i: (0, i, 0, 0))
    return pl.pallas_call(
        _peer_score_kernel,
        grid=(nt,),
        in_specs=[pl.BlockSpec((tt, d), lambda i: (i, 0)), pl.BlockSpec((1, d), lambda i: (0, 0)),
                  pl.BlockSpec(wq_bf.shape, lambda i: (0, 0)), pl.BlockSpec(sk_bf.shape, lambda i: (0, 0, 0))],
        out_specs=[pl.BlockSpec((d, tt), lambda i: (0, i)), tspec, tspec, tspec],
        out_shape=[jax.ShapeDtypeStruct((d, rows), BF16), tab, tab, tab],
        compiler_params=_cparams("parallel"),
        name="peer_scores",
    )(h, g.reshape(1, d), wq_bf, sk_bf)


PEER_A_PER_STEP = 16
PEER_A_PER_SUB = 8
PEER_A_PER_ACC = 2


def _peer_expert_kernel(xt_ref, u_ref, vt_ref, th_ref, f1_ref, e2_ref, h_ref, out_ref, acc_sc):
    c = pl.program_id(1)
    n_keys = e2_ref.shape[2]
    tt = xt_ref.shape[1]
    a_per_step = th_ref.shape[2]
    sub = PEER_A_PER_SUB * n_keys

    @pl.when(c == 0)
    def _():
        acc_sc[...] = jnp.zeros(acc_sc.shape, F32)

    xt = xt_ref[...]
    total = None
    for sc in range(a_per_step // PEER_A_PER_SUB):
        rows = slice(sc * sub, (sc + 1) * sub)
        ux = _dot(u_ref[rows, :], xt)
        act = ux * (1.0 + lax.erf(ux * (2.0 ** -0.5)))
        g_parts = []
        for tc in range(tt // LANES):
            col = []
            for a0 in range(0, PEER_A_PER_SUB, PEER_A_PER_ACC):
                w = [jnp.zeros((n_keys, LANES), F32) for _ in range(PEER_A_PER_ACC)]
                for h in range(PEER_HEADS):
                    e2 = e2_ref[h, tc]
                    for u in range(PEER_A_PER_ACC):
                        a = sc * PEER_A_PER_SUB + a0 + u
                        keep = e2 >= th_ref[h, tc, a:a + 1, :]
                        w[u] = w[u] + jnp.where(keep, f1_ref[h, tc, a:a + 1, :] * e2, 0.0)
                col += w
            g_parts.append(jnp.concatenate(col, axis=0) * act[:, tc * LANES:(tc + 1) * LANES])
        g = jnp.concatenate(g_parts, axis=1).astype(BF16)
        part = _dot(vt_ref[:, rows], g)
        total = part if total is None else total + part
    acc_sc[...] += total

    @pl.when(c == pl.num_programs(1) - 1)
    def _():
        out_ref[...] = h_ref[...] + acc_sc[...].T


def _peer_experts(h, xt, u_bf, vt_bf, th, f1, e2, tt):
    rows, d = h.shape
    n_exp = u_bf.shape[0]
    n_keys = e2.shape[2]
    ec = PEER_A_PER_STEP * n_keys
    e2spec = pl.BlockSpec((PEER_HEADS, tt // LANES, n_keys, LANES), lambda i, c: (0, i, 0, 0))
    aspec = pl.BlockSpec((PEER_HEADS, tt // LANES, PEER_A_PER_STEP, LANES), lambda i, c: (0, i, c, 0))
    return pl.pallas_call(
        _peer_expert_kernel,
        grid=(rows // tt, n_exp // ec),
        in_specs=[pl.BlockSpec((d, tt), lambda i, c: (0, i)),
                  pl.BlockSpec((ec, d), lambda i, c: (c, 0)),
                  pl.BlockSpec((d, ec), lambda i, c: (0, c)),
                  aspec, aspec, e2spec,
                  pl.BlockSpec((tt, d), lambda i, c: (i, 0))],
        out_specs=pl.BlockSpec((tt, d), lambda i, c: (i, 0)),
        out_shape=jax.ShapeDtypeStruct((rows, d), F32),
        scratch_shapes=[pltpu.VMEM((d, tt), F32)],
        compiler_params=_cparams("parallel", "arbitrary"),
        name="peer_experts",
    )(xt, u_bf, vt_bf, th, f1, e2, h)


def _peer(h, g, wq_bf, sk_bf, u_bf, vt_bf):
    rows = h.shape[0]
    tt = min(512, rows)
    xt, th, f1, e2 = _peer_scores(h, g, wq_bf, sk_bf, tt)
    return _peer_experts(h, xt, u_bf, vt_bf, th, f1, e2, tt)


def _final_norm_kernel(h_ref, g_ref, o_ref):
    o_ref[...] = _rmsnorm_rows(h_ref[...], g_ref[...])


def _final_norm(h, g):
    rows, d = h.shape
    tm = min(512, rows)
    return pl.pallas_call(
        _final_norm_kernel,
        grid=(rows // tm,),
        in_specs=[pl.BlockSpec((tm, d), lambda i: (i, 0)), pl.BlockSpec((1, d), lambda i: (0, 0))],
        out_specs=pl.BlockSpec((tm, d), lambda i: (i, 0)),
        out_shape=jax.ShapeDtypeStruct((rows, d), F32),
        compiler_params=_cparams("parallel"),
        name="final_norm",
    )(h, g.reshape(1, d))


def _heads_from_t(xt, db, ts, hd):
    return xt.reshape(-1, hd, db, ts).transpose(2, 0, 3, 1)


def _heads_to_t(x):
    db, heads, ts, hd = x.shape
    return x.transpose(1, 3, 0, 2).reshape(heads * hd, db * ts)


def kernel(x_prompt, x_sample, cache_diff_kv, cache_nsa_cmp_kv, cache_nsa_sel_kv, state_nsa_win_kv, page_table, norm_mix_g, diff_w_in, diff_lambda, diff_subln_g, diff_w_out, nsa_w_in, nsa_cmp_pos, nsa_cmp_w1, nsa_cmp_b1, nsa_cmp_w2, nsa_cmp_b2, nsa_w_out, norm_ffn_g, peer_wq, peer_subkeys, peer_u, peer_v, final_norm_g):
    batch, seq, d = x_prompt.shape
    db, ts, _ = x_sample.shape
    depth = norm_mix_g.shape[0]
    past_len = page_table.shape[1] * cache_diff_kv.shape[2]
    assert past_len % NSA_BLOCK == 0 and ts < NSA_BLOCK and seq % LANES == 0 and (db * ts) % LANES == 0
    assert state_nsa_win_kv.shape[2] == min(NSA_WINDOW, past_len)

    pos_p = jnp.arange(seq)
    pos_s = jnp.tile(past_len + jnp.arange(ts), db)
    tab_p = _rope_tables(pos_p, DA_HEAD_DIM)
    tab_s = _rope_tables(pos_s, DA_HEAD_DIM)

    hp = x_prompt.reshape(batch * seq, d)
    hs = x_sample.reshape(db * ts, d)
    outs = {k: [] for k in ("diff_p", "diff_s", "cmp_p", "cmp_s", "sel_p", "sel_s", "win_p", "win_s")}

    for i in range(depth):
        g_mix = norm_mix_g[i]
        if i % 2 == 0:
            a = i // 2
            lam_init = 0.8 - 0.6 * math.exp(-0.3 * i)
            w_in = diff_w_in[a].astype(BF16)
            w_out = diff_w_out[a].astype(BF16)
            qt, kvp, kb, vtc = _diff_proj(hp, g_mix, w_in, tab_p, seq)
            ot = _diff_attn_prompt(qt, kb, vtc, diff_lambda[a], diff_subln_g[a], batch, seq, lam_init)
            hp = _outproj(hp, ot, w_out)
            qt_s, kvs, _, _ = _diff_proj(hs, g_mix, w_in, tab_s, db * ts)
            os_ = _diff_attn_sample(qt_s.T.reshape(db, ts, d), cache_diff_kv, a, page_table,
                                    kvs.reshape(db, ts, 2 * d), diff_lambda[a], diff_subln_g[a], lam_init)
            hs = _outproj(hs, os_.reshape(db * ts, d).T, w_out)
            outs["diff_p"].append(kvp.reshape(batch, seq, 2, DA_HEADS, 2 * DA_HEAD_DIM))
            outs["diff_s"].append(kvs.reshape(db, ts, 2, DA_HEADS, 2 * DA_HEAD_DIM))
        else:
            b = i // 2
            hd = NSA_HEAD_DIM
            n_main = d + 3 * 2 * NSA_KV_HEADS * hd
            w_main = nsa_w_in[b][:, :n_main].astype(BF16)
            n_gate = nsa_w_in.shape[2] - n_main
            w_gate = jnp.pad(nsa_w_in[b][:, n_main:], ((0, 0), (0, LANES - n_gate))).astype(BF16)
            w_out = nsa_w_out[b].astype(BF16)
            cw_tok, cw_feat = _compress_weights(nsa_cmp_pos[b], nsa_cmp_w1[b], nsa_cmp_b1[b],
                                                nsa_cmp_w2[b], nsa_cmp_b2[b])
            kv_shape = (2, NSA_KV_HEADS, hd)
            (qplt, qrtt, kvc, kvs_, kvw, ks, vst, kw, vwt, gates_t) = _nsa_proj(hp, g_mix, w_main, w_gate, tab_p, seq)
            kc, vct = _compress_prompt(kvc, cw_tok, batch, seq)
            oct_, sel = _cmp_sel_prompt(qplt, kc, vct, batch, seq)
            ost, owt = _sel_win_prompt(qrtt, ks, vst, kw, vwt, sel, batch, seq)
            hp = _nsa_out(hp, oct_, ost, owt, gates_t, w_out)
            outs["cmp_p"].append(kvc.reshape((batch, seq) + kv_shape))
            outs["sel_p"].append(kvs_.reshape((batch, seq) + kv_shape))
            w_keep = min(NSA_WINDOW, seq)
            outs["win_p"].append(kvw.reshape((batch, seq) + kv_shape)[:, seq - w_keep:])
            (qplt, qrtt, kvc, kvs_, kvw, _, _, _, _, gates_t) = _nsa_proj(hs, g_mix, w_main, w_gate, tab_s, db * ts)
            kc, vc = _compress_sample(cache_nsa_cmp_kv, b, page_table, cw_feat)
            o_c, sel = _cmp_sel_sample(_heads_from_t(qplt, db, ts, hd), kc, vc, past_len)
            qrt_sm = _heads_from_t(qrtt, db, ts, hd)
            o_s = _sel_sample(qrt_sm, sel, cache_nsa_sel_kv, b, page_table, kvs_.reshape(db, ts, -1), past_len)
            o_w = _win_sample(qrt_sm, state_nsa_win_kv, b, kvw.reshape(db, ts, -1), past_len)
            hs = _nsa_out(hs, _heads_to_t(o_c), _heads_to_t(o_s), _heads_to_t(o_w), gates_t, w_out)
            outs["cmp_s"].append(kvc.reshape((db, ts) + kv_shape))
            outs["sel_s"].append(kvs_.reshape((db, ts) + kv_shape))
            win_all = jnp.concatenate([state_nsa_win_kv[b], kvw.reshape((db, ts) + kv_shape)], axis=1)
            outs["win_s"].append(win_all[:, win_all.shape[1] - state_nsa_win_kv.shape[2]:])
        g_ffn = norm_ffn_g[i]
        wq = peer_wq[i].astype(BF16)
        sk = peer_subkeys[i].astype(BF16)
        u_bf = peer_u[i].astype(BF16)
        vt_bf = peer_v[i].T.astype(BF16)
        hp = _peer(hp, g_ffn, wq, sk, u_bf, vt_bf)
        hs = _peer(hs, g_ffn, wq, sk, u_bf, vt_bf)

    y_prompt = _final_norm(hp, final_norm_g).reshape(batch, seq, d)
    y_sample = _final_norm(hs, final_norm_g).reshape(db, ts, d)
    stack = lambda k: jnp.stack(outs[k])
    return (y_prompt, y_sample, stack("diff_p"), stack("diff_s"), stack("cmp_p"), stack("cmp_s"),
            stack("sel_p"), stack("sel_s"), stack("win_p"), stack("win_s"))
```

```python
import functools
import math

import jax
import jax.numpy as jnp
from jax import lax
from jax.experimental import pallas as pl
from jax.experimental.pallas import tpu as pltpu

F32 = jnp.float32
BF16 = jnp.bfloat16

NORM_EPS = 1e-6
ROPE_THETA = 500000.0
ROPE_FRACTION = 4
NEG_INF = -1e30
MASKED = -3.0e38
NO_KEEP = 3.0e38

DA_HEADS = 8
DA_HEAD_DIM = 64
NSA_HEADS = 16
NSA_KV_HEADS = 4
NSA_GROUP = NSA_HEADS // NSA_KV_HEADS
NSA_HEAD_DIM = 64
NSA_BLOCK = 64
NSA_TOPN = 16
NSA_WINDOW = 512
NSA_FORCE_SCORE = 1e4
NSA_V_CHUNK = 256
PEER_HEADS = 8
PEER_TOPK = 16

LANES = 128
SUBLANES = 8
VMEM_LIMIT_BYTES = 56 * 1024 * 1024
ROW_TILE = 512


def _cparams(*sem):
    return pltpu.CompilerParams(dimension_semantics=tuple(sem), vmem_limit_bytes=VMEM_LIMIT_BYTES)


def _dot(a, b):
    return jnp.dot(a, b, preferred_element_type=F32)


def _nt_dot(a, b):
    return lax.dot_general(a, b, (((1,), (1,)), ((), ())), preferred_element_type=F32)


def _tn_dot(a, b):
    return lax.dot_general(a, b, (((0,), (0,)), ((), ())), preferred_element_type=F32)


def _rmsnorm_rows(x, g):
    ms = jnp.mean(x * x, axis=-1, keepdims=True)
    return x * lax.rsqrt(ms + NORM_EPS) * g


def _gelu(x):
    return 0.5 * x * (1.0 + lax.erf(x * (2.0 ** -0.5)))


def _rope_tables(pos, head_dim):
    d_rot = head_dim // ROPE_FRACTION
    half = d_rot // 2
    inv_freq = ROPE_THETA ** (-jnp.arange(half, dtype=F32) / half)
    ang = pos.astype(F32)[:, None] * inv_freq[None, :]
    cos, sin = jnp.cos(ang), jnp.sin(ang)
    n = pos.shape[0]
    zeros = lambda w: jnp.zeros((n, w), F32)
    c = jnp.concatenate([cos, cos, jnp.ones((n, head_dim - d_rot), F32)], axis=1)
    sa = jnp.concatenate([-sin, zeros(head_dim - half)], axis=1)
    sb = jnp.concatenate([zeros(half), sin, zeros(head_dim - d_rot)], axis=1)
    rep = LANES // head_dim
    return jnp.tile(c, (1, rep)), jnp.tile(sa, (1, rep)), jnp.tile(sb, (1, rep)), half


def _rope_cols(y, c, sa, sb, half):
    outs = []
    for j in range(y.shape[1] // LANES):
        ch = y[:, j * LANES:(j + 1) * LANES]
        outs.append(ch * c + pltpu.roll(ch, LANES - half, 1) * sa + pltpu.roll(ch, half, 1) * sb)
    return outs[0] if len(outs) == 1 else jnp.concatenate(outs, axis=1)


def _store_lane_chunks(ref, xt):
    width = ref.shape[2]
    for c in range(xt.shape[1] // width):
        ref[c] = xt[:, c * width:(c + 1) * width]


def _diff_proj_kernel(x_ref, g_ref, w_ref, c_ref, sa_ref, sb_ref,
                      qt_ref, kv_ref, kb_ref, vt_ref, *, half, scale):
    d = x_ref.shape[1]
    xn = _rmsnorm_rows(x_ref[...], g_ref[...]).astype(BF16)
    y = _dot(xn, w_ref[...])
    c, sa, sb = c_ref[...], sa_ref[...], sb_ref[...]
    q = _rope_cols(y[:, :d], c, sa, sb, half) * scale
    k = _rope_cols(y[:, d:2 * d], c, sa, sb, half)
    v = y[:, 2 * d:]
    qt_ref[...] = q.T.astype(BF16)
    kv_ref[:, :d] = k
    kv_ref[:, d:] = v
    kb_ref[...] = k.astype(BF16)
    _store_lane_chunks(vt_ref, v.T.astype(BF16))


def _key_chunk(rows):
    return min(512, rows)


def _diff_proj(x, g, w_bf, tables, period_rows):
    rows, d = x.shape
    c, sa, sb, half = tables
    tm = min(ROW_TILE, rows)
    nper = period_rows // tm
    cw = _key_chunk(tm)
    row = lambda i: (i, 0)
    tab = pl.BlockSpec((tm, LANES), lambda i: (i % nper, 0))
    return pl.pallas_call(
        functools.partial(_diff_proj_kernel, half=half, scale=DA_HEAD_DIM ** -0.5),
        grid=(rows // tm,),
        in_specs=[pl.BlockSpec((tm, d), row), pl.BlockSpec((1, d), lambda i: (0, 0)),
                  pl.BlockSpec((d, 3 * d), lambda i: (0, 0)), tab, tab, tab],
        out_specs=[pl.BlockSpec((d, tm), lambda i: (0, i)), pl.BlockSpec((tm, 2 * d), row),
                   pl.BlockSpec((tm, d), row), pl.BlockSpec((tm // cw, d, cw), lambda i: (i, 0, 0))],
        out_shape=[jax.ShapeDtypeStruct((d, rows), BF16), jax.ShapeDtypeStruct((rows, 2 * d), F32),
                   jax.ShapeDtypeStruct((rows, d), BF16), jax.ShapeDtypeStruct((rows // cw, d, cw), BF16)],
        compiler_params=_cparams("parallel"),
        name="diff_proj",
    )(x, g.reshape(1, d), w_bf, c, sa, sb)


def _diff_lambda(lam_ref, lam_init):
    lv = lam_ref[...]
    a = jnp.sum(lv[0:1, :] * lv[1:2, :], axis=-1, keepdims=True)
    b = jnp.sum(lv[2:3, :] * lv[3:4, :], axis=-1, keepdims=True)
    return jnp.exp(a) - jnp.exp(b) + lam_init


def _subln(o, g, lam_init):
    ms = jnp.mean(o * o, axis=-1, keepdims=True)
    return o * lax.rsqrt(ms + NORM_EPS) * g * (1.0 - lam_init)


def _online_update(s, mask, v, m, l, acc):
    if mask is not None:
        s = jnp.where(mask, s, NEG_INF)
    m_new = jnp.maximum(m, jnp.max(s, axis=-1, keepdims=True))
    alpha = jnp.exp(m - m_new)
    p = jnp.exp(s - m_new)
    if mask is not None:
        p = jnp.where(mask, p, 0.0)
    l_new = alpha * l + jnp.sum(p, axis=-1, keepdims=True)
    acc_new = alpha * acc + _dot(p.astype(BF16), v)
    return m_new, l_new, acc_new


def _online_update_t(st, mask, vt, m, l, acc, some_key_visible=False):
    if mask is not None:
        st = jnp.where(mask, st, NEG_INF)
    m_new = jnp.maximum(m, jnp.max(st, axis=0, keepdims=True))
    alpha = jnp.exp(m - m_new)
    p = jnp.exp(st - m_new)
    if mask is not None and not some_key_visible:
        p = jnp.where(mask, p, 0.0)
    l_new = alpha * l + jnp.sum(p, axis=0, keepdims=True)
    acc_new = alpha * acc + _dot(vt, p.astype(BF16))
    return m_new, l_new, acc_new


def _diff_attn_kernel(qt_ref, k_ref, vt_ref, lam_ref, g_ref, o_ref, *, tq, tk, lam_init):
    i = pl.program_id(2)
    hd2 = qt_ref.shape[0]
    qt = qt_ref[...]
    comp = lax.broadcasted_iota(jnp.int32, (hd2, tq), 0) // (hd2 // 2)
    zero = jnp.zeros_like(qt)
    qq = jnp.concatenate([jnp.where(comp == 0, qt, zero), jnp.where(comp == 1, qt, zero)], axis=1)
    qpos = i * tq + lax.broadcasted_iota(jnp.int32, (tk, 2 * tq), 1) % tq
    krow = lax.broadcasted_iota(jnp.int32, (tk, 2 * tq), 0)

    def step(j, carry, masked):
        start = pl.multiple_of(j * tk, tk)
        st = _dot(k_ref[pl.ds(start, tk), :], qq)
        mask = ((krow + j * tk) <= qpos) if masked else None
        return _online_update_t(st, mask, vt_ref[j], *carry, some_key_visible=True)

    n_full = (i * tq) // tk
    n_kv = ((i + 1) * tq + tk - 1) // tk
    carry = (jnp.full((1, 2 * tq), NEG_INF, F32), jnp.zeros((1, 2 * tq), F32), jnp.zeros((hd2, 2 * tq), F32))
    carry = lax.fori_loop(0, n_full, lambda j, c: step(j, c, False), carry)
    m, l, acc = lax.fori_loop(n_full, n_kv, lambda j, c: step(j, c, True), carry)
    lam = _diff_lambda(lam_ref, lam_init)
    inv = 1.0 / jnp.maximum(l, 1e-30)
    o = acc[:, :tq] * inv[:, :tq] - lam * (acc[:, tq:] * inv[:, tq:])
    ms = jnp.mean(o * o, axis=0, keepdims=True)
    o_ref[...] = (o * lax.rsqrt(ms + NORM_EPS) * g_ref[...] * (1.0 - lam_init)).astype(BF16)


def _diff_attn_prompt(qt, kb, vtc, lam_vec, subln_g, batch, seq, lam_init):
    d, rows = qt.shape
    hd2 = 2 * DA_HEAD_DIM
    tq = min(512, seq)
    tk = vtc.shape[2]
    nq = seq // tq
    return pl.pallas_call(
        functools.partial(_diff_attn_kernel, tq=tq, tk=tk, lam_init=lam_init),
        grid=(batch, DA_HEADS, nq),
        in_specs=[pl.BlockSpec((hd2, tq), lambda b, h, i: (h, b * nq + i)),
                  pl.BlockSpec((seq, hd2), lambda b, h, i: (b, h)),
                  pl.BlockSpec((seq // tk, hd2, tk), lambda b, h, i: (b, h, 0)),
                  pl.BlockSpec(lam_vec.shape, lambda b, h, i: (0, 0)),
                  pl.BlockSpec((hd2, 1), lambda b, h, i: (0, 0))],
        out_specs=pl.BlockSpec((hd2, tq), lambda b, h, i: (h, b * nq + i)),
        out_shape=jax.ShapeDtypeStruct((d, rows), BF16),
        compiler_params=_cparams("parallel", "parallel", "parallel"),
        name="diff_attn_prompt",
    )(qt, kb, vtc, lam_vec, subln_g.reshape(hd2, 1))


def _diff_dec_kernel(pt_ref, q_ref, *rest, n_pg, ts, lam_init):
    page_refs = rest[:n_pg]
    kvn_ref, lam_ref, g_ref, o_ref, qq_sc, m_sc, l_sc, acc_sc = rest[n_pg:]
    j = pl.program_id(1)
    hd2 = 2 * DA_HEAD_DIM
    grp = 2 * ts
    page = page_refs[0].shape[0]
    rows = DA_HEADS * grp

    @pl.when(j == 0)
    def _():
        q = q_ref[...]
        comp = lax.broadcasted_iota(jnp.int32, (ts, hd2), 1) // DA_HEAD_DIM
        for h in range(DA_HEADS):
            qh = q[:, h * hd2:(h + 1) * hd2]
            zero = jnp.zeros_like(qh)
            qq_sc[h * grp:h * grp + ts, :] = jnp.where(comp == 0, qh, zero)
            qq_sc[h * grp + ts:(h + 1) * grp, :] = jnp.where(comp == 1, qh, zero)
        m_sc[...] = jnp.full(m_sc.shape, NEG_INF, F32)
        l_sc[...] = jnp.zeros(l_sc.shape, F32)
        acc_sc[...] = jnp.zeros(acc_sc.shape, F32)

    def attend(k_all, v_all, visible, m, l, acc):
        n_cols = k_all.shape[0]
        s = _nt_dot(qq_sc[...], k_all)
        keep = (lax.broadcasted_iota(jnp.int32, (rows, n_cols), 0) // grp
                == lax.broadcasted_iota(jnp.int32, (rows, n_cols), 1) % DA_HEADS)
        if visible is not None:
            keep = keep & visible
        s = jnp.where(keep, s, NEG_INF)
        m_new = jnp.maximum(m, jnp.max(s, axis=-1, keepdims=True))
        alpha = jnp.exp(m - m_new)
        p = jnp.where(keep, jnp.exp(s - m_new), 0.0)
        l_new = alpha * l + jnp.sum(p, axis=-1, keepdims=True)
        return m_new, l_new, alpha * acc + _dot(p.astype(BF16), v_all)

    def k_and_v(ref, n_tok):
        k = ref[:, 0:DA_HEADS, :].reshape(n_tok * DA_HEADS, hd2).astype(BF16)
        v = ref[:, DA_HEADS:2 * DA_HEADS, :].reshape(n_tok * DA_HEADS, hd2).astype(BF16)
        return k, v

    kv = [k_and_v(ref, page) for ref in page_refs]
    k_all = jnp.concatenate([k for k, _ in kv], axis=0)
    v_all = jnp.concatenate([v for _, v in kv], axis=0)
    m, l, acc = attend(k_all, v_all, None, m_sc[...], l_sc[...], acc_sc[...])
    m_sc[...], l_sc[...], acc_sc[...] = m, l, acc

    @pl.when(j == pl.num_programs(1) - 1)
    def _():
        kn, vn = k_and_v(kvn_ref, ts)
        pad = jnp.zeros((LANES - ts * DA_HEADS, hd2), BF16)
        qi = lax.broadcasted_iota(jnp.int32, (rows, LANES), 0) % ts
        tok = lax.broadcasted_iota(jnp.int32, (rows, LANES), 1) // DA_HEADS
        mf, lf, af = attend(jnp.concatenate([kn, pad], axis=0), jnp.concatenate([vn, pad], axis=0),
                            tok <= qi, m_sc[...], l_sc[...], acc_sc[...])
        lam = _diff_lambda(lam_ref, lam_init)
        af = af * (1.0 / jnp.maximum(lf, 1e-30))
        for h in range(DA_HEADS):
            o = af[h * grp:h * grp + ts] - lam * af[h * grp + ts:(h + 1) * grp]
            o_ref[:, h * hd2:(h + 1) * hd2] = _subln(o, g_ref[...], lam_init).astype(BF16)


def _diff_attn_sample(q, cache, layer, page_table, kv_new, lam_vec, subln_g, lam_init):
    db, ts, d = q.shape
    n_pages = page_table.shape[1]
    page = cache.shape[2]
    hd2 = 2 * DA_HEAD_DIM
    n_pg = 8 if n_pages % 8 == 0 else 1
    n_kvh = 2 * DA_HEADS
    assert ts * DA_HEADS <= LANES
    cache2 = cache.reshape(cache.shape[0], cache.shape[1], page, n_kvh, hd2)
    kv_new = kv_new.reshape(db, ts, n_kvh, hd2)
    rows = 2 * DA_HEADS * ts

    def page_spec(u):
        return pl.BlockSpec((None, None, page, n_kvh, hd2), lambda s, j, pt: (layer, pt[s, j * n_pg + u], 0, 0, 0))

    gs = pltpu.PrefetchScalarGridSpec(
        num_scalar_prefetch=1,
        grid=(db, n_pages // n_pg),
        in_specs=[pl.BlockSpec((None, ts, d), lambda s, j, pt: (s, 0, 0))]
                 + [page_spec(u) for u in range(n_pg)]
                 + [pl.BlockSpec((None, ts, n_kvh, hd2), lambda s, j, pt: (s, 0, 0, 0)),
                    pl.BlockSpec(lam_vec.shape, lambda s, j, pt: (0, 0)),
                    pl.BlockSpec((1, hd2), lambda s, j, pt: (0, 0))],
        out_specs=pl.BlockSpec((None, ts, d), lambda s, j, pt: (s, 0, 0)),
        scratch_shapes=[pltpu.VMEM((rows, hd2), BF16), pltpu.VMEM((rows, 1), F32),
                        pltpu.VMEM((rows, 1), F32), pltpu.VMEM((rows, hd2), F32)],
    )
    return pl.pallas_call(
        functools.partial(_diff_dec_kernel, n_pg=n_pg, ts=ts, lam_init=lam_init),
        grid_spec=gs,
        out_shape=jax.ShapeDtypeStruct((db, ts, d), BF16),
        compiler_params=_cparams("parallel", "arbitrary"),
        name="diff_attn_sample",
    )(page_table, q, *([cache2] * n_pg), kv_new, lam_vec, subln_g.reshape(1, -1))


def _outproj_kernel(h_ref, ot_ref, w_ref, out_ref):
    out_ref[...] = h_ref[...] + _tn_dot(ot_ref[...], w_ref[...])


def _outproj(h, ot, w_bf):
    rows, d = h.shape
    tm = min(ROW_TILE, rows)
    row = lambda i: (i, 0)
    return pl.pallas_call(
        _outproj_kernel,
        grid=(rows // tm,),
        in_specs=[pl.BlockSpec((tm, d), row), pl.BlockSpec((ot.shape[0], tm), lambda i: (0, i)),
                  pl.BlockSpec(w_bf.shape, lambda i: (0, 0))],
        out_specs=pl.BlockSpec((tm, d), row),
        out_shape=jax.ShapeDtypeStruct((rows, d), F32),
        compiler_params=_cparams("parallel"),
        name="outproj",
    )(h, ot, w_bf)


def _nsa_proj_kernel(x_ref, g_ref, w_ref, wg_ref, c_ref, sa_ref, sb_ref,
                     qpl_ref, qrt_ref, kvc_ref, kvs_ref, kvw_ref, ks_ref, vst_ref, kw_ref, vwt_ref, gate_ref,
                     *, half, scale):
    d = x_ref.shape[1]
    hd = NSA_HEAD_DIM
    kw = NSA_KV_HEADS * hd
    xn = _rmsnorm_rows(x_ref[...], g_ref[...]).astype(BF16)
    y = _dot(xn, w_ref[...])
    gl = _dot(xn, wg_ref[...])
    gate_ref[...] = (1.0 / (1.0 + jnp.exp(-gl))).T
    c, sa, sb = c_ref[...], sa_ref[...], sb_ref[...]
    q = y[:, :d]
    qpl_ref[...] = (q * scale).T.astype(BF16)
    qrt_ref[...] = (_rope_cols(q, c, sa, sb, half) * scale).T.astype(BF16)
    kvc_ref[...] = y[:, d:d + 2 * kw]
    off = d + 2 * kw
    for kv_ref, k_ref, vt_ref in ((kvs_ref, ks_ref, vst_ref), (kvw_ref, kw_ref, vwt_ref)):
        k = _rope_cols(y[:, off:off + kw], c, sa, sb, half)
        v = y[:, off + kw:off + 2 * kw]
        kv_ref[:, :kw] = k
        kv_ref[:, kw:] = v
        kb = k.astype(BF16)
        for h in range(NSA_KV_HEADS):
            k_ref[h] = kb[:, h * hd:(h + 1) * hd]
        _store_lane_chunks(vt_ref, v.T.astype(BF16))
        off += 2 * kw


def _nsa_proj(x, g, w_bf, wg_bf, tables, period_rows):
    rows, d = x.shape
    c, sa, sb, half = tables
    tm = min(ROW_TILE, rows)
    nper = period_rows // tm
    hd = NSA_HEAD_DIM
    kw = NSA_KV_HEADS * hd
    row = lambda i: (i, 0)
    col = lambda i: (0, i)
    tab = pl.BlockSpec((tm, LANES), lambda i: (i % nper, 0))
    const = lambda i: (0, 0)
    qt = jax.ShapeDtypeStruct((d, rows), BF16)
    khm = jax.ShapeDtypeStruct((NSA_KV_HEADS, rows, hd), BF16)
    cw = min(NSA_V_CHUNK, tm)
    vtc = jax.ShapeDtypeStruct((rows // cw, kw, cw), BF16)
    kvf = jax.ShapeDtypeStruct((rows, 2 * kw), F32)
    khm_spec = pl.BlockSpec((NSA_KV_HEADS, tm, hd), lambda i: (0, i, 0))
    vtc_spec = pl.BlockSpec((tm // cw, kw, cw), lambda i: (i, 0, 0))
    kv_spec = pl.BlockSpec((tm, 2 * kw), row)
    return pl.pallas_call(
        functools.partial(_nsa_proj_kernel, half=half, scale=hd ** -0.5),
        grid=(rows // tm,),
        in_specs=[pl.BlockSpec((tm, d), row), pl.BlockSpec((1, d), const),
                  pl.BlockSpec(w_bf.shape, const), pl.BlockSpec(wg_bf.shape, const), tab, tab, tab],
        out_specs=[pl.BlockSpec((d, tm), col), pl.BlockSpec((d, tm), col), kv_spec, kv_spec, kv_spec,
                   khm_spec, vtc_spec, khm_spec, vtc_spec, pl.BlockSpec((LANES, tm), col)],
        out_shape=[qt, qt, kvf, kvf, kvf, khm, vtc, khm, vtc, jax.ShapeDtypeStruct((LANES, rows), F32)],
        compiler_params=_cparams("parallel"),
        name="nsa_proj",
    )(x, g.reshape(1, d), w_bf, wg_bf, c, sa, sb)


def _compress_weights(cmp_pos, w1, b1, w2, b2):
    kvh, hd, blk = NSA_KV_HEADS, NSA_HEAD_DIM, NSA_BLOCK
    hid = w1.shape[-1]
    eye = jnp.eye(kvh, dtype=F32)
    eye2 = jnp.eye(2, dtype=F32)
    pos_rep = jnp.broadcast_to(cmp_pos[:, :, None, :], (blk, 2, kvh, hd)).reshape(blk, 2 * kvh * hd)
    w1bd = jnp.einsum('crde,kl->rckdle', w1, eye).reshape(blk, 2, kvh * hd, kvh * hid).astype(BF16)
    b1_rep = jnp.broadcast_to(b1[:, None, :], (2, kvh, hid)).reshape(1, 2 * kvh * hid)
    w2bd = jnp.einsum('aed,ab,kl->akebld', w2, eye2, eye).reshape(2 * kvh * hid, 2 * kvh * hd).astype(BF16)
    b2_rep = jnp.broadcast_to(b2[:, None, :], (2, kvh, hd)).reshape(1, 2 * kvh * hd)
    token_major = (pos_rep, w1bd, b1_rep, w2bd, b2_rep)
    pos_t = jnp.tile(cmp_pos.transpose(1, 2, 0), (1, 1, 2))
    w1_t = jnp.einsum('crde,ab->cdarbe', w1, eye2).reshape(2, hd, 2 * blk, 2 * hid).astype(BF16)
    b1_t = jnp.tile(b1, (1, 2)).reshape(2, 1, 2 * hid)
    w2_t = jnp.einsum('ced,ab->caebd', w2, eye2).reshape(2, 2 * hid, 2 * hd).astype(BF16)
    b2_t = jnp.tile(b2, (1, 2)).reshape(2, 1, 2 * hd)
    feature_major = (pos_t, w1_t, b1_t, w2_t, b2_t)
    return token_major, feature_major


def _compress_prompt_kernel(*refs):
    pos_ref, w1_ref, b1_ref, w2_ref, b2_ref, kc_ref, vct_ref = refs[-7:]
    slabs = refs[:-7]
    n_blk = slabs[0].shape[0] // NSA_BLOCK
    half = w1_ref.shape[2]
    acc = [jnp.zeros((n_blk, half), F32), jnp.zeros((n_blk, half), F32)]
    for r in range(NSA_BLOCK):
        xr = jnp.concatenate([ref[pl.ds(r, n_blk, stride=NSA_BLOCK), :] for ref in slabs], axis=1)
        xr = (xr + pos_ref[r:r + 1, :]).astype(BF16)
        for c in range(2):
            acc[c] = acc[c] + _dot(xr[:, c * half:(c + 1) * half], w1_ref[r, c])
    hid = _gelu(jnp.concatenate(acc, axis=1) + b1_ref[...]).astype(BF16)
    out = _dot(hid, w2_ref[...]) + b2_ref[...]
    hd = NSA_HEAD_DIM
    kb = out[:, :half].astype(BF16)
    for h in range(NSA_KV_HEADS):
        kc_ref[h] = kb[:, h * hd:(h + 1) * hd]
    vct_ref[...] = out[:, half:].T.astype(BF16)


def _compress_prompt(kv_c, cw, batch, seq):
    rows, w = kv_c.shape
    nc = seq // NSA_BLOCK
    chunks = w // LANES
    kw = NSA_KV_HEADS * NSA_HEAD_DIM
    full = lambda a: pl.BlockSpec(a.shape, lambda b: (0,) * a.ndim)
    return pl.pallas_call(
        _compress_prompt_kernel,
        grid=(batch,),
        in_specs=[pl.BlockSpec((seq, LANES), lambda b, q=q: (b, q)) for q in range(chunks)] + [full(a) for a in cw],
        out_specs=[pl.BlockSpec((None, NSA_KV_HEADS, nc, NSA_HEAD_DIM), lambda b: (b, 0, 0, 0)),
                   pl.BlockSpec((None, kw, nc), lambda b: (b, 0, 0))],
        out_shape=[jax.ShapeDtypeStruct((batch, NSA_KV_HEADS, nc, NSA_HEAD_DIM), BF16),
                   jax.ShapeDtypeStruct((batch, kw, nc), BF16)],
        compiler_params=_cparams("parallel"),
        name="nsa_compress_prompt",
    )(*([kv_c] * chunks), *cw)


def _compress_sample_kernel(pt_ref, *rest, n_pg):
    page_refs = rest[:n_pg]
    pos_ref, w1_ref, b1_ref, w2_ref, b2_ref, kc_ref, vc_ref = rest[n_pg:]
    hd, kvh = NSA_HEAD_DIM, NSA_KV_HEADS
    lanes = page_refs[0].shape[1]
    outs = []
    for c in range(2):
        acc = jnp.zeros((n_pg * kvh, w1_ref.shape[3]), F32)
        for dd in range(hd):
            xr = jnp.concatenate([ref[pl.ds(c * kvh * hd + dd, kvh, stride=hd), :] for ref in page_refs], axis=0)
            xr = (xr + pos_ref[c, dd:dd + 1, :]).astype(BF16)
            acc = acc + _dot(xr, w1_ref[c, dd])
        hid = _gelu(acc + b1_ref[c]).astype(BF16)
        outs.append((_dot(hid, w2_ref[c]) + b2_ref[c]).astype(BF16))
    kc_ref[...] = outs[0].reshape(kc_ref.shape)
    vc_ref[...] = outs[1].reshape(vc_ref.shape)


def _feature_major_pages(cache):
    l, p, page = cache.shape[:3]
    return cache.transpose(0, 1, 3, 4, 5, 2).reshape(l, p, -1, page)


def _compress_sample(cache, layer, page_table, cw):
    db, n_pages = page_table.shape
    page = cache.shape[2]
    kvh, hd = NSA_KV_HEADS, NSA_HEAD_DIM
    bpp = page // NSA_BLOCK
    assert bpp == 2
    cache_t = _feature_major_pages(cache)
    n_pg = 16 if n_pages % 16 == 0 else 1
    full = lambda a: pl.BlockSpec(a.shape, lambda s, j, pt: (0,) * a.ndim)

    def page_spec(u):
        return pl.BlockSpec((None, None, cache_t.shape[2], page), lambda s, j, pt: (layer, pt[s, j * n_pg + u], 0, 0))

    out = jax.ShapeDtypeStruct((db, n_pages, kvh, bpp * hd), BF16)
    ospec = pl.BlockSpec((None, n_pg, kvh, bpp * hd), lambda s, j, pt: (s, j, 0, 0))
    gs = pltpu.PrefetchScalarGridSpec(
        num_scalar_prefetch=1,
        grid=(db, n_pages // n_pg),
        in_specs=[page_spec(u) for u in range(n_pg)] + [full(a) for a in cw],
        out_specs=[ospec, ospec],
    )
    kc, vc = pl.pallas_call(
        functools.partial(_compress_sample_kernel, n_pg=n_pg),
        grid_spec=gs,
        out_shape=[out, out],
        compiler_params=_cparams("parallel", "parallel"),
        name="nsa_compress_sample",
    )(page_table, *([cache_t] * n_pg), *cw)
    to_blocks = lambda a: a.reshape(db, n_pages, kvh, bpp, hd).transpose(0, 2, 1, 3, 4).reshape(db, kvh, -1, hd)
    return to_blocks(kc), to_blocks(vc)


def _block_scores(imp, jblk, cur, nb):
    forced = (jblk == 0) | (jblk == cur) | (jblk == cur - 1)
    score = jnp.where(forced, NSA_FORCE_SCORE, imp)
    return jnp.where((jblk <= cur) & (jblk < nb), score, NEG_INF)


def _cmp_sel_prompt_kernel(qt_ref, kc_ref, vct_ref, oc_ref, sel_ref, score_sc, *, tq, nb):
    i = pl.program_id(1)
    nc = kc_ref.shape[1]
    nbp = sel_ref.shape[1]
    nr = score_sc.shape[0]
    hd = NSA_HEAD_DIM
    pos = i * tq + lax.broadcasted_iota(jnp.int32, (nr, tq), 1)
    jblk = lax.broadcasted_iota(jnp.int32, (nr, tq), 0)
    vis = (((lax.broadcasted_iota(jnp.int32, (nc, tq), 0) + 1) * NSA_BLOCK - 1)
           <= i * tq + lax.broadcasted_iota(jnp.int32, (nc, tq), 1))
    cur = pos // NSA_BLOCK
    for k in range(NSA_KV_HEADS):
        kc = kc_ref[k]
        vct = vct_ref[k * hd:(k + 1) * hd, :]
        imp = jnp.zeros((nc, tq), F32)
        for g in range(NSA_GROUP):
            h = k * NSA_GROUP + g
            s = jnp.where(vis, _dot(kc, qt_ref[h * hd:(h + 1) * hd, :]), NEG_INF)
            m = jnp.max(s, axis=0, keepdims=True)
            e = jnp.where(vis, jnp.exp(s - m), 0.0)
            p = e / jnp.maximum(jnp.sum(e, axis=0, keepdims=True), 1e-30)
            imp = imp + p
            oc_ref[h * hd:(h + 1) * hd, :] = _dot(vct, p.astype(BF16)).astype(BF16)
        if nr > nc:
            imp = jnp.concatenate([imp, jnp.zeros((nr - nc, tq), F32)], axis=0)
        score = _block_scores(imp, jblk, cur, nb)
        score_sc[...] = score

        def rank_body(r, rank):
            row = score_sc[pl.ds(r, 1), :]
            ahead = (row > score) | ((row == score) & (r < jblk))
            return rank + jnp.where(ahead, 1.0, 0.0)

        rank = lax.fori_loop(0, nb, rank_body, jnp.zeros((nr, tq), F32))
        sel = jnp.where((rank < NSA_TOPN) & (score > 0.5 * NEG_INF), 1.0, 0.0)
        if nbp > nr:
            sel = jnp.concatenate([sel, jnp.zeros((nbp - nr, tq), F32)], axis=0)
        sel_ref[k] = sel


def _cmp_sel_prompt(qplt, kc, vct, batch, seq):
    d, rows = qplt.shape
    hd = NSA_HEAD_DIM
    tq = min(256, seq)
    nq = seq // tq
    nc = kc.shape[2]
    nb = -(-seq // NSA_BLOCK)
    nbp = -(-nb // LANES) * LANES
    return pl.pallas_call(
        functools.partial(_cmp_sel_prompt_kernel, tq=tq, nb=nb),
        grid=(batch, nq),
        in_specs=[pl.BlockSpec((d, tq), lambda b, i: (0, b * nq + i)),
                  pl.BlockSpec((None, NSA_KV_HEADS, nc, hd), lambda b, i: (b, 0, 0, 0)),
                  pl.BlockSpec((None,) + vct.shape[1:], lambda b, i: (b, 0, 0))],
        out_specs=[pl.BlockSpec((d, tq), lambda b, i: (0, b * nq + i)),
                   pl.BlockSpec((NSA_KV_HEADS, nbp, tq), lambda b, i: (0, 0, b * nq + i))],
        out_shape=[jax.ShapeDtypeStruct((d, rows), BF16),
                   jax.ShapeDtypeStruct((NSA_KV_HEADS, nbp, rows), F32)],
        scratch_shapes=[pltpu.VMEM((-(-nb // SUBLANES) * SUBLANES, tq), F32)],
        compiler_params=_cparams("parallel", "parallel"),
        name="nsa_cmp_select_prompt",
    )(qplt, kc, vct)


def _sel_win_prompt_kernel(qt_ref, ks_ref, vst_ref, kw_ref, vwt_ref, sel_ref, os_ref, ow_ref, *, tq, tk):
    i = pl.program_id(2)
    hd = NSA_HEAD_DIM
    g = NSA_GROUP
    nbp = sel_ref.shape[0]
    tw = vwt_ref.shape[2]
    cps = tk // vst_ref.shape[2]
    q4 = jnp.concatenate([qt_ref[a * hd:(a + 1) * hd, :] for a in range(g)], axis=1)
    flags = sel_ref[...].astype(BF16)
    blk_col = lax.broadcasted_iota(jnp.int32, (tk, nbp), 1)
    blk_of_row = lax.broadcasted_iota(jnp.int32, (tk, nbp), 0) // NSA_BLOCK

    def sel_step(j, carry, diagonal):
        start = pl.multiple_of(j * tk, tk)
        st = _dot(ks_ref[pl.ds(start, tk), :], q4)
        expand = jnp.where(blk_col == blk_of_row + j * (tk // NSA_BLOCK), 1.0, 0.0).astype(BF16)
        chosen = _dot(expand, flags) > 0.5
        if diagonal:
            qpos = i * tq + lax.broadcasted_iota(jnp.int32, (tk, tq), 1)
            chosen = chosen & ((lax.broadcasted_iota(jnp.int32, (tk, tq), 0) + j * tk) <= qpos)
        mask = jnp.concatenate([chosen] * g, axis=1)
        vt = jnp.concatenate([vst_ref[j * cps + u] for u in range(cps)], axis=1)
        return _online_update_t(st, mask, vt, *carry, some_key_visible=True)

    def win_step(j, carry):
        start = pl.multiple_of(j * tw, tw)
        st = _dot(kw_ref[pl.ds(start, tw), :], q4)
        dist = (i * tq + lax.broadcasted_iota(jnp.int32, (tw, tq), 1)
                - (lax.broadcasted_iota(jnp.int32, (tw, tq), 0) + j * tw))
        inside = (dist >= 0) & (dist < NSA_WINDOW)
        return _online_update_t(st, jnp.concatenate([inside] * g, axis=1), vwt_ref[j], *carry)

    def finish(carry, o_ref):
        m, l, acc = carry
        o = (acc * (1.0 / jnp.maximum(l, 1e-30))).astype(BF16)
        for a in range(g):
            o_ref[a * hd:(a + 1) * hd, :] = o[:, a * tq:(a + 1) * tq]

    init = (jnp.full((1, g * tq), NEG_INF, F32), jnp.zeros((1, g * tq), F32), jnp.zeros((hd, g * tq), F32))
    n_full = (i * tq) // tk
    n_kv = ((i + 1) * tq + tk - 1) // tk
    carry = lax.fori_loop(0, n_full, lambda j, c: sel_step(j, c, False), init)
    finish(lax.fori_loop(n_full, n_kv, lambda j, c: sel_step(j, c, True), carry), os_ref)
    first_win = jnp.maximum(i * tq - (NSA_WINDOW - 1), 0) // tw
    finish(lax.fori_loop(first_win, ((i + 1) * tq + tw - 1) // tw, win_step, init), ow_ref)


def _sel_win_prompt(qrtt, ks, vst, kw, vwt, sel, batch, seq):
    d, rows = qrtt.shape
    hd = NSA_HEAD_DIM
    cw = vst.shape[2]
    tq = min(256, seq)
    tk = _key_chunk(seq)
    assert tk % cw == 0 and seq % tk == 0
    nq = seq // tq
    nbp = sel.shape[1]
    gw = NSA_GROUP * hd
    qspec = pl.BlockSpec((gw, tq), lambda b, k, i: (k, b * nq + i))
    kspec = pl.BlockSpec((None, seq, hd), lambda b, k, i: (k, b, 0))
    vspec = pl.BlockSpec((seq // cw, hd, cw), lambda b, k, i: (b, k, 0))
    out = jax.ShapeDtypeStruct((d, rows), BF16)
    return pl.pallas_call(
        functools.partial(_sel_win_prompt_kernel, tq=tq, tk=tk),
        grid=(batch, NSA_KV_HEADS, nq),
        in_specs=[qspec, kspec, vspec, kspec, vspec,
                  pl.BlockSpec((None, nbp, tq), lambda b, k, i: (k, 0, b * nq + i))],
        out_specs=[qspec, qspec],
        out_shape=[out, out],
        compiler_params=_cparams("parallel", "parallel", "parallel"),
        name="nsa_sel_win_prompt",
    )(qrtt, ks, vst, kw, vwt, sel)


def _rows_by_head(q_ref):
    return jnp.concatenate([q_ref[h] for h in range(q_ref.shape[0])], axis=0)


def _cmp_sel_sample_kernel(q_ref, kc_ref, vc_ref, oc_ref, sel_ref, *, ts, past_len):
    nc = kc_ref.shape[1]
    nbp = sel_ref.shape[2]
    nb = -(-(past_len + ts) // NSA_BLOCK)
    gt = NSA_GROUP * ts
    q = _rows_by_head(q_ref)
    pos_g = past_len + lax.broadcasted_iota(jnp.int32, (gt, nc), 0) % ts
    vis = ((lax.broadcasted_iota(jnp.int32, (gt, nc), 1) + 1) * NSA_BLOCK - 1) <= pos_g
    pos = past_len + lax.broadcasted_iota(jnp.int32, (ts, nbp), 0)
    jblk = lax.broadcasted_iota(jnp.int32, (ts, nbp), 1)
    cur = pos // NSA_BLOCK
    for k in range(NSA_KV_HEADS):
        s = jnp.where(vis, _nt_dot(q[k * gt:(k + 1) * gt], kc_ref[k]), NEG_INF)
        m = jnp.max(s, axis=-1, keepdims=True)
        e = jnp.where(vis, jnp.exp(s - m), 0.0)
        p = e / jnp.maximum(jnp.sum(e, axis=-1, keepdims=True), 1e-30)
        o = _dot(p.astype(BF16), vc_ref[k]).astype(BF16)
        imp = p[0:ts]
        for g in range(NSA_GROUP):
            oc_ref[k * NSA_GROUP + g] = o[g * ts:(g + 1) * ts]
            if g:
                imp = imp + p[g * ts:(g + 1) * ts]
        if nbp > nc:
            imp = jnp.concatenate([imp, jnp.zeros((ts, nbp - nc), F32)], axis=1)
        score = _block_scores(imp, jblk, cur, nb)
        rank = jnp.zeros((ts, nbp), F32)
        for r in range(nb):
            col = score[:, r:r + 1]
            ahead = (col > score) | ((col == score) & (r < jblk))
            rank = rank + jnp.where(ahead, 1.0, 0.0)
        sel_ref[k] = jnp.where((rank < NSA_TOPN) & (score > 0.5 * NEG_INF), 1.0, 0.0)


def _cmp_sel_sample(q_pl, kc, vc, past_len):
    db, _, ts, hd = q_pl.shape
    nb = -(-(past_len + ts) // NSA_BLOCK)
    nbp = -(-nb // LANES) * LANES
    seqspec = lambda a: pl.BlockSpec((None,) + a.shape[1:], lambda s: (s, 0, 0, 0))
    return pl.pallas_call(
        functools.partial(_cmp_sel_sample_kernel, ts=ts, past_len=past_len),
        grid=(db,),
        in_specs=[seqspec(q_pl), seqspec(kc), seqspec(vc)],
        out_specs=[pl.BlockSpec((None, NSA_HEADS, ts, hd), lambda s: (s, 0, 0, 0)),
                   pl.BlockSpec((None, NSA_KV_HEADS, ts, nbp), lambda s: (s, 0, 0, 0))],
        out_shape=[jax.ShapeDtypeStruct((db, NSA_HEADS, ts, hd), BF16),
                   jax.ShapeDtypeStruct((db, NSA_KV_HEADS, ts, nbp), F32)],
        compiler_params=_cparams("parallel"),
        name="nsa_cmp_select_sample",
    )(q_pl, kc, vc)


def _new_token_kv(kvn, k, ts, n_rows):
    hd = NSA_HEAD_DIM
    kw = NSA_KV_HEADS * hd
    pad = jnp.zeros((n_rows - ts, hd), F32)
    kn = jnp.concatenate([kvn[:, k * hd:(k + 1) * hd], pad], axis=0).astype(BF16)
    vn = jnp.concatenate([kvn[:, kw + k * hd:kw + (k + 1) * hd], pad], axis=0).astype(BF16)
    return kn, vn


def _softmax_step(s, mask, m, l):
    s = jnp.where(mask, s, NEG_INF)
    m_new = jnp.maximum(m, jnp.max(s, axis=-1, keepdims=True))
    alpha = jnp.exp(m - m_new)
    p = jnp.where(mask, jnp.exp(s - m_new), 0.0)
    return m_new, alpha, alpha * l + jnp.sum(p, axis=-1, keepdims=True), p.astype(BF16)


def _sel_sample_kernel(pt_ref, q_ref, sel_ref, *rest, n_pg, ts, past_len):
    page_refs = rest[:n_pg]
    kvn_ref, o_ref, flag_sc, m_sc, l_sc, acc_sc = rest[n_pg:]
    j = pl.program_id(1)
    rows = NSA_HEADS * ts
    gt = NSA_GROUP * ts
    hd, kvh = NSA_HEAD_DIM, NSA_KV_HEADS
    page = page_refs[0].shape[1]
    nbp = sel_ref.shape[2]

    @pl.when(j == 0)
    def _():
        flag_sc[...] = jnp.concatenate([sel_ref[h // NSA_GROUP] for h in range(NSA_HEADS)], axis=0).astype(BF16)
        m_sc[...] = jnp.full(m_sc.shape, NEG_INF, F32)
        l_sc[...] = jnp.zeros(l_sc.shape, F32)
        acc_sc[...] = jnp.zeros(acc_sc.shape, F32)

    q = _rows_by_head(q_ref)
    flags = flag_sc[...]
    def chosen_keys(first_block, n_keys):
        blk_row = lax.broadcasted_iota(jnp.int32, (nbp, n_keys), 0)
        key_blk = lax.broadcasted_iota(jnp.int32, (nbp, n_keys), 1) // NSA_BLOCK
        expand = jnp.where(blk_row == key_blk + first_block, 1.0, 0.0).astype(BF16)
        return _dot(flags, expand) > 0.5

    s = jnp.concatenate(
        [jnp.concatenate([_dot(q[k * gt:(k + 1) * gt], ref[k * hd:(k + 1) * hd, :].astype(BF16))
                          for k in range(kvh)], axis=0) for ref in page_refs], axis=1)
    mask = chosen_keys(j * n_pg * (page // NSA_BLOCK), n_pg * page)
    m, alpha, l, p = _softmax_step(s, mask, m_sc[...], l_sc[...])
    acc = alpha * acc_sc[...]
    for u, ref in enumerate(page_refs):
        acc = acc + jnp.concatenate(
            [_nt_dot(p[k * gt:(k + 1) * gt, u * page:(u + 1) * page], ref[(kvh + k) * hd:(kvh + k + 1) * hd, :].astype(BF16))
             for k in range(kvh)], axis=0)
    m_sc[...], l_sc[...], acc_sc[...] = m, l, acc

    @pl.when(j == pl.num_programs(1) - 1)
    def _():
        kvn = kvn_ref[...]
        new = [_new_token_kv(kvn, k, ts, page) for k in range(kvh)]
        qi = lax.broadcasted_iota(jnp.int32, (rows, page), 0) % ts
        col = lax.broadcasted_iota(jnp.int32, (rows, page), 1)
        s = jnp.concatenate([_nt_dot(q[k * gt:(k + 1) * gt], new[k][0]) for k in range(kvh)], axis=0)
        mask = chosen_keys(past_len // NSA_BLOCK, page) & (col <= qi)
        mf, alpha, lf, p = _softmax_step(s, mask, m_sc[...], l_sc[...])
        pv = jnp.concatenate([_dot(p[k * gt:(k + 1) * gt], new[k][1]) for k in range(kvh)], axis=0)
        o = ((alpha * acc_sc[...] + pv) * (1.0 / jnp.maximum(lf, 1e-30))).astype(BF16)
        for h in range(NSA_HEADS):
            o_ref[h] = o[h * ts:(h + 1) * ts]


def _sel_sample(q_rt, sel, cache, layer, page_table, kv_new, past_len):
    db, _, ts, hd = q_rt.shape
    n_pages = page_table.shape[1]
    page = cache.shape[2]
    kw = NSA_KV_HEADS * hd
    nbp = sel.shape[3]
    cache_t = _feature_major_pages(cache)
    n_pg = 16 if n_pages % 16 == 0 else 1
    rows = NSA_HEADS * ts

    def page_spec(u):
        return pl.BlockSpec((None, None, 2 * kw, page), lambda s, j, pt: (layer, pt[s, j * n_pg + u], 0, 0))

    gs = pltpu.PrefetchScalarGridSpec(
        num_scalar_prefetch=1,
        grid=(db, n_pages // n_pg),
        in_specs=[pl.BlockSpec((None, NSA_HEADS, ts, hd), lambda s, j, pt: (s, 0, 0, 0)),
                  pl.BlockSpec((None, NSA_KV_HEADS, ts, nbp), lambda s, j, pt: (s, 0, 0, 0))]
                 + [page_spec(u) for u in range(n_pg)]
                 + [pl.BlockSpec((None, ts, 2 * kw), lambda s, j, pt: (s, 0, 0))],
        out_specs=pl.BlockSpec((None, NSA_HEADS, ts, hd), lambda s, j, pt: (s, 0, 0, 0)),
        scratch_shapes=[pltpu.VMEM((rows, nbp), BF16), pltpu.VMEM((rows, 1), F32),
                        pltpu.VMEM((rows, 1), F32), pltpu.VMEM((rows, hd), F32)],
    )
    return pl.pallas_call(
        functools.partial(_sel_sample_kernel, n_pg=n_pg, ts=ts, past_len=past_len),
        grid_spec=gs,
        out_shape=jax.ShapeDtypeStruct((db, NSA_HEADS, ts, hd), BF16),
        compiler_params=_cparams("parallel", "arbitrary"),
        name="nsa_sel_sample",
    )(page_table, q_rt, sel, *([cache_t] * n_pg), kv_new)


def _win_sample_kernel(q_ref, win_ref, kvn_ref, o_ref, *, ts, past_len):
    rows = NSA_HEADS * ts
    gt = NSA_GROUP * ts
    hd, kvh = NSA_HEAD_DIM, NSA_KV_HEADS
    kw = kvh * hd
    w_buf = win_ref.shape[0]
    q = _rows_by_head(q_ref)
    win = win_ref[...].astype(BF16)
    qpos = past_len + lax.broadcasted_iota(jnp.int32, (rows, w_buf), 0) % ts
    kpos = past_len - w_buf + lax.broadcasted_iota(jnp.int32, (rows, w_buf), 1)
    dist = qpos - kpos
    mask = (dist >= 0) & (dist < NSA_WINDOW) & (kpos >= 0)
    m = jnp.full((rows, 1), NEG_INF, F32)
    l = jnp.zeros((rows, 1), F32)
    s = jnp.concatenate([_nt_dot(q[k * gt:(k + 1) * gt], win[:, k * hd:(k + 1) * hd]) for k in range(kvh)], axis=0)
    m, alpha, l, p = _softmax_step(s, mask, m, l)
    acc = jnp.concatenate([_dot(p[k * gt:(k + 1) * gt], win[:, kw + k * hd:kw + (k + 1) * hd])
                           for k in range(kvh)], axis=0)
    new = [_new_token_kv(kvn_ref[...], k, ts, LANES) for k in range(kvh)]
    qi = lax.broadcasted_iota(jnp.int32, (rows, LANES), 0) % ts
    col = lax.broadcasted_iota(jnp.int32, (rows, LANES), 1)
    s = jnp.concatenate([_nt_dot(q[k * gt:(k + 1) * gt], new[k][0]) for k in range(kvh)], axis=0)
    m, alpha, l, p = _softmax_step(s, col <= qi, m, l)
    pv = jnp.concatenate([_dot(p[k * gt:(k + 1) * gt], new[k][1]) for k in range(kvh)], axis=0)
    o = ((alpha * acc + pv) * (1.0 / jnp.maximum(l, 1e-30))).astype(BF16)
    for h in range(NSA_HEADS):
        o_ref[h] = o[h * ts:(h + 1) * ts]


def _win_sample(q_rt, win_state, layer, kv_new, past_len):
    db, _, ts, hd = q_rt.shape
    kw = NSA_KV_HEADS * hd
    w_buf = win_state.shape[2]
    win2 = win_state.reshape(win_state.shape[0], db, w_buf, 2 * kw)
    return pl.pallas_call(
        functools.partial(_win_sample_kernel, ts=ts, past_len=past_len),
        grid=(db,),
        in_specs=[pl.BlockSpec((None, NSA_HEADS, ts, hd), lambda s: (s, 0, 0, 0)),
                  pl.BlockSpec((None, None, w_buf, 2 * kw), lambda s: (layer, s, 0, 0)),
                  pl.BlockSpec((None, ts, 2 * kw), lambda s: (s, 0, 0))],
        out_specs=pl.BlockSpec((None, NSA_HEADS, ts, hd), lambda s: (s, 0, 0, 0)),
        out_shape=jax.ShapeDtypeStruct((db, NSA_HEADS, ts, hd), BF16),
        compiler_params=_cparams("parallel"),
        name="nsa_win_sample",
    )(q_rt, win2, kv_new)


def _nsa_out_kernel(h_ref, oc_ref, os_ref, ow_ref, gate_ref, w_ref, out_ref, o_sc):
    hd = NSA_HEAD_DIM
    for h in range(NSA_HEADS):
        rows = slice(h * hd, (h + 1) * hd)
        o = (gate_ref[3 * h:3 * h + 1, :] * oc_ref[rows, :].astype(F32)
             + gate_ref[3 * h + 1:3 * h + 2, :] * os_ref[rows, :].astype(F32)
             + gate_ref[3 * h + 2:3 * h + 3, :] * ow_ref[rows, :].astype(F32))
        o_sc[rows, :] = o.astype(BF16)
    out_ref[...] = h_ref[...] + _tn_dot(o_sc[...], w_ref[...])


def _nsa_out(h, oct_, ost, owt, gates_t, w_out_bf):
    rows, d = h.shape
    tm = min(ROW_TILE, rows)
    col = lambda i: (0, i)
    ot = pl.BlockSpec((d, tm), col)
    return pl.pallas_call(
        _nsa_out_kernel,
        grid=(rows // tm,),
        in_specs=[pl.BlockSpec((tm, d), lambda i: (i, 0)), ot, ot, ot, pl.BlockSpec((LANES, tm), col),
                  pl.BlockSpec(w_out_bf.shape, lambda i: (0, 0))],
        out_specs=pl.BlockSpec((tm, d), lambda i: (i, 0)),
        out_shape=jax.ShapeDtypeStruct((rows, d), F32),
        scratch_shapes=[pltpu.VMEM((d, tm), BF16)],
        compiler_params=_cparams("parallel"),
        name="nsa_out",
    )(h, oct_, ost, owt, gates_t, w_out_bf)


def _top_desc(s, n):
    vals = []
    cur = s
    for _ in range(n):
        m = jnp.max(cur, axis=0, keepdims=True)
        vals.append(m)
        cur = jnp.where(cur == m, MASKED, cur)
    return jnp.concatenate(vals, axis=0)


def _oddeven_merge_sort_pairs(n):
    def merge(lo, hi, r):
        step = r * 2
        if step < hi - lo:
            yield from merge(lo, hi, step)
            yield from merge(lo + r, hi, step)
            yield from [(i, i + r) for i in range(lo + r, hi - r, step)]
        else:
            yield (lo, lo + r)

    def sort(lo, hi):
        if hi - lo >= 1:
            mid = lo + (hi - lo) // 2
            yield from sort(lo, mid)
            yield from sort(mid + 1, hi)
            yield from merge(lo, hi, 1)

    return list(sort(0, n - 1))


def _top_sorted(s, n):
    x = [s[v * SUBLANES:(v + 1) * SUBLANES, :] for v in range(n)]

    def exchange(i, j):
        x[i], x[j] = jnp.maximum(x[i], x[j]), jnp.minimum(x[i], x[j])

    for i, j in _oddeven_merge_sort_pairs(n):
        exchange(i, j)
    shift = SUBLANES // 2
    while shift:
        other = [pltpu.roll(v, shift, 0) for v in x]
        x = [jnp.maximum(x[v], other[n - 1 - v]) for v in range(n)]
        dist = n // 2
        while dist:
            for i in range(n):
                if not i & dist:
                    exchange(i, i + dist)
            dist //= 2
        shift //= 2
    return jnp.concatenate([v[0:1, :] for v in x], axis=0)


def _peer_score_kernel(h_ref, g_ref, wq_ref, sk_ref, xt_ref, th_ref, f1_ref, e2_ref):
    half = sk_ref.shape[2]
    xn = _rmsnorm_rows(h_ref[...], g_ref[...])
    xt_ref[...] = xn.T.astype(BF16)
    q = _dot(xn.astype(BF16), wq_ref[...])
    sk1, sk2 = sk_ref[0], sk_ref[1]
    kk = PEER_TOPK
    for h in range(PEER_HEADS):
        q1 = q[:, (2 * h) * half:(2 * h + 1) * half].astype(BF16)
        q2 = q[:, (2 * h + 1) * half:(2 * h + 2) * half].astype(BF16)
        s1 = _nt_dot(sk1, q1)
        s2 = _nt_dot(sk2, q2)
        top_of = _top_sorted if s1.shape[0] == kk * SUBLANES else _top_desc
        t1 = top_of(s1, kk)
        t2 = top_of(s2, kk)
        cand = [t1[0:1] + t2]
        for i in range(1, kk // 2):
            cand.append(t1[i:i + 1] + t2[0:kk // 2])
        cand.append(t1[kk // 2:] + t2[0:1])
        top = _top_desc(jnp.concatenate(cand, axis=0), kk)
        tau = top[kk - 1:kk]
        z = jnp.sum(jnp.exp(top - top[0:1]), axis=0, keepdims=True)
        thr = jnp.full(s1.shape, NO_KEEP, F32)
        for j in range(kk):
            t2j = t2[j:j + 1]
            thr = jnp.where((s1 + t2j) >= tau, t2j, thr)
        m2 = t2[0:1]
        th = jnp.exp(jnp.minimum(thr - m2, 1.0))
        f1 = jnp.exp(s1 - t1[0:1]) * (0.5 / z)
        e2 = jnp.exp(s2 - m2)
        for tc in range(s1.shape[1] // LANES):
            lanes = slice(tc * LANES, (tc + 1) * LANES)
            th_ref[h, tc] = th[:, lanes]
            f1_ref[h, tc] = f1[:, lanes]
            e2_ref[h, tc] = e2[:, lanes]


def _peer_scores(h, g, wq_bf, sk_bf, tt):
    rows, d = h.shape
    n_keys = sk_bf.shape[1]
    nt = rows // tt
    tab = jax.ShapeDtypeStruct((PEER_HEADS, rows // LANES, n_keys, LANES), F32)
    tspec = pl.BlockSpec((PEER_HEADS, tt // LANES, n_keys, LANES), lambda i: (0, i, 0, 0))
    return pl.pallas_call(
        _peer_score_kernel,
        grid=(nt,),
        in_specs=[pl.BlockSpec((tt, d), lambda i: (i, 0)), pl.BlockSpec((1, d), lambda i: (0, 0)),
                  pl.BlockSpec(wq_bf.shape, lambda i: (0, 0)), pl.BlockSpec(sk_bf.shape, lambda i: (0, 0, 0))],
        out_specs=[pl.BlockSpec((d, tt), lambda i: (0, i)), tspec, tspec, tspec],
        out_shape=[jax.ShapeDtypeStruct((d, rows), BF16), tab, tab, tab],
        compiler_params=_cparams("parallel"),
        name="peer_scores",
    )(h, g.reshape(1, d), wq_bf, sk_bf)


PEER_A_PER_STEP = 16
PEER_A_PER_SUB = 8
PEER_A_PER_ACC = 2


def _peer_expert_kernel(xt_ref, u_ref, vt_ref, th_ref, f1_ref, e2_ref, h_ref, out_ref, acc_sc):
    c = pl.program_id(1)
    n_keys = e2_ref.shape[2]
    tt = xt_ref.shape[1]
    a_per_step = th_ref.shape[2]
    sub = PEER_A_PER_SUB * n_keys

    @pl.when(c == 0)
    def _():
        acc_sc[...] = jnp.zeros(acc_sc.shape, F32)

    xt = xt_ref[...]
    total = None
    for sc in range(a_per_step // PEER_A_PER_SUB):
        rows = slice(sc * sub, (sc + 1) * sub)
        ux = _dot(u_ref[rows, :], xt)
        act = ux * (1.0 + lax.erf(ux * (2.0 ** -0.5)))
        g_parts = []
        for tc in range(tt // LANES):
            col = []
            for a0 in range(0, PEER_A_PER_SUB, PEER_A_PER_ACC):
                w = [jnp.zeros((n_keys, LANES), F32) for _ in range(PEER_A_PER_ACC)]
                for h in range(PEER_HEADS):
                    e2 = e2_ref[h, tc]
                    for u in range(PEER_A_PER_ACC):
                        a = sc * PEER_A_PER_SUB + a0 + u
                        keep = e2 >= th_ref[h, tc, a:a + 1, :]
                        w[u] = w[u] + jnp.where(keep, f1_ref[h, tc, a:a + 1, :] * e2, 0.0)
                col += w
            g_parts.append(jnp.concatenate(col, axis=0) * act[:, tc * LANES:(tc + 1) * LANES])
        g = jnp.concatenate(g_parts, axis=1).astype(BF16)
        part = _dot(vt_ref[:, rows], g)
        total = part if total is None else total + part
    acc_sc[...] += total

    @pl.when(c == pl.num_programs(1) - 1)
    def _():
        out_ref[...] = h_ref[...] + acc_sc[...].T


def _peer_experts(h, xt, u_bf, vt_bf, layer, th, f1, e2, tt):
    rows, d = h.shape
    n_exp = u_bf.shape[1]
    n_keys = e2.shape[2]
    ec = PEER_A_PER_STEP * n_keys
    e2spec = pl.BlockSpec((PEER_HEADS, tt // LANES, n_keys, LANES), lambda i, c: (0, i, 0, 0))
    aspec = pl.BlockSpec((PEER_HEADS, tt // LANES, PEER_A_PER_STEP, LANES), lambda i, c: (0, i, c, 0))
    return pl.pallas_call(
        _peer_expert_kernel,
        grid=(rows // tt, n_exp // ec),
        in_specs=[pl.BlockSpec((d, tt), lambda i, c: (0, i)),
                  pl.BlockSpec((None, ec, d), lambda i, c: (layer, c, 0)),
                  pl.BlockSpec((None, d, ec), lambda i, c: (layer, 0, c)),
                  aspec, aspec, e2spec,
                  pl.BlockSpec((tt, d), lambda i, c: (i, 0))],
        out_specs=pl.BlockSpec((tt, d), lambda i, c: (i, 0)),
        out_shape=jax.ShapeDtypeStruct((rows, d), F32),
        scratch_shapes=[pltpu.VMEM((d, tt), F32)],
        compiler_params=_cparams("parallel", "arbitrary"),
        name="peer_experts",
    )(xt, u_bf, vt_bf, th, f1, e2, h)


def _peer(h, g, wq_bf, sk_bf, u_bf, vt_bf, layer):
    rows = h.shape[0]
    tt = min(ROW_TILE, rows)
    xt, th, f1, e2 = _peer_scores(h, g, wq_bf, sk_bf, tt)
    return _peer_experts(h, xt, u_bf, vt_bf, layer, th, f1, e2, tt)


def _final_norm_kernel(h_ref, g_ref, o_ref):
    o_ref[...] = _rmsnorm_rows(h_ref[...], g_ref[...])


def _final_norm(h, g):
    rows, d = h.shape
    tm = min(ROW_TILE, rows)
    return pl.pallas_call(
        _final_norm_kernel,
        grid=(rows // tm,),
        in_specs=[pl.BlockSpec((tm, d), lambda i: (i, 0)), pl.BlockSpec((1, d), lambda i: (0, 0))],
        out_specs=pl.BlockSpec((tm, d), lambda i: (i, 0)),
        out_shape=jax.ShapeDtypeStruct((rows, d), F32),
        compiler_params=_cparams("parallel"),
        name="final_norm",
    )(h, g.reshape(1, d))


def _heads_from_t(xt, db, ts, hd):
    return xt.reshape(-1, hd, db, ts).transpose(2, 0, 3, 1)


def _heads_to_t(x):
    db, heads, ts, hd = x.shape
    return x.transpose(1, 3, 0, 2).reshape(heads * hd, db * ts)


def kernel(x_prompt, x_sample, cache_diff_kv, cache_nsa_cmp_kv, cache_nsa_sel_kv, state_nsa_win_kv, page_table, norm_mix_g, diff_w_in, diff_lambda, diff_subln_g, diff_w_out, nsa_w_in, nsa_cmp_pos, nsa_cmp_w1, nsa_cmp_b1, nsa_cmp_w2, nsa_cmp_b2, nsa_w_out, norm_ffn_g, peer_wq, peer_subkeys, peer_u, peer_v, final_norm_g):
    batch, seq, d = x_prompt.shape
    db, ts, _ = x_sample.shape
    depth = norm_mix_g.shape[0]
    past_len = page_table.shape[1] * cache_diff_kv.shape[2]
    assert past_len % NSA_BLOCK == 0 and ts < NSA_BLOCK and seq % LANES == 0 and (db * ts) % LANES == 0
    assert state_nsa_win_kv.shape[2] == min(NSA_WINDOW, past_len)

    pos_p = jnp.arange(seq)
    pos_s = jnp.tile(past_len + jnp.arange(ts), db)
    tab_p = _rope_tables(pos_p, DA_HEAD_DIM)
    tab_s = _rope_tables(pos_s, DA_HEAD_DIM)

    hp = x_prompt.reshape(batch * seq, d)
    hs = x_sample.reshape(db * ts, d)
    outs = {k: [] for k in ("diff_p", "diff_s", "cmp_p", "cmp_s", "sel_p", "sel_s", "win_p", "win_s")}
    u_bf = peer_u.astype(BF16)
    vt_bf = peer_v.transpose(0, 2, 1).astype(BF16)

    for i in range(depth):
        g_mix = norm_mix_g[i]
        if i % 2 == 0:
            a = i // 2
            lam_init = 0.8 - 0.6 * math.exp(-0.3 * i)
            w_in = diff_w_in[a].astype(BF16)
            w_out = diff_w_out[a].astype(BF16)
            qt, kvp, kb, vtc = _diff_proj(hp, g_mix, w_in, tab_p, seq)
            ot = _diff_attn_prompt(qt, kb, vtc, diff_lambda[a], diff_subln_g[a], batch, seq, lam_init)
            hp = _outproj(hp, ot, w_out)
            qt_s, kvs, _, _ = _diff_proj(hs, g_mix, w_in, tab_s, db * ts)
            os_ = _diff_attn_sample(qt_s.T.reshape(db, ts, d), cache_diff_kv, a, page_table,
                                    kvs.reshape(db, ts, 2 * d), diff_lambda[a], diff_subln_g[a], lam_init)
            hs = _outproj(hs, os_.reshape(db * ts, d).T, w_out)
            outs["diff_p"].append(kvp.reshape(batch, seq, 2, DA_HEADS, 2 * DA_HEAD_DIM))
            outs["diff_s"].append(kvs.reshape(db, ts, 2, DA_HEADS, 2 * DA_HEAD_DIM))
        else:
            b = i // 2
            hd = NSA_HEAD_DIM
            n_main = d + 3 * 2 * NSA_KV_HEADS * hd
            w_main = nsa_w_in[b][:, :n_main].astype(BF16)
            n_gate = nsa_w_in.shape[2] - n_main
            w_gate = jnp.pad(nsa_w_in[b][:, n_main:], ((0, 0), (0, LANES - n_gate))).astype(BF16)
            w_out = nsa_w_out[b].astype(BF16)
            cw_tok, cw_feat = _compress_weights(nsa_cmp_pos[b], nsa_cmp_w1[b], nsa_cmp_b1[b],
                                                nsa_cmp_w2[b], nsa_cmp_b2[b])
            kv_shape = (2, NSA_KV_HEADS, hd)
            (qplt, qrtt, kvc, kvs_, kvw, ks, vst, kw, vwt, gates_t) = _nsa_proj(hp, g_mix, w_main, w_gate, tab_p, seq)
            kc, vct = _compress_prompt(kvc, cw_tok, batch, seq)
            oct_, sel = _cmp_sel_prompt(qplt, kc, vct, batch, seq)
            ost, owt = _sel_win_prompt(qrtt, ks, vst, kw, vwt, sel, batch, seq)
            hp = _nsa_out(hp, oct_, ost, owt, gates_t, w_out)
            outs["cmp_p"].append(kvc.reshape((batch, seq) + kv_shape))
            outs["sel_p"].append(kvs_.reshape((batch, seq) + kv_shape))
            w_keep = min(NSA_WINDOW, seq)
            outs["win_p"].append(kvw.reshape((batch, seq) + kv_shape)[:, seq - w_keep:])
            (qplt, qrtt, kvc, kvs_, kvw, _, _, _, _, gates_t) = _nsa_proj(hs, g_mix, w_main, w_gate, tab_s, db * ts)
            kc, vc = _compress_sample(cache_nsa_cmp_kv, b, page_table, cw_feat)
            o_c, sel = _cmp_sel_sample(_heads_from_t(qplt, db, ts, hd), kc, vc, past_len)
            qrt_sm = _heads_from_t(qrtt, db, ts, hd)
            o_s = _sel_sample(qrt_sm, sel, cache_nsa_sel_kv, b, page_table, kvs_.reshape(db, ts, -1), past_len)
            o_w = _win_sample(qrt_sm, state_nsa_win_kv, b, kvw.reshape(db, ts, -1), past_len)
            hs = _nsa_out(hs, _heads_to_t(o_c), _heads_to_t(o_s), _heads_to_t(o_w), gates_t, w_out)
            outs["cmp_s"].append(kvc.reshape((db, ts) + kv_shape))
            outs["sel_s"].append(kvs_.reshape((db, ts) + kv_shape))
            win_all = jnp.concatenate([state_nsa_win_kv[b], kvw.reshape((db, ts) + kv_shape)], axis=1)
            outs["win_s"].append(win_all[:, win_all.shape[1] - state_nsa_win_kv.shape[2]:])
        g_ffn = norm_ffn_g[i]
        wq = peer_wq[i].astype(BF16)
        sk = peer_subkeys[i].astype(BF16)
        hp = _peer(hp, g_ffn, wq, sk, u_bf, vt_bf, i)
        hs = _peer(hs, g_ffn, wq, sk, u_bf, vt_bf, i)

    y_prompt = _final_norm(hp, final_norm_g).reshape(batch, seq, d)
    y_sample = _final_norm(hs, final_norm_g).reshape(db, ts, d)
    stack = lambda k: jnp.stack(outs[k])
    return (y_prompt, y_sample, stack("diff_p"), stack("diff_s"), stack("cmp_p"), stack("cmp_s"),
            stack("sel_p"), stack("sel_s"), stack("win_p"), stack("win_s"))
```

```python
import functools
import math

import jax
import jax.numpy as jnp
from jax import lax
from jax.experimental import pallas as pl
from jax.experimental.pallas import tpu as pltpu

F32 = jnp.float32
BF16 = jnp.bfloat16

NORM_EPS = 1e-6
ROPE_THETA = 500000.0
ROPE_FRACTION = 4
NEG_INF = -1e30
MASKED = -3.0e38
NO_KEEP = 3.0e38

DA_HEADS = 8
DA_HEAD_DIM = 64
NSA_HEADS = 16
NSA_KV_HEADS = 4
NSA_GROUP = NSA_HEADS // NSA_KV_HEADS
NSA_HEAD_DIM = 64
NSA_BLOCK = 64
NSA_TOPN = 16
NSA_WINDOW = 512
NSA_FORCE_SCORE = 1e4
NSA_V_CHUNK = 256
PEER_HEADS = 8
PEER_TOPK = 16

LANES = 128
SUBLANES = 8
VMEM_LIMIT_BYTES = 56 * 1024 * 1024
ROW_TILE = 512


def _cparams(*sem):
    return pltpu.CompilerParams(dimension_semantics=tuple(sem), vmem_limit_bytes=VMEM_LIMIT_BYTES)


def _dot(a, b):
    return jnp.dot(a, b, preferred_element_type=F32)


def _nt_dot(a, b):
    return lax.dot_general(a, b, (((1,), (1,)), ((), ())), preferred_element_type=F32)


def _tn_dot(a, b):
    return lax.dot_general(a, b, (((0,), (0,)), ((), ())), preferred_element_type=F32)


def _rmsnorm_rows(x, g):
    ms = jnp.mean(x * x, axis=-1, keepdims=True)
    return x * lax.rsqrt(ms + NORM_EPS) * g


def _gelu(x):
    return 0.5 * x * (1.0 + lax.erf(x * (2.0 ** -0.5)))


def _rope_tables(pos, head_dim):
    d_rot = head_dim // ROPE_FRACTION
    half = d_rot // 2
    inv_freq = ROPE_THETA ** (-jnp.arange(half, dtype=F32) / half)
    ang = pos.astype(F32)[:, None] * inv_freq[None, :]
    cos, sin = jnp.cos(ang), jnp.sin(ang)
    n = pos.shape[0]
    zeros = lambda w: jnp.zeros((n, w), F32)
    c = jnp.concatenate([cos, cos, jnp.ones((n, head_dim - d_rot), F32)], axis=1)
    sa = jnp.concatenate([-sin, zeros(head_dim - half)], axis=1)
    sb = jnp.concatenate([zeros(half), sin, zeros(head_dim - d_rot)], axis=1)
    rep = LANES // head_dim
    return jnp.tile(c, (1, rep)), jnp.tile(sa, (1, rep)), jnp.tile(sb, (1, rep)), half


def _rope_cols(y, c, sa, sb, half):
    outs = []
    for j in range(y.shape[1] // LANES):
        ch = y[:, j * LANES:(j + 1) * LANES]
        outs.append(ch * c + pltpu.roll(ch, LANES - half, 1) * sa + pltpu.roll(ch, half, 1) * sb)
    return outs[0] if len(outs) == 1 else jnp.concatenate(outs, axis=1)


def _store_lane_chunks(ref, xt):
    width = ref.shape[2]
    for c in range(xt.shape[1] // width):
        ref[c] = xt[:, c * width:(c + 1) * width]


def _diff_proj_kernel(x_ref, g_ref, w_ref, c_ref, sa_ref, sb_ref,
                      qt_ref, kv_ref, kb_ref, vt_ref, *, half, scale):
    d = x_ref.shape[1]
    xn = _rmsnorm_rows(x_ref[...], g_ref[...]).astype(BF16)
    y = _dot(xn, w_ref[...])
    c, sa, sb = c_ref[...], sa_ref[...], sb_ref[...]
    q = _rope_cols(y[:, :d], c, sa, sb, half) * scale
    k = _rope_cols(y[:, d:2 * d], c, sa, sb, half)
    v = y[:, 2 * d:]
    qt_ref[...] = q.T.astype(BF16)
    kv_ref[:, :d] = k
    kv_ref[:, d:] = v
    kb_ref[...] = k.astype(BF16)
    _store_lane_chunks(vt_ref, v.T.astype(BF16))


def _key_chunk(rows):
    return min(512, rows)


def _diff_proj(x, g, w_bf, tables, period_rows):
    rows, d = x.shape
    c, sa, sb, half = tables
    tm = min(ROW_TILE, rows)
    nper = period_rows // tm
    cw = _key_chunk(tm)
    row = lambda i: (i, 0)
    tab = pl.BlockSpec((tm, LANES), lambda i: (i % nper, 0))
    return pl.pallas_call(
        functools.partial(_diff_proj_kernel, half=half, scale=DA_HEAD_DIM ** -0.5),
        grid=(rows // tm,),
        in_specs=[pl.BlockSpec((tm, d), row), pl.BlockSpec((1, d), lambda i: (0, 0)),
                  pl.BlockSpec((d, 3 * d), lambda i: (0, 0)), tab, tab, tab],
        out_specs=[pl.BlockSpec((d, tm), lambda i: (0, i)), pl.BlockSpec((tm, 2 * d), row),
                   pl.BlockSpec((tm, d), row), pl.BlockSpec((tm // cw, d, cw), lambda i: (i, 0, 0))],
        out_shape=[jax.ShapeDtypeStruct((d, rows), BF16), jax.ShapeDtypeStruct((rows, 2 * d), F32),
                   jax.ShapeDtypeStruct((rows, d), BF16), jax.ShapeDtypeStruct((rows // cw, d, cw), BF16)],
        compiler_params=_cparams("parallel"),
        name="diff_proj",
    )(x, g.reshape(1, d), w_bf, c, sa, sb)


def _diff_lambda(lam_ref, lam_init):
    lv = lam_ref[...]
    a = jnp.sum(lv[0:1, :] * lv[1:2, :], axis=-1, keepdims=True)
    b = jnp.sum(lv[2:3, :] * lv[3:4, :], axis=-1, keepdims=True)
    return jnp.exp(a) - jnp.exp(b) + lam_init


def _subln(o, g, lam_init):
    ms = jnp.mean(o * o, axis=-1, keepdims=True)
    return o * lax.rsqrt(ms + NORM_EPS) * g * (1.0 - lam_init)


def _online_update(s, mask, v, m, l, acc):
    if mask is not None:
        s = jnp.where(mask, s, NEG_INF)
    m_new = jnp.maximum(m, jnp.max(s, axis=-1, keepdims=True))
    alpha = jnp.exp(m - m_new)
    p = jnp.exp(s - m_new)
    if mask is not None:
        p = jnp.where(mask, p, 0.0)
    l_new = alpha * l + jnp.sum(p, axis=-1, keepdims=True)
    acc_new = alpha * acc + _dot(p.astype(BF16), v)
    return m_new, l_new, acc_new


def _online_update_t(st, mask, vt, m, l, acc, some_key_visible=False):
    if mask is not None:
        st = jnp.where(mask, st, NEG_INF)
    m_new = jnp.maximum(m, jnp.max(st, axis=0, keepdims=True))
    alpha = jnp.exp(m - m_new)
    p = jnp.exp(st - m_new)
    if mask is not None and not some_key_visible:
        p = jnp.where(mask, p, 0.0)
    l_new = alpha * l + jnp.sum(p, axis=0, keepdims=True)
    acc_new = alpha * acc + _dot(vt, p.astype(BF16))
    return m_new, l_new, acc_new


def _diff_attn_kernel(qt_ref, k_ref, vt_ref, lam_ref, g_ref, o_ref, *, tq, tk, lam_init):
    i = pl.program_id(2)
    hd2 = qt_ref.shape[0]
    qt = qt_ref[...]
    comp = lax.broadcasted_iota(jnp.int32, (hd2, tq), 0) // (hd2 // 2)
    zero = jnp.zeros_like(qt)
    qq = jnp.concatenate([jnp.where(comp == 0, qt, zero), jnp.where(comp == 1, qt, zero)], axis=1)
    qpos = i * tq + lax.broadcasted_iota(jnp.int32, (tk, 2 * tq), 1) % tq
    krow = lax.broadcasted_iota(jnp.int32, (tk, 2 * tq), 0)

    def step(j, carry, masked):
        start = pl.multiple_of(j * tk, tk)
        st = _dot(k_ref[pl.ds(start, tk), :], qq)
        mask = ((krow + j * tk) <= qpos) if masked else None
        return _online_update_t(st, mask, vt_ref[j], *carry, some_key_visible=True)

    n_full = (i * tq) // tk
    n_kv = ((i + 1) * tq + tk - 1) // tk
    carry = (jnp.full((1, 2 * tq), NEG_INF, F32), jnp.zeros((1, 2 * tq), F32), jnp.zeros((hd2, 2 * tq), F32))
    carry = lax.fori_loop(0, n_full, lambda j, c: step(j, c, False), carry)
    m, l, acc = lax.fori_loop(n_full, n_kv, lambda j, c: step(j, c, True), carry)
    lam = _diff_lambda(lam_ref, lam_init)
    inv = 1.0 / jnp.maximum(l, 1e-30)
    o = acc[:, :tq] * inv[:, :tq] - lam * (acc[:, tq:] * inv[:, tq:])
    ms = jnp.mean(o * o, axis=0, keepdims=True)
    o_ref[...] = (o * lax.rsqrt(ms + NORM_EPS) * g_ref[...] * (1.0 - lam_init)).astype(BF16)


def _diff_attn_prompt(qt, kb, vtc, lam_vec, subln_g, batch, seq, lam_init):
    d, rows = qt.shape
    hd2 = 2 * DA_HEAD_DIM
    tq = min(512, seq)
    tk = vtc.shape[2]
    nq = seq // tq
    return pl.pallas_call(
        functools.partial(_diff_attn_kernel, tq=tq, tk=tk, lam_init=lam_init),
        grid=(batch, DA_HEADS, nq),
        in_specs=[pl.BlockSpec((hd2, tq), lambda b, h, i: (h, b * nq + i)),
                  pl.BlockSpec((seq, hd2), lambda b, h, i: (b, h)),
                  pl.BlockSpec((seq // tk, hd2, tk), lambda b, h, i: (b, h, 0)),
                  pl.BlockSpec(lam_vec.shape, lambda b, h, i: (0, 0)),
                  pl.BlockSpec((hd2, 1), lambda b, h, i: (0, 0))],
        out_specs=pl.BlockSpec((hd2, tq), lambda b, h, i: (h, b * nq + i)),
        out_shape=jax.ShapeDtypeStruct((d, rows), BF16),
        compiler_params=_cparams("parallel", "parallel", "parallel"),
        name="diff_attn_prompt",
    )(qt, kb, vtc, lam_vec, subln_g.reshape(hd2, 1))


def _diff_dec_kernel(pt_ref, q_ref, *rest, n_pg, ts, lam_init):
    page_refs = rest[:n_pg]
    kvn_ref, lam_ref, g_ref, o_ref, qq_sc, m_sc, l_sc, acc_sc = rest[n_pg:]
    j = pl.program_id(1)
    hd2 = 2 * DA_HEAD_DIM
    grp = 2 * ts
    page = page_refs[0].shape[0]
    rows = DA_HEADS * grp

    @pl.when(j == 0)
    def _():
        q = q_ref[...]
        comp = lax.broadcasted_iota(jnp.int32, (ts, hd2), 1) // DA_HEAD_DIM
        for h in range(DA_HEADS):
            qh = q[:, h * hd2:(h + 1) * hd2]
            zero = jnp.zeros_like(qh)
            qq_sc[h * grp:h * grp + ts, :] = jnp.where(comp == 0, qh, zero)
            qq_sc[h * grp + ts:(h + 1) * grp, :] = jnp.where(comp == 1, qh, zero)
        m_sc[...] = jnp.full(m_sc.shape, NEG_INF, F32)
        l_sc[...] = jnp.zeros(l_sc.shape, F32)
        acc_sc[...] = jnp.zeros(acc_sc.shape, F32)

    def attend(k_all, v_all, visible, m, l, acc):
        n_cols = k_all.shape[0]
        s = _nt_dot(qq_sc[...], k_all)
        keep = (lax.broadcasted_iota(jnp.int32, (rows, n_cols), 0) // grp
                == lax.broadcasted_iota(jnp.int32, (rows, n_cols), 1) % DA_HEADS)
        if visible is not None:
            keep = keep & visible
        s = jnp.where(keep, s, NEG_INF)
        m_new = jnp.maximum(m, jnp.max(s, axis=-1, keepdims=True))
        alpha = jnp.exp(m - m_new)
        p = jnp.where(keep, jnp.exp(s - m_new), 0.0)
        l_new = alpha * l + jnp.sum(p, axis=-1, keepdims=True)
        return m_new, l_new, alpha * acc + _dot(p.astype(BF16), v_all)

    def k_and_v(ref, n_tok):
        k = ref[:, 0:DA_HEADS, :].reshape(n_tok * DA_HEADS, hd2).astype(BF16)
        v = ref[:, DA_HEADS:2 * DA_HEADS, :].reshape(n_tok * DA_HEADS, hd2).astype(BF16)
        return k, v

    kv = [k_and_v(ref, page) for ref in page_refs]
    k_all = jnp.concatenate([k for k, _ in kv], axis=0)
    v_all = jnp.concatenate([v for _, v in kv], axis=0)
    m, l, acc = attend(k_all, v_all, None, m_sc[...], l_sc[...], acc_sc[...])
    m_sc[...], l_sc[...], acc_sc[...] = m, l, acc

    @pl.when(j == pl.num_programs(1) - 1)
    def _():
        kn, vn = k_and_v(kvn_ref, ts)
        pad = jnp.zeros((LANES - ts * DA_HEADS, hd2), BF16)
        qi = lax.broadcasted_iota(jnp.int32, (rows, LANES), 0) % ts
        tok = lax.broadcasted_iota(jnp.int32, (rows, LANES), 1) // DA_HEADS
        mf, lf, af = attend(jnp.concatenate([kn, pad], axis=0), jnp.concatenate([vn, pad], axis=0),
                            tok <= qi, m_sc[...], l_sc[...], acc_sc[...])
        lam = _diff_lambda(lam_ref, lam_init)
        af = af * (1.0 / jnp.maximum(lf, 1e-30))
        for h in range(DA_HEADS):
            o = af[h * grp:h * grp + ts] - lam * af[h * grp + ts:(h + 1) * grp]
            o_ref[:, h * hd2:(h + 1) * hd2] = _subln(o, g_ref[...], lam_init).astype(BF16)


def _diff_attn_sample(q, cache, layer, page_table, kv_new, lam_vec, subln_g, lam_init):
    db, ts, d = q.shape
    n_pages = page_table.shape[1]
    page = cache.shape[2]
    hd2 = 2 * DA_HEAD_DIM
    n_pg = 8 if n_pages % 8 == 0 else 1
    n_kvh = 2 * DA_HEADS
    assert ts * DA_HEADS <= LANES
    cache2 = cache.reshape(cache.shape[0], cache.shape[1], page, n_kvh, hd2)
    kv_new = kv_new.reshape(db, ts, n_kvh, hd2)
    rows = 2 * DA_HEADS * ts

    def page_spec(u):
        return pl.BlockSpec((None, None, page, n_kvh, hd2), lambda s, j, pt: (layer, pt[s, j * n_pg + u], 0, 0, 0))

    gs = pltpu.PrefetchScalarGridSpec(
        num_scalar_prefetch=1,
        grid=(db, n_pages // n_pg),
        in_specs=[pl.BlockSpec((None, ts, d), lambda s, j, pt: (s, 0, 0))]
                 + [page_spec(u) for u in range(n_pg)]
                 + [pl.BlockSpec((None, ts, n_kvh, hd2), lambda s, j, pt: (s, 0, 0, 0)),
                    pl.BlockSpec(lam_vec.shape, lambda s, j, pt: (0, 0)),
                    pl.BlockSpec((1, hd2), lambda s, j, pt: (0, 0))],
        out_specs=pl.BlockSpec((None, ts, d), lambda s, j, pt: (s, 0, 0)),
        scratch_shapes=[pltpu.VMEM((rows, hd2), BF16), pltpu.VMEM((rows, 1), F32),
                        pltpu.VMEM((rows, 1), F32), pltpu.VMEM((rows, hd2), F32)],
    )
    return pl.pallas_call(
        functools.partial(_diff_dec_kernel, n_pg=n_pg, ts=ts, lam_init=lam_init),
        grid_spec=gs,
        out_shape=jax.ShapeDtypeStruct((db, ts, d), BF16),
        compiler_params=_cparams("parallel", "arbitrary"),
        name="diff_attn_sample",
    )(page_table, q, *([cache2] * n_pg), kv_new, lam_vec, subln_g.reshape(1, -1))


def _outproj_kernel(h_ref, ot_ref, w_ref, out_ref):
    out_ref[...] = h_ref[...] + _tn_dot(ot_ref[...], w_ref[...])


def _outproj(h, ot, w_bf):
    rows, d = h.shape
    tm = min(ROW_TILE, rows)
    row = lambda i: (i, 0)
    return pl.pallas_call(
        _outproj_kernel,
        grid=(rows // tm,),
        in_specs=[pl.BlockSpec((tm, d), row), pl.BlockSpec((ot.shape[0], tm), lambda i: (0, i)),
                  pl.BlockSpec(w_bf.shape, lambda i: (0, 0))],
        out_specs=pl.BlockSpec((tm, d), row),
        out_shape=jax.ShapeDtypeStruct((rows, d), F32),
        compiler_params=_cparams("parallel"),
        name="outproj",
    )(h, ot, w_bf)


def _nsa_proj_kernel(x_ref, g_ref, w_ref, wg_ref, c_ref, sa_ref, sb_ref,
                     qpl_ref, qrt_ref, kvc_ref, kvs_ref, kvw_ref, ks_ref, vst_ref, kw_ref, vwt_ref, gate_ref,
                     *, half, scale):
    d = x_ref.shape[1]
    hd = NSA_HEAD_DIM
    kw = NSA_KV_HEADS * hd
    xn = _rmsnorm_rows(x_ref[...], g_ref[...]).astype(BF16)
    y = _dot(xn, w_ref[...])
    gl = _dot(xn, wg_ref[...])
    gate_ref[...] = (1.0 / (1.0 + jnp.exp(-gl))).T
    c, sa, sb = c_ref[...], sa_ref[...], sb_ref[...]
    q = y[:, :d]
    qpl_ref[...] = (q * scale).T.astype(BF16)
    qrt_ref[...] = (_rope_cols(q, c, sa, sb, half) * scale).T.astype(BF16)
    kvc_ref[...] = y[:, d:d + 2 * kw]
    off = d + 2 * kw
    for kv_ref, k_ref, vt_ref in ((kvs_ref, ks_ref, vst_ref), (kvw_ref, kw_ref, vwt_ref)):
        k = _rope_cols(y[:, off:off + kw], c, sa, sb, half)
        v = y[:, off + kw:off + 2 * kw]
        kv_ref[:, :kw] = k
        kv_ref[:, kw:] = v
        kb = k.astype(BF16)
        for h in range(NSA_KV_HEADS):
            k_ref[h] = kb[:, h * hd:(h + 1) * hd]
        _store_lane_chunks(vt_ref, v.T.astype(BF16))
        off += 2 * kw


def _nsa_proj(x, g, w_bf, wg_bf, tables, period_rows):
    rows, d = x.shape
    c, sa, sb, half = tables
    tm = min(ROW_TILE, rows)
    nper = period_rows // tm
    hd = NSA_HEAD_DIM
    kw = NSA_KV_HEADS * hd
    row = lambda i: (i, 0)
    col = lambda i: (0, i)
    tab = pl.BlockSpec((tm, LANES), lambda i: (i % nper, 0))
    const = lambda i: (0, 0)
    qt = jax.ShapeDtypeStruct((d, rows), BF16)
    khm = jax.ShapeDtypeStruct((NSA_KV_HEADS, rows, hd), BF16)
    cw = min(NSA_V_CHUNK, tm)
    vtc = jax.ShapeDtypeStruct((rows // cw, kw, cw), BF16)
    kvf = jax.ShapeDtypeStruct((rows, 2 * kw), F32)
    khm_spec = pl.BlockSpec((NSA_KV_HEADS, tm, hd), lambda i: (0, i, 0))
    vtc_spec = pl.BlockSpec((tm // cw, kw, cw), lambda i: (i, 0, 0))
    kv_spec = pl.BlockSpec((tm, 2 * kw), row)
    return pl.pallas_call(
        functools.partial(_nsa_proj_kernel, half=half, scale=hd ** -0.5),
        grid=(rows // tm,),
        in_specs=[pl.BlockSpec((tm, d), row), pl.BlockSpec((1, d), const),
                  pl.BlockSpec(w_bf.shape, const), pl.BlockSpec(wg_bf.shape, const), tab, tab, tab],
        out_specs=[pl.BlockSpec((d, tm), col), pl.BlockSpec((d, tm), col), kv_spec, kv_spec, kv_spec,
                   khm_spec, vtc_spec, khm_spec, vtc_spec, pl.BlockSpec((LANES, tm), col)],
        out_shape=[qt, qt, kvf, kvf, kvf, khm, vtc, khm, vtc, jax.ShapeDtypeStruct((LANES, rows), F32)],
        compiler_params=_cparams("parallel"),
        name="nsa_proj",
    )(x, g.reshape(1, d), w_bf, wg_bf, c, sa, sb)


def _compress_weights(cmp_pos, w1, b1, w2, b2):
    kvh, hd, blk = NSA_KV_HEADS, NSA_HEAD_DIM, NSA_BLOCK
    hid = w1.shape[-1]
    eye = jnp.eye(kvh, dtype=F32)
    eye2 = jnp.eye(2, dtype=F32)
    pos_rep = jnp.broadcast_to(cmp_pos[:, :, None, :], (blk, 2, kvh, hd)).reshape(blk, 2 * kvh * hd)
    w1bd = jnp.einsum('crde,kl->rckdle', w1, eye).reshape(blk, 2, kvh * hd, kvh * hid).astype(BF16)
    b1_rep = jnp.broadcast_to(b1[:, None, :], (2, kvh, hid)).reshape(1, 2 * kvh * hid)
    w2bd = jnp.einsum('aed,ab,kl->akebld', w2, eye2, eye).reshape(2 * kvh * hid, 2 * kvh * hd).astype(BF16)
    b2_rep = jnp.broadcast_to(b2[:, None, :], (2, kvh, hd)).reshape(1, 2 * kvh * hd)
    token_major = (pos_rep, w1bd, b1_rep, w2bd, b2_rep)
    pos_t = jnp.tile(cmp_pos.transpose(1, 2, 0), (1, 1, 2))
    w1_t = jnp.einsum('crde,ab->cdarbe', w1, eye2).reshape(2, hd, 2 * blk, 2 * hid).astype(BF16)
    b1_t = jnp.tile(b1, (1, 2)).reshape(2, 1, 2 * hid)
    w2_t = jnp.einsum('ced,ab->caebd', w2, eye2).reshape(2, 2 * hid, 2 * hd).astype(BF16)
    b2_t = jnp.tile(b2, (1, 2)).reshape(2, 1, 2 * hd)
    feature_major = (pos_t, w1_t, b1_t, w2_t, b2_t)
    return token_major, feature_major


def _compress_prompt_kernel(*refs):
    pos_ref, w1_ref, b1_ref, w2_ref, b2_ref, kc_ref, vct_ref = refs[-7:]
    slabs = refs[:-7]
    n_blk = slabs[0].shape[0] // NSA_BLOCK
    half = w1_ref.shape[2]
    acc = [jnp.zeros((n_blk, half), F32), jnp.zeros((n_blk, half), F32)]
    for r in range(NSA_BLOCK):
        xr = jnp.concatenate([ref[pl.ds(r, n_blk, stride=NSA_BLOCK), :] for ref in slabs], axis=1)
        xr = (xr + pos_ref[r:r + 1, :]).astype(BF16)
        for c in range(2):
            acc[c] = acc[c] + _dot(xr[:, c * half:(c + 1) * half], w1_ref[r, c])
    hid = _gelu(jnp.concatenate(acc, axis=1) + b1_ref[...]).astype(BF16)
    out = _dot(hid, w2_ref[...]) + b2_ref[...]
    hd = NSA_HEAD_DIM
    kb = out[:, :half].astype(BF16)
    for h in range(NSA_KV_HEADS):
        kc_ref[h] = kb[:, h * hd:(h + 1) * hd]
    vct_ref[...] = out[:, half:].T.astype(BF16)


def _compress_prompt(kv_c, cw, batch, seq):
    rows, w = kv_c.shape
    nc = seq // NSA_BLOCK
    chunks = w // LANES
    kw = NSA_KV_HEADS * NSA_HEAD_DIM
    full = lambda a: pl.BlockSpec(a.shape, lambda b: (0,) * a.ndim)
    return pl.pallas_call(
        _compress_prompt_kernel,
        grid=(batch,),
        in_specs=[pl.BlockSpec((seq, LANES), lambda b, q=q: (b, q)) for q in range(chunks)] + [full(a) for a in cw],
        out_specs=[pl.BlockSpec((None, NSA_KV_HEADS, nc, NSA_HEAD_DIM), lambda b: (b, 0, 0, 0)),
                   pl.BlockSpec((None, kw, nc), lambda b: (b, 0, 0))],
        out_shape=[jax.ShapeDtypeStruct((batch, NSA_KV_HEADS, nc, NSA_HEAD_DIM), BF16),
                   jax.ShapeDtypeStruct((batch, kw, nc), BF16)],
        compiler_params=_cparams("parallel"),
        name="nsa_compress_prompt",
    )(*([kv_c] * chunks), *cw)


def _compress_sample_kernel(pt_ref, *rest, n_pg):
    page_refs = rest[:n_pg]
    pos_ref, w1_ref, b1_ref, w2_ref, b2_ref, kc_ref, vc_ref = rest[n_pg:]
    hd, kvh = NSA_HEAD_DIM, NSA_KV_HEADS
    lanes = page_refs[0].shape[1]
    outs = []
    for c in range(2):
        acc = jnp.zeros((n_pg * kvh, w1_ref.shape[3]), F32)
        for dd in range(hd):
            xr = jnp.concatenate([ref[pl.ds(c * kvh * hd + dd, kvh, stride=hd), :] for ref in page_refs], axis=0)
            xr = (xr + pos_ref[c, dd:dd + 1, :]).astype(BF16)
            acc = acc + _dot(xr, w1_ref[c, dd])
        hid = _gelu(acc + b1_ref[c]).astype(BF16)
        outs.append((_dot(hid, w2_ref[c]) + b2_ref[c]).astype(BF16))
    kc_ref[...] = outs[0].reshape(kc_ref.shape)
    vc_ref[...] = outs[1].reshape(vc_ref.shape)


def _feature_major_pages(cache):
    l, p, page = cache.shape[:3]
    return cache.transpose(0, 1, 3, 4, 5, 2).reshape(l, p, -1, page)


def _compress_sample(cache, layer, page_table, cw):
    db, n_pages = page_table.shape
    page = cache.shape[2]
    kvh, hd = NSA_KV_HEADS, NSA_HEAD_DIM
    bpp = page // NSA_BLOCK
    assert bpp == 2
    cache_t = _feature_major_pages(cache)
    n_pg = 16 if n_pages % 16 == 0 else 1
    full = lambda a: pl.BlockSpec(a.shape, lambda s, j, pt: (0,) * a.ndim)

    def page_spec(u):
        return pl.BlockSpec((None, None, cache_t.shape[2], page), lambda s, j, pt: (layer, pt[s, j * n_pg + u], 0, 0))

    out = jax.ShapeDtypeStruct((db, n_pages, kvh, bpp * hd), BF16)
    ospec = pl.BlockSpec((None, n_pg, kvh, bpp * hd), lambda s, j, pt: (s, j, 0, 0))
    gs = pltpu.PrefetchScalarGridSpec(
        num_scalar_prefetch=1,
        grid=(db, n_pages // n_pg),
        in_specs=[page_spec(u) for u in range(n_pg)] + [full(a) for a in cw],
        out_specs=[ospec, ospec],
    )
    kc, vc = pl.pallas_call(
        functools.partial(_compress_sample_kernel, n_pg=n_pg),
        grid_spec=gs,
        out_shape=[out, out],
        compiler_params=_cparams("parallel", "parallel"),
        name="nsa_compress_sample",
    )(page_table, *([cache_t] * n_pg), *cw)
    to_blocks = lambda a: a.reshape(db, n_pages, kvh, bpp, hd).transpose(0, 2, 1, 3, 4).reshape(db, kvh, -1, hd)
    return to_blocks(kc), to_blocks(vc)


def _block_scores(imp, jblk, cur, nb):
    forced = (jblk == 0) | (jblk == cur) | (jblk == cur - 1)
    score = jnp.where(forced, NSA_FORCE_SCORE, imp)
    return jnp.where((jblk <= cur) & (jblk < nb), score, NEG_INF)


def _cmp_sel_prompt_kernel(qt_ref, kc_ref, vct_ref, oc_ref, sel_ref, score_sc, *, tq, nb):
    i = pl.program_id(1)
    nc = kc_ref.shape[1]
    nbp = sel_ref.shape[1]
    nr = score_sc.shape[0]
    hd = NSA_HEAD_DIM
    pos = i * tq + lax.broadcasted_iota(jnp.int32, (nr, tq), 1)
    jblk = lax.broadcasted_iota(jnp.int32, (nr, tq), 0)
    vis = (((lax.broadcasted_iota(jnp.int32, (nc, tq), 0) + 1) * NSA_BLOCK - 1)
           <= i * tq + lax.broadcasted_iota(jnp.int32, (nc, tq), 1))
    cur = pos // NSA_BLOCK
    for k in range(NSA_KV_HEADS):
        kc = kc_ref[k]
        vct = vct_ref[k * hd:(k + 1) * hd, :]
        imp = jnp.zeros((nc, tq), F32)
        for g in range(NSA_GROUP):
            h = k * NSA_GROUP + g
            s = jnp.where(vis, _dot(kc, qt_ref[h * hd:(h + 1) * hd, :]), NEG_INF)
            m = jnp.max(s, axis=0, keepdims=True)
            e = jnp.where(vis, jnp.exp(s - m), 0.0)
            p = e / jnp.maximum(jnp.sum(e, axis=0, keepdims=True), 1e-30)
            imp = imp + p
            oc_ref[h * hd:(h + 1) * hd, :] = _dot(vct, p.astype(BF16)).astype(BF16)
        if nr > nc:
            imp = jnp.concatenate([imp, jnp.zeros((nr - nc, tq), F32)], axis=0)
        score = _block_scores(imp, jblk, cur, nb)
        score_sc[...] = score

        def rank_body(r, rank):
            row = score_sc[pl.ds(r, 1), :]
            ahead = (row > score) | ((row == score) & (r < jblk))
            return rank + jnp.where(ahead, 1.0, 0.0)

        rank = lax.fori_loop(0, nb, rank_body, jnp.zeros((nr, tq), F32))
        sel = jnp.where((rank < NSA_TOPN) & (score > 0.5 * NEG_INF), 1.0, 0.0)
        if nbp > nr:
            sel = jnp.concatenate([sel, jnp.zeros((nbp - nr, tq), F32)], axis=0)
        sel_ref[k] = sel


def _cmp_sel_prompt(qplt, kc, vct, batch, seq):
    d, rows = qplt.shape
    hd = NSA_HEAD_DIM
    tq = min(256, seq)
    nq = seq // tq
    nc = kc.shape[2]
    nb = -(-seq // NSA_BLOCK)
    nbp = -(-nb // LANES) * LANES
    return pl.pallas_call(
        functools.partial(_cmp_sel_prompt_kernel, tq=tq, nb=nb),
        grid=(batch, nq),
        in_specs=[pl.BlockSpec((d, tq), lambda b, i: (0, b * nq + i)),
                  pl.BlockSpec((None, NSA_KV_HEADS, nc, hd), lambda b, i: (b, 0, 0, 0)),
                  pl.BlockSpec((None,) + vct.shape[1:], lambda b, i: (b, 0, 0))],
        out_specs=[pl.BlockSpec((d, tq), lambda b, i: (0, b * nq + i)),
                   pl.BlockSpec((NSA_KV_HEADS, nbp, tq), lambda b, i: (0, 0, b * nq + i))],
        out_shape=[jax.ShapeDtypeStruct((d, rows), BF16),
                   jax.ShapeDtypeStruct((NSA_KV_HEADS, nbp, rows), F32)],
        scratch_shapes=[pltpu.VMEM((-(-nb // SUBLANES) * SUBLANES, tq), F32)],
        compiler_params=_cparams("parallel", "parallel"),
        name="nsa_cmp_select_prompt",
    )(qplt, kc, vct)


def _sel_win_prompt_kernel(qt_ref, ks_ref, vst_ref, kw_ref, vwt_ref, sel_ref, os_ref, ow_ref, *, tq, tk):
    i = pl.program_id(2)
    hd = NSA_HEAD_DIM
    g = NSA_GROUP
    nbp = sel_ref.shape[0]
    tw = vwt_ref.shape[2]
    cps = tk // vst_ref.shape[2]
    q4 = jnp.concatenate([qt_ref[a * hd:(a + 1) * hd, :] for a in range(g)], axis=1)
    flags = sel_ref[...].astype(BF16)
    blk_col = lax.broadcasted_iota(jnp.int32, (tk, nbp), 1)
    blk_of_row = lax.broadcasted_iota(jnp.int32, (tk, nbp), 0) // NSA_BLOCK

    def sel_step(j, carry, diagonal):
        start = pl.multiple_of(j * tk, tk)
        st = _dot(ks_ref[pl.ds(start, tk), :], q4)
        expand = jnp.where(blk_col == blk_of_row + j * (tk // NSA_BLOCK), 1.0, 0.0).astype(BF16)
        chosen = _dot(expand, flags) > 0.5
        if diagonal:
            qpos = i * tq + lax.broadcasted_iota(jnp.int32, (tk, tq), 1)
            chosen = chosen & ((lax.broadcasted_iota(jnp.int32, (tk, tq), 0) + j * tk) <= qpos)
        mask = jnp.concatenate([chosen] * g, axis=1)
        vt = jnp.concatenate([vst_ref[j * cps + u] for u in range(cps)], axis=1)
        return _online_update_t(st, mask, vt, *carry, some_key_visible=True)

    def win_step(j, carry):
        start = pl.multiple_of(j * tw, tw)
        st = _dot(kw_ref[pl.ds(start, tw), :], q4)
        dist = (i * tq + lax.broadcasted_iota(jnp.int32, (tw, tq), 1)
                - (lax.broadcasted_iota(jnp.int32, (tw, tq), 0) + j * tw))
        inside = (dist >= 0) & (dist < NSA_WINDOW)
        return _online_update_t(st, jnp.concatenate([inside] * g, axis=1), vwt_ref[j], *carry)

    def finish(carry, o_ref):
        m, l, acc = carry
        o = (acc * (1.0 / jnp.maximum(l, 1e-30))).astype(BF16)
        for a in range(g):
            o_ref[a * hd:(a + 1) * hd, :] = o[:, a * tq:(a + 1) * tq]

    init = (jnp.full((1, g * tq), NEG_INF, F32), jnp.zeros((1, g * tq), F32), jnp.zeros((hd, g * tq), F32))
    n_full = (i * tq) // tk
    n_kv = ((i + 1) * tq + tk - 1) // tk
    carry = lax.fori_loop(0, n_full, lambda j, c: sel_step(j, c, False), init)
    finish(lax.fori_loop(n_full, n_kv, lambda j, c: sel_step(j, c, True), carry), os_ref)
    first_win = jnp.maximum(i * tq - (NSA_WINDOW - 1), 0) // tw
    finish(lax.fori_loop(first_win, ((i + 1) * tq + tw - 1) // tw, win_step, init), ow_ref)


def _sel_win_prompt(qrtt, ks, vst, kw, vwt, sel, batch, seq):
    d, rows = qrtt.shape
    hd = NSA_HEAD_DIM
    cw = vst.shape[2]
    tq = min(256, seq)
    tk = _key_chunk(seq)
    assert tk % cw == 0 and seq % tk == 0
    nq = seq // tq
    nbp = sel.shape[1]
    gw = NSA_GROUP * hd
    qspec = pl.BlockSpec((gw, tq), lambda b, k, i: (k, b * nq + i))
    kspec = pl.BlockSpec((None, seq, hd), lambda b, k, i: (k, b, 0))
    vspec = pl.BlockSpec((seq // cw, hd, cw), lambda b, k, i: (b, k, 0))
    out = jax.ShapeDtypeStruct((d, rows), BF16)
    return pl.pallas_call(
        functools.partial(_sel_win_prompt_kernel, tq=tq, tk=tk),
        grid=(batch, NSA_KV_HEADS, nq),
        in_specs=[qspec, kspec, vspec, kspec, vspec,
                  pl.BlockSpec((None, nbp, tq), lambda b, k, i: (k, 0, b * nq + i))],
        out_specs=[qspec, qspec],
        out_shape=[out, out],
        compiler_params=_cparams("parallel", "parallel", "parallel"),
        name="nsa_sel_win_prompt",
    )(qrtt, ks, vst, kw, vwt, sel)


def _rows_by_head(q_ref):
    return jnp.concatenate([q_ref[h] for h in range(q_ref.shape[0])], axis=0)


def _cmp_sel_sample_kernel(q_ref, kc_ref, vc_ref, oc_ref, sel_ref, *, ts, past_len):
    nc = kc_ref.shape[1]
    nbp = sel_ref.shape[2]
    nb = -(-(past_len + ts) // NSA_BLOCK)
    gt = NSA_GROUP * ts
    q = _rows_by_head(q_ref)
    pos_g = past_len + lax.broadcasted_iota(jnp.int32, (gt, nc), 0) % ts
    vis = ((lax.broadcasted_iota(jnp.int32, (gt, nc), 1) + 1) * NSA_BLOCK - 1) <= pos_g
    pos = past_len + lax.broadcasted_iota(jnp.int32, (ts, nbp), 0)
    jblk = lax.broadcasted_iota(jnp.int32, (ts, nbp), 1)
    cur = pos // NSA_BLOCK
    for k in range(NSA_KV_HEADS):
        s = jnp.where(vis, _nt_dot(q[k * gt:(k + 1) * gt], kc_ref[k]), NEG_INF)
        m = jnp.max(s, axis=-1, keepdims=True)
        e = jnp.where(vis, jnp.exp(s - m), 0.0)
        p = e / jnp.maximum(jnp.sum(e, axis=-1, keepdims=True), 1e-30)
        o = _dot(p.astype(BF16), vc_ref[k]).astype(BF16)
        imp = p[0:ts]
        for g in range(NSA_GROUP):
            oc_ref[k * NSA_GROUP + g] = o[g * ts:(g + 1) * ts]
            if g:
                imp = imp + p[g * ts:(g + 1) * ts]
        if nbp > nc:
            imp = jnp.concatenate([imp, jnp.zeros((ts, nbp - nc), F32)], axis=1)
        score = _block_scores(imp, jblk, cur, nb)
        rank = jnp.zeros((ts, nbp), F32)
        for r in range(nb):
            col = score[:, r:r + 1]
            ahead = (col > score) | ((col == score) & (r < jblk))
            rank = rank + jnp.where(ahead, 1.0, 0.0)
        sel_ref[k] = jnp.where((rank < NSA_TOPN) & (score > 0.5 * NEG_INF), 1.0, 0.0)


def _cmp_sel_sample(q_pl, kc, vc, past_len):
    db, _, ts, hd = q_pl.shape
    nb = -(-(past_len + ts) // NSA_BLOCK)
    nbp = -(-nb // LANES) * LANES
    seqspec = lambda a: pl.BlockSpec((None,) + a.shape[1:], lambda s: (s, 0, 0, 0))
    return pl.pallas_call(
        functools.partial(_cmp_sel_sample_kernel, ts=ts, past_len=past_len),
        grid=(db,),
        in_specs=[seqspec(q_pl), seqspec(kc), seqspec(vc)],
        out_specs=[pl.BlockSpec((None, NSA_HEADS, ts, hd), lambda s: (s, 0, 0, 0)),
                   pl.BlockSpec((None, NSA_KV_HEADS, ts, nbp), lambda s: (s, 0, 0, 0))],
        out_shape=[jax.ShapeDtypeStruct((db, NSA_HEADS, ts, hd), BF16),
                   jax.ShapeDtypeStruct((db, NSA_KV_HEADS, ts, nbp), F32)],
        compiler_params=_cparams("parallel"),
        name="nsa_cmp_select_sample",
    )(q_pl, kc, vc)


def _new_token_kv(kvn, k, ts, n_rows):
    hd = NSA_HEAD_DIM
    kw = NSA_KV_HEADS * hd
    pad = jnp.zeros((n_rows - ts, hd), F32)
    kn = jnp.concatenate([kvn[:, k * hd:(k + 1) * hd], pad], axis=0).astype(BF16)
    vn = jnp.concatenate([kvn[:, kw + k * hd:kw + (k + 1) * hd], pad], axis=0).astype(BF16)
    return kn, vn


def _softmax_step(s, mask, m, l):
    s = jnp.where(mask, s, NEG_INF)
    m_new = jnp.maximum(m, jnp.max(s, axis=-1, keepdims=True))
    alpha = jnp.exp(m - m_new)
    p = jnp.where(mask, jnp.exp(s - m_new), 0.0)
    return m_new, alpha, alpha * l + jnp.sum(p, axis=-1, keepdims=True), p.astype(BF16)


def _sel_sample_kernel(pt_ref, q_ref, sel_ref, *rest, n_pg, ts, past_len):
    page_refs = rest[:n_pg]
    kvn_ref, o_ref, flag_sc, m_sc, l_sc, acc_sc = rest[n_pg:]
    j = pl.program_id(1)
    rows = NSA_HEADS * ts
    gt = NSA_GROUP * ts
    hd, kvh = NSA_HEAD_DIM, NSA_KV_HEADS
    page = page_refs[0].shape[1]
    nbp = sel_ref.shape[2]

    @pl.when(j == 0)
    def _():
        flag_sc[...] = jnp.concatenate([sel_ref[h // NSA_GROUP] for h in range(NSA_HEADS)], axis=0).astype(BF16)
        m_sc[...] = jnp.full(m_sc.shape, NEG_INF, F32)
        l_sc[...] = jnp.zeros(l_sc.shape, F32)
        acc_sc[...] = jnp.zeros(acc_sc.shape, F32)

    q = _rows_by_head(q_ref)
    flags = flag_sc[...]
    def chosen_keys(first_block, n_keys):
        blk_row = lax.broadcasted_iota(jnp.int32, (nbp, n_keys), 0)
        key_blk = lax.broadcasted_iota(jnp.int32, (nbp, n_keys), 1) // NSA_BLOCK
        expand = jnp.where(blk_row == key_blk + first_block, 1.0, 0.0).astype(BF16)
        return _dot(flags, expand) > 0.5

    s = jnp.concatenate(
        [jnp.concatenate([_dot(q[k * gt:(k + 1) * gt], ref[k * hd:(k + 1) * hd, :].astype(BF16))
                          for k in range(kvh)], axis=0) for ref in page_refs], axis=1)
    mask = chosen_keys(j * n_pg * (page // NSA_BLOCK), n_pg * page)
    m, alpha, l, p = _softmax_step(s, mask, m_sc[...], l_sc[...])
    acc = alpha * acc_sc[...]
    for u, ref in enumerate(page_refs):
        acc = acc + jnp.concatenate(
            [_nt_dot(p[k * gt:(k + 1) * gt, u * page:(u + 1) * page], ref[(kvh + k) * hd:(kvh + k + 1) * hd, :].astype(BF16))
             for k in range(kvh)], axis=0)
    m_sc[...], l_sc[...], acc_sc[...] = m, l, acc

    @pl.when(j == pl.num_programs(1) - 1)
    def _():
        kvn = kvn_ref[...]
        new = [_new_token_kv(kvn, k, ts, page) for k in range(kvh)]
        qi = lax.broadcasted_iota(jnp.int32, (rows, page), 0) % ts
        col = lax.broadcasted_iota(jnp.int32, (rows, page), 1)
        s = jnp.concatenate([_nt_dot(q[k * gt:(k + 1) * gt], new[k][0]) for k in range(kvh)], axis=0)
        mask = chosen_keys(past_len // NSA_BLOCK, page) & (col <= qi)
        mf, alpha, lf, p = _softmax_step(s, mask, m_sc[...], l_sc[...])
        pv = jnp.concatenate([_dot(p[k * gt:(k + 1) * gt], new[k][1]) for k in range(kvh)], axis=0)
        o = ((alpha * acc_sc[...] + pv) * (1.0 / jnp.maximum(lf, 1e-30))).astype(BF16)
        for h in range(NSA_HEADS):
            o_ref[h] = o[h * ts:(h + 1) * ts]


def _sel_sample(q_rt, sel, cache, layer, page_table, kv_new, past_len):
    db, _, ts, hd = q_rt.shape
    n_pages = page_table.shape[1]
    page = cache.shape[2]
    kw = NSA_KV_HEADS * hd
    nbp = sel.shape[3]
    cache_t = _feature_major_pages(cache)
    n_pg = 16 if n_pages % 16 == 0 else 1
    rows = NSA_HEADS * ts

    def page_spec(u):
        return pl.BlockSpec((None, None, 2 * kw, page), lambda s, j, pt: (layer, pt[s, j * n_pg + u], 0, 0))

    gs = pltpu.PrefetchScalarGridSpec(
        num_scalar_prefetch=1,
        grid=(db, n_pages // n_pg),
        in_specs=[pl.BlockSpec((None, NSA_HEADS, ts, hd), lambda s, j, pt: (s, 0, 0, 0)),
                  pl.BlockSpec((None, NSA_KV_HEADS, ts, nbp), lambda s, j, pt: (s, 0, 0, 0))]
                 + [page_spec(u) for u in range(n_pg)]
                 + [pl.BlockSpec((None, ts, 2 * kw), lambda s, j, pt: (s, 0, 0))],
        out_specs=pl.BlockSpec((None, NSA_HEADS, ts, hd), lambda s, j, pt: (s, 0, 0, 0)),
        scratch_shapes=[pltpu.VMEM((rows, nbp), BF16), pltpu.VMEM((rows, 1), F32),
                        pltpu.VMEM((rows, 1), F32), pltpu.VMEM((rows, hd), F32)],
    )
    return pl.pallas_call(
        functools.partial(_sel_sample_kernel, n_pg=n_pg, ts=ts, past_len=past_len),
        grid_spec=gs,
        out_shape=jax.ShapeDtypeStruct((db, NSA_HEADS, ts, hd), BF16),
        compiler_params=_cparams("parallel", "arbitrary"),
        name="nsa_sel_sample",
    )(page_table, q_rt, sel, *([cache_t] * n_pg), kv_new)


def _win_sample_kernel(q_ref, win_ref, kvn_ref, o_ref, *, ts, past_len):
    rows = NSA_HEADS * ts
    gt = NSA_GROUP * ts
    hd, kvh = NSA_HEAD_DIM, NSA_KV_HEADS
    kw = kvh * hd
    w_buf = win_ref.shape[0]
    q = _rows_by_head(q_ref)
    win = win_ref[...].astype(BF16)
    qpos = past_len + lax.broadcasted_iota(jnp.int32, (rows, w_buf), 0) % ts
    kpos = past_len - w_buf + lax.broadcasted_iota(jnp.int32, (rows, w_buf), 1)
    dist = qpos - kpos
    mask = (dist >= 0) & (dist < NSA_WINDOW) & (kpos >= 0)
    m = jnp.full((rows, 1), NEG_INF, F32)
    l = jnp.zeros((rows, 1), F32)
    s = jnp.concatenate([_nt_dot(q[k * gt:(k + 1) * gt], win[:, k * hd:(k + 1) * hd]) for k in range(kvh)], axis=0)
    m, alpha, l, p = _softmax_step(s, mask, m, l)
    acc = jnp.concatenate([_dot(p[k * gt:(k + 1) * gt], win[:, kw + k * hd:kw + (k + 1) * hd])
                           for k in range(kvh)], axis=0)
    new = [_new_token_kv(kvn_ref[...], k, ts, LANES) for k in range(kvh)]
    qi = lax.broadcasted_iota(jnp.int32, (rows, LANES), 0) % ts
    col = lax.broadcasted_iota(jnp.int32, (rows, LANES), 1)
    s = jnp.concatenate([_nt_dot(q[k * gt:(k + 1) * gt], new[k][0]) for k in range(kvh)], axis=0)
    m, alpha, l, p = _softmax_step(s, col <= qi, m, l)
    pv = jnp.concatenate([_dot(p[k * gt:(k + 1) * gt], new[k][1]) for k in range(kvh)], axis=0)
    o = ((alpha * acc + pv) * (1.0 / jnp.maximum(l, 1e-30))).astype(BF16)
    for h in range(NSA_HEADS):
        o_ref[h] = o[h * ts:(h + 1) * ts]


def _win_sample(q_rt, win_state, layer, kv_new, past_len):
    db, _, ts, hd = q_rt.shape
    kw = NSA_KV_HEADS * hd
    w_buf = win_state.shape[2]
    win2 = win_state.reshape(win_state.shape[0], db, w_buf, 2 * kw)
    return pl.pallas_call(
        functools.partial(_win_sample_kernel, ts=ts, past_len=past_len),
        grid=(db,),
        in_specs=[pl.BlockSpec((None, NSA_HEADS, ts, hd), lambda s: (s, 0, 0, 0)),
                  pl.BlockSpec((None, None, w_buf, 2 * kw), lambda s: (layer, s, 0, 0)),
                  pl.BlockSpec((None, ts, 2 * kw), lambda s: (s, 0, 0))],
        out_specs=pl.BlockSpec((None, NSA_HEADS, ts, hd), lambda s: (s, 0, 0, 0)),
        out_shape=jax.ShapeDtypeStruct((db, NSA_HEADS, ts, hd), BF16),
        compiler_params=_cparams("parallel"),
        name="nsa_win_sample",
    )(q_rt, win2, kv_new)


def _nsa_out_kernel(h_ref, oc_ref, os_ref, ow_ref, gate_ref, w_ref, out_ref, o_sc):
    hd = NSA_HEAD_DIM
    for h in range(NSA_HEADS):
        rows = slice(h * hd, (h + 1) * hd)
        o = (gate_ref[3 * h:3 * h + 1, :] * oc_ref[rows, :].astype(F32)
             + gate_ref[3 * h + 1:3 * h + 2, :] * os_ref[rows, :].astype(F32)
             + gate_ref[3 * h + 2:3 * h + 3, :] * ow_ref[rows, :].astype(F32))
        o_sc[rows, :] = o.astype(BF16)
    out_ref[...] = h_ref[...] + _tn_dot(o_sc[...], w_ref[...])


def _nsa_out(h, oct_, ost, owt, gates_t, w_out_bf):
    rows, d = h.shape
    tm = min(ROW_TILE, rows)
    col = lambda i: (0, i)
    ot = pl.BlockSpec((d, tm), col)
    return pl.pallas_call(
        _nsa_out_kernel,
        grid=(rows // tm,),
        in_specs=[pl.BlockSpec((tm, d), lambda i: (i, 0)), ot, ot, ot, pl.BlockSpec((LANES, tm), col),
                  pl.BlockSpec(w_out_bf.shape, lambda i: (0, 0))],
        out_specs=pl.BlockSpec((tm, d), lambda i: (i, 0)),
        out_shape=jax.ShapeDtypeStruct((rows, d), F32),
        scratch_shapes=[pltpu.VMEM((d, tm), BF16)],
        compiler_params=_cparams("parallel"),
        name="nsa_out",
    )(h, oct_, ost, owt, gates_t, w_out_bf)


def _top_desc(s, n):
    vals = []
    cur = s
    for _ in range(n):
        m = jnp.max(cur, axis=0, keepdims=True)
        vals.append(m)
        cur = jnp.where(cur == m, MASKED, cur)
    return jnp.concatenate(vals, axis=0)


def _oddeven_merge_sort_pairs(n):
    def merge(lo, hi, r):
        step = r * 2
        if step < hi - lo:
            yield from merge(lo, hi, step)
            yield from merge(lo + r, hi, step)
            yield from [(i, i + r) for i in range(lo + r, hi - r, step)]
        else:
            yield (lo, lo + r)

    def sort(lo, hi):
        if hi - lo >= 1:
            mid = lo + (hi - lo) // 2
            yield from sort(lo, mid)
            yield from sort(mid + 1, hi)
            yield from merge(lo, hi, 1)

    return list(sort(0, n - 1))


def _top_sorted(s, n):
    x = [s[v * SUBLANES:(v + 1) * SUBLANES, :] for v in range(n)]

    def exchange(i, j):
        x[i], x[j] = jnp.maximum(x[i], x[j]), jnp.minimum(x[i], x[j])

    for i, j in _oddeven_merge_sort_pairs(n):
        exchange(i, j)
    shift = SUBLANES // 2
    while shift:
        other = [pltpu.roll(v, shift, 0) for v in x]
        x = [jnp.maximum(x[v], other[n - 1 - v]) for v in range(n)]
        dist = n // 2
        while dist:
            for i in range(n):
                if not i & dist:
                    exchange(i, i + dist)
            dist //= 2
        shift //= 2
    return jnp.concatenate([v[0:1, :] for v in x], axis=0)


def _peer_score_kernel(h_ref, g_ref, wq_ref, sk_ref, xt_ref, th_ref, f1_ref, e2_ref):
    half = sk_ref.shape[2]
    xn = _rmsnorm_rows(h_ref[...], g_ref[...])
    xt_ref[...] = xn.T.astype(BF16)
    q = _dot(xn.astype(BF16), wq_ref[...])
    sk1, sk2 = sk_ref[0], sk_ref[1]
    kk = PEER_TOPK
    for h in range(PEER_HEADS):
        q1 = q[:, (2 * h) * half:(2 * h + 1) * half].astype(BF16)
        q2 = q[:, (2 * h + 1) * half:(2 * h + 2) * half].astype(BF16)
        s1 = _nt_dot(sk1, q1)
        s2 = _nt_dot(sk2, q2)
        top_of = _top_sorted if s1.shape[0] == kk * SUBLANES else _top_desc
        t1 = top_of(s1, kk)
        t2 = top_of(s2, kk)
        cand = [t1[0:1] + t2]
        for i in range(1, kk // 2):
            cand.append(t1[i:i + 1] + t2[0:kk // 2])
        cand.append(t1[kk // 2:] + t2[0:1])
        n_cand = sum(c.shape[0] for c in cand)
        if top_of is _top_sorted and n_cand <= kk * SUBLANES:
            cand.append(jnp.full((kk * SUBLANES - n_cand, s1.shape[1]), MASKED, F32))
        top = top_of(jnp.concatenate(cand, axis=0), kk)
        tau = top[kk - 1:kk]
        z = jnp.sum(jnp.exp(top - top[0:1]), axis=0, keepdims=True)
        thr = jnp.full(s1.shape, NO_KEEP, F32)
        for j in range(kk):
            t2j = t2[j:j + 1]
            thr = jnp.where((s1 + t2j) >= tau, t2j, thr)
        m2 = t2[0:1]
        th = jnp.exp(jnp.minimum(thr - m2, 1.0))
        f1 = jnp.exp(s1 - t1[0:1]) * (0.5 / z)
        e2 = jnp.exp(s2 - m2)
        for tc in range(s1.shape[1] // LANES):
            lanes = slice(tc * LANES, (tc + 1) * LANES)
            th_ref[h, tc] = th[:, lanes]
            f1_ref[h, tc] = f1[:, lanes]
            e2_ref[h, tc] = e2[:, lanes]


def _peer_scores(h, g, wq_bf, sk_bf, tt):
    rows, d = h.shape
    n_keys = sk_bf.shape[1]
    nt = rows // tt
    tab = jax.ShapeDtypeStruct((PEER_HEADS, rows // LANES, n_keys, LANES), F32)
    tspec = pl.BlockSpec((PEER_HEADS, tt // LANES, n_keys, LANES), lambda i: (0, i, 0, 0))
    return pl.pallas_call(
        _peer_score_kernel,
        grid=(nt,),
        in_specs=[pl.BlockSpec((tt, d), lambda i: (i, 0)), pl.BlockSpec((1, d), lambda i: (0, 0)),
                  pl.BlockSpec(wq_bf.shape, lambda i: (0, 0)), pl.BlockSpec(sk_bf.shape, lambda i: (0, 0, 0))],
        out_specs=[pl.BlockSpec((d, tt), lambda i: (0, i)), tspec, tspec, tspec],
        out_shape=[jax.ShapeDtypeStruct((d, rows), BF16), tab, tab, tab],
        compiler_params=_cparams("parallel"),
        name="peer_scores",
    )(h, g.reshape(1, d), wq_bf, sk_bf)


PEER_A_PER_STEP = 16
PEER_A_PER_SUB = 8
PEER_A_PER_ACC = 2


def _peer_expert_kernel(xt_ref, u_ref, vt_ref, th_ref, f1_ref, e2_ref, h_ref, out_ref, acc_sc):
    c = pl.program_id(1)
    n_keys = e2_ref.shape[2]
    tt = xt_ref.shape[1]
    a_per_step = th_ref.shape[2]
    sub = PEER_A_PER_SUB * n_keys

    @pl.when(c == 0)
    def _():
        acc_sc[...] = jnp.zeros(acc_sc.shape, F32)

    xt = xt_ref[...]
    total = None
    for sc in range(a_per_step // PEER_A_PER_SUB):
        rows = slice(sc * sub, (sc + 1) * sub)
        ux = _dot(u_ref[rows, :], xt)
        act = ux * (1.0 + lax.erf(ux * (2.0 ** -0.5)))
        g_parts = []
        for tc in range(tt // LANES):
            col = []
            for a0 in range(0, PEER_A_PER_SUB, PEER_A_PER_ACC):
                w = [jnp.zeros((n_keys, LANES), F32) for _ in range(PEER_A_PER_ACC)]
                for h in range(PEER_HEADS):
                    e2 = e2_ref[h, tc]
                    for u in range(PEER_A_PER_ACC):
                        a = sc * PEER_A_PER_SUB + a0 + u
                        keep = e2 >= th_ref[h, tc, a:a + 1, :]
                        w[u] = w[u] + jnp.where(keep, f1_ref[h, tc, a:a + 1, :] * e2, 0.0)
                col += w
            g_parts.append(jnp.concatenate(col, axis=0) * act[:, tc * LANES:(tc + 1) * LANES])
        g = jnp.concatenate(g_parts, axis=1).astype(BF16)
        part = _dot(vt_ref[:, rows], g)
        total = part if total is None else total + part
    acc_sc[...] += total

    @pl.when(c == pl.num_programs(1) - 1)
    def _():
        out_ref[...] = h_ref[...] + acc_sc[...].T


def _peer_experts(h, xt, u_bf, vt_bf, layer, th, f1, e2, tt):
    rows, d = h.shape
    n_exp = u_bf.shape[1]
    n_keys = e2.shape[2]
    ec = PEER_A_PER_STEP * n_keys
    e2spec = pl.BlockSpec((PEER_HEADS, tt // LANES, n_keys, LANES), lambda i, c: (0, i, 0, 0))
    aspec = pl.BlockSpec((PEER_HEADS, tt // LANES, PEER_A_PER_STEP, LANES), lambda i, c: (0, i, c, 0))
    return pl.pallas_call(
        _peer_expert_kernel,
        grid=(rows // tt, n_exp // ec),
        in_specs=[pl.BlockSpec((d, tt), lambda i, c: (0, i)),
                  pl.BlockSpec((None, ec, d), lambda i, c: (layer, c, 0)),
                  pl.BlockSpec((None, d, ec), lambda i, c: (layer, 0, c)),
                  aspec, aspec, e2spec,
                  pl.BlockSpec((tt, d), lambda i, c: (i, 0))],
        out_specs=pl.BlockSpec((tt, d), lambda i, c: (i, 0)),
        out_shape=jax.ShapeDtypeStruct((rows, d), F32),
        scratch_shapes=[pltpu.VMEM((d, tt), F32)],
        compiler_params=_cparams("parallel", "arbitrary"),
        name="peer_experts",
    )(xt, u_bf, vt_bf, th, f1, e2, h)


def _peer(h, g, wq_bf, sk_bf, u_bf, vt_bf, layer):
    rows = h.shape[0]
    tt = min(ROW_TILE, rows)
    xt, th, f1, e2 = _peer_scores(h, g, wq_bf, sk_bf, tt)
    return _peer_experts(h, xt, u_bf, vt_bf, layer, th, f1, e2, tt)


def _final_norm_kernel(h_ref, g_ref, o_ref):
    o_ref[...] = _rmsnorm_rows(h_ref[...], g_ref[...])


def _final_norm(h, g):
    rows, d = h.shape
    tm = min(ROW_TILE, rows)
    return pl.pallas_call(
        _final_norm_kernel,
        grid=(rows // tm,),
        in_specs=[pl.BlockSpec((tm, d), lambda i: (i, 0)), pl.BlockSpec((1, d), lambda i: (0, 0))],
        out_specs=pl.BlockSpec((tm, d), lambda i: (i, 0)),
        out_shape=jax.ShapeDtypeStruct((rows, d), F32),
        compiler_params=_cparams("parallel"),
        name="final_norm",
    )(h, g.reshape(1, d))


def _heads_from_t(xt, db, ts, hd):
    return xt.reshape(-1, hd, db, ts).transpose(2, 0, 3, 1)


def _heads_to_t(x):
    db, heads, ts, hd = x.shape
    return x.transpose(1, 3, 0, 2).reshape(heads * hd, db * ts)


def kernel(x_prompt, x_sample, cache_diff_kv, cache_nsa_cmp_kv, cache_nsa_sel_kv, state_nsa_win_kv, page_table, norm_mix_g, diff_w_in, diff_lambda, diff_subln_g, diff_w_out, nsa_w_in, nsa_cmp_pos, nsa_cmp_w1, nsa_cmp_b1, nsa_cmp_w2, nsa_cmp_b2, nsa_w_out, norm_ffn_g, peer_wq, peer_subkeys, peer_u, peer_v, final_norm_g):
    batch, seq, d = x_prompt.shape
    db, ts, _ = x_sample.shape
    depth = norm_mix_g.shape[0]
    past_len = page_table.shape[1] * cache_diff_kv.shape[2]
    assert past_len % NSA_BLOCK == 0 and ts < NSA_BLOCK and seq % LANES == 0 and (db * ts) % LANES == 0
    assert state_nsa_win_kv.shape[2] == min(NSA_WINDOW, past_len)

    pos_p = jnp.arange(seq)
    pos_s = jnp.tile(past_len + jnp.arange(ts), db)
    tab_p = _rope_tables(pos_p, DA_HEAD_DIM)
    tab_s = _rope_tables(pos_s, DA_HEAD_DIM)

    hp = x_prompt.reshape(batch * seq, d)
    hs = x_sample.reshape(db * ts, d)
    outs = {k: [] for k in ("diff_p", "diff_s", "cmp_p", "cmp_s", "sel_p", "sel_s", "win_p", "win_s")}
    u_bf = peer_u.astype(BF16)
    vt_bf = peer_v.transpose(0, 2, 1).astype(BF16)

    for i in range(depth):
        g_mix = norm_mix_g[i]
        if i % 2 == 0:
            a = i // 2
            lam_init = 0.8 - 0.6 * math.exp(-0.3 * i)
            w_in = diff_w_in[a].astype(BF16)
            w_out = diff_w_out[a].astype(BF16)
            qt, kvp, kb, vtc = _diff_proj(hp, g_mix, w_in, tab_p, seq)
            ot = _diff_attn_prompt(qt, kb, vtc, diff_lambda[a], diff_subln_g[a], batch, seq, lam_init)
            hp = _outproj(hp, ot, w_out)
            qt_s, kvs, _, _ = _diff_proj(hs, g_mix, w_in, tab_s, db * ts)
            os_ = _diff_attn_sample(qt_s.T.reshape(db, ts, d), cache_diff_kv, a, page_table,
                                    kvs.reshape(db, ts, 2 * d), diff_lambda[a], diff_subln_g[a], lam_init)
            hs = _outproj(hs, os_.reshape(db * ts, d).T, w_out)
            outs["diff_p"].append(kvp.reshape(batch, seq, 2, DA_HEADS, 2 * DA_HEAD_DIM))
            outs["diff_s"].append(kvs.reshape(db, ts, 2, DA_HEADS, 2 * DA_HEAD_DIM))
        else:
            b = i // 2
            hd = NSA_HEAD_DIM
            n_main = d + 3 * 2 * NSA_KV_HEADS * hd
            w_main = nsa_w_in[b][:, :n_main].astype(BF16)
            n_gate = nsa_w_in.shape[2] - n_main
            w_gate = jnp.pad(nsa_w_in[b][:, n_main:], ((0, 0), (0, LANES - n_gate))).astype(BF16)
            w_out = nsa_w_out[b].astype(BF16)
            cw_tok, cw_feat = _compress_weights(nsa_cmp_pos[b], nsa_cmp_w1[b], nsa_cmp_b1[b],
                                                nsa_cmp_w2[b], nsa_cmp_b2[b])
            kv_shape = (2, NSA_KV_HEADS, hd)
            (qplt, qrtt, kvc, kvs_, kvw, ks, vst, kw, vwt, gates_t) = _nsa_proj(hp, g_mix, w_main, w_gate, tab_p, seq)
            kc, vct = _compress_prompt(kvc, cw_tok, batch, seq)
            oct_, sel = _cmp_sel_prompt(qplt, kc, vct, batch, seq)
            ost, owt = _sel_win_prompt(qrtt, ks, vst, kw, vwt, sel, batch, seq)
            hp = _nsa_out(hp, oct_, ost, owt, gates_t, w_out)
            outs["cmp_p"].append(kvc.reshape((batch, seq) + kv_shape))
            outs["sel_p"].append(kvs_.reshape((batch, seq) + kv_shape))
            w_keep = min(NSA_WINDOW, seq)
            outs["win_p"].append(kvw.reshape((batch, seq) + kv_shape)[:, seq - w_keep:])
            (qplt, qrtt, kvc, kvs_, kvw, _, _, _, _, gates_t) = _nsa_proj(hs, g_mix, w_main, w_gate, tab_s, db * ts)
            kc, vc = _compress_sample(cache_nsa_cmp_kv, b, page_table, cw_feat)
            o_c, sel = _cmp_sel_sample(_heads_from_t(qplt, db, ts, hd), kc, vc, past_len)
            qrt_sm = _heads_from_t(qrtt, db, ts, hd)
            o_s = _sel_sample(qrt_sm, sel, cache_nsa_sel_kv, b, page_table, kvs_.reshape(db, ts, -1), past_len)
            o_w = _win_sample(qrt_sm, state_nsa_win_kv, b, kvw.reshape(db, ts, -1), past_len)
            hs = _nsa_out(hs, _heads_to_t(o_c), _heads_to_t(o_s), _heads_to_t(o_w), gates_t, w_out)
            outs["cmp_s"].append(kvc.reshape((db, ts) + kv_shape))
            outs["sel_s"].append(kvs_.reshape((db, ts) + kv_shape))
            win_all = jnp.concatenate([state_nsa_win_kv[b], kvw.reshape((db, ts) + kv_shape)], axis=1)
            outs["win_s"].append(win_all[:, win_all.shape[1] - state_nsa_win_kv.shape[2]:])
        g_ffn = norm_ffn_g[i]
        wq = peer_wq[i].astype(BF16)
        sk = peer_subkeys[i].astype(BF16)
        hp = _peer(hp, g_ffn, wq, sk, u_bf, vt_bf, i)
        hs = _peer(hs, g_ffn, wq, sk, u_bf, vt_bf, i)

    y_prompt = _final_norm(hp, final_norm_g).reshape(batch, seq, d)
    y_sample = _final_norm(hs, final_norm_g).reshape(db, ts, d)
    stack = lambda k: jnp.stack(outs[k])
    return (y_prompt, y_sample, stack("diff_p"), stack("diff_s"), stack("cmp_p"), stack("cmp_s"),
            stack("sel_p"), stack("sel_s"), stack("win_p"), stack("win_s"))
```

```python
import functools
import math

import jax
import jax.numpy as jnp
from jax import lax
from jax.experimental import pallas as pl
from jax.experimental.pallas import tpu as pltpu

F32 = jnp.float32
BF16 = jnp.bfloat16

NORM_EPS = 1e-6
ROPE_THETA = 500000.0
ROPE_FRACTION = 4
NEG_INF = -1e30
MASKED = -3.0e38
NO_KEEP = 3.0e38

DA_HEADS = 8
DA_HEAD_DIM = 64
NSA_HEADS = 16
NSA_KV_HEADS = 4
NSA_GROUP = NSA_HEADS // NSA_KV_HEADS
NSA_HEAD_DIM = 64
NSA_BLOCK = 64
NSA_TOPN = 16
NSA_WINDOW = 512
NSA_FORCE_SCORE = 1e4
NSA_V_CHUNK = 256
PEER_HEADS = 8
PEER_TOPK = 16

LANES = 128
SUBLANES = 8
VMEM_LIMIT_BYTES = 56 * 1024 * 1024
ROW_TILE = 512


def _cparams(*sem):
    return pltpu.CompilerParams(dimension_semantics=tuple(sem), vmem_limit_bytes=VMEM_LIMIT_BYTES)


def _dot(a, b):
    return jnp.dot(a, b, preferred_element_type=F32)


def _nt_dot(a, b):
    return lax.dot_general(a, b, (((1,), (1,)), ((), ())), preferred_element_type=F32)


def _tn_dot(a, b):
    return lax.dot_general(a, b, (((0,), (0,)), ((), ())), preferred_element_type=F32)


def _rmsnorm_rows(x, g):
    ms = jnp.mean(x * x, axis=-1, keepdims=True)
    return x * lax.rsqrt(ms + NORM_EPS) * g


def _gelu(x):
    return 0.5 * x * (1.0 + lax.erf(x * (2.0 ** -0.5)))


def _rope_tables(pos, head_dim):
    d_rot = head_dim // ROPE_FRACTION
    half = d_rot // 2
    inv_freq = ROPE_THETA ** (-jnp.arange(half, dtype=F32) / half)
    ang = pos.astype(F32)[:, None] * inv_freq[None, :]
    cos, sin = jnp.cos(ang), jnp.sin(ang)
    n = pos.shape[0]
    zeros = lambda w: jnp.zeros((n, w), F32)
    c = jnp.concatenate([cos, cos, jnp.ones((n, head_dim - d_rot), F32)], axis=1)
    sa = jnp.concatenate([-sin, zeros(head_dim - half)], axis=1)
    sb = jnp.concatenate([zeros(half), sin, zeros(head_dim - d_rot)], axis=1)
    rep = LANES // head_dim
    return jnp.tile(c, (1, rep)), jnp.tile(sa, (1, rep)), jnp.tile(sb, (1, rep)), half


def _rope_cols(y, c, sa, sb, half):
    outs = []
    for j in range(y.shape[1] // LANES):
        ch = y[:, j * LANES:(j + 1) * LANES]
        outs.append(ch * c + pltpu.roll(ch, LANES - half, 1) * sa + pltpu.roll(ch, half, 1) * sb)
    return outs[0] if len(outs) == 1 else jnp.concatenate(outs, axis=1)


def _store_lane_chunks(ref, xt):
    width = ref.shape[2]
    for c in range(xt.shape[1] // width):
        ref[c] = xt[:, c * width:(c + 1) * width]


def _diff_proj_kernel(x_ref, g_ref, w_ref, c_ref, sa_ref, sb_ref,
                      qt_ref, kv_ref, kb_ref, vt_ref, *, half, scale):
    d = x_ref.shape[1]
    xn = _rmsnorm_rows(x_ref[...], g_ref[...]).astype(BF16)
    y = _dot(xn, w_ref[...])
    c, sa, sb = c_ref[...], sa_ref[...], sb_ref[...]
    q = _rope_cols(y[:, :d], c, sa, sb, half) * scale
    k = _rope_cols(y[:, d:2 * d], c, sa, sb, half)
    v = y[:, 2 * d:]
    qt_ref[...] = q.T.astype(BF16)
    kv_ref[:, :d] = k
    kv_ref[:, d:] = v
    kb_ref[...] = k.astype(BF16)
    _store_lane_chunks(vt_ref, v.T.astype(BF16))


def _key_chunk(rows):
    return min(512, rows)


def _diff_proj(x, g, w_bf, tables, period_rows):
    rows, d = x.shape
    c, sa, sb, half = tables
    tm = min(ROW_TILE, rows)
    nper = period_rows // tm
    cw = _key_chunk(tm)
    row = lambda i: (i, 0)
    tab = pl.BlockSpec((tm, LANES), lambda i: (i % nper, 0))
    return pl.pallas_call(
        functools.partial(_diff_proj_kernel, half=half, scale=DA_HEAD_DIM ** -0.5),
        grid=(rows // tm,),
        in_specs=[pl.BlockSpec((tm, d), row), pl.BlockSpec((1, d), lambda i: (0, 0)),
                  pl.BlockSpec((d, 3 * d), lambda i: (0, 0)), tab, tab, tab],
        out_specs=[pl.BlockSpec((d, tm), lambda i: (0, i)), pl.BlockSpec((tm, 2 * d), row),
                   pl.BlockSpec((tm, d), row), pl.BlockSpec((tm // cw, d, cw), lambda i: (i, 0, 0))],
        out_shape=[jax.ShapeDtypeStruct((d, rows), BF16), jax.ShapeDtypeStruct((rows, 2 * d), F32),
                   jax.ShapeDtypeStruct((rows, d), BF16), jax.ShapeDtypeStruct((rows // cw, d, cw), BF16)],
        compiler_params=_cparams("parallel"),
        name="diff_proj",
    )(x, g.reshape(1, d), w_bf, c, sa, sb)


def _diff_lambda(lam_ref, lam_init):
    lv = lam_ref[...]
    a = jnp.sum(lv[0:1, :] * lv[1:2, :], axis=-1, keepdims=True)
    b = jnp.sum(lv[2:3, :] * lv[3:4, :], axis=-1, keepdims=True)
    return jnp.exp(a) - jnp.exp(b) + lam_init


def _subln(o, g, lam_init):
    ms = jnp.mean(o * o, axis=-1, keepdims=True)
    return o * lax.rsqrt(ms + NORM_EPS) * g * (1.0 - lam_init)


def _online_update(s, mask, v, m, l, acc):
    if mask is not None:
        s = jnp.where(mask, s, NEG_INF)
    m_new = jnp.maximum(m, jnp.max(s, axis=-1, keepdims=True))
    alpha = jnp.exp(m - m_new)
    p = jnp.exp(s - m_new)
    if mask is not None:
        p = jnp.where(mask, p, 0.0)
    l_new = alpha * l + jnp.sum(p, axis=-1, keepdims=True)
    acc_new = alpha * acc + _dot(p.astype(BF16), v)
    return m_new, l_new, acc_new


def _online_update_t(st, mask, vt, m, l, acc, some_key_visible=False):
    if mask is not None:
        st = jnp.where(mask, st, NEG_INF)
    m_new = jnp.maximum(m, jnp.max(st, axis=0, keepdims=True))
    alpha = jnp.exp(m - m_new)
    p = jnp.exp(st - m_new)
    if mask is not None and not some_key_visible:
        p = jnp.where(mask, p, 0.0)
    l_new = alpha * l + jnp.sum(p, axis=0, keepdims=True)
    acc_new = alpha * acc + _dot(vt, p.astype(BF16))
    return m_new, l_new, acc_new


def _diff_attn_kernel(qt_ref, k_ref, vt_ref, lam_ref, g_ref, o_ref, *, tq, tk, lam_init):
    i = pl.program_id(2)
    hd2 = qt_ref.shape[0]
    qt = qt_ref[...]
    comp = lax.broadcasted_iota(jnp.int32, (hd2, tq), 0) // (hd2 // 2)
    zero = jnp.zeros_like(qt)
    qq = jnp.concatenate([jnp.where(comp == 0, qt, zero), jnp.where(comp == 1, qt, zero)], axis=1)
    qpos = i * tq + lax.broadcasted_iota(jnp.int32, (tk, 2 * tq), 1) % tq
    krow = lax.broadcasted_iota(jnp.int32, (tk, 2 * tq), 0)

    def step(j, carry, masked):
        start = pl.multiple_of(j * tk, tk)
        st = _dot(k_ref[pl.ds(start, tk), :], qq)
        mask = ((krow + j * tk) <= qpos) if masked else None
        return _online_update_t(st, mask, vt_ref[j], *carry, some_key_visible=True)

    n_full = (i * tq) // tk
    n_kv = ((i + 1) * tq + tk - 1) // tk
    carry = (jnp.full((1, 2 * tq), NEG_INF, F32), jnp.zeros((1, 2 * tq), F32), jnp.zeros((hd2, 2 * tq), F32))
    carry = lax.fori_loop(0, n_full, lambda j, c: step(j, c, False), carry)
    m, l, acc = lax.fori_loop(n_full, n_kv, lambda j, c: step(j, c, True), carry)
    lam = _diff_lambda(lam_ref, lam_init)
    inv = 1.0 / jnp.maximum(l, 1e-30)
    o = acc[:, :tq] * inv[:, :tq] - lam * (acc[:, tq:] * inv[:, tq:])
    ms = jnp.mean(o * o, axis=0, keepdims=True)
    o_ref[...] = (o * lax.rsqrt(ms + NORM_EPS) * g_ref[...] * (1.0 - lam_init)).astype(BF16)


def _diff_attn_prompt(qt, kb, vtc, lam_vec, subln_g, batch, seq, lam_init):
    d, rows = qt.shape
    hd2 = 2 * DA_HEAD_DIM
    tq = min(512, seq)
    tk = vtc.shape[2]
    nq = seq // tq
    return pl.pallas_call(
        functools.partial(_diff_attn_kernel, tq=tq, tk=tk, lam_init=lam_init),
        grid=(batch, DA_HEADS, nq),
        in_specs=[pl.BlockSpec((hd2, tq), lambda b, h, i: (h, b * nq + i)),
                  pl.BlockSpec((seq, hd2), lambda b, h, i: (b, h)),
                  pl.BlockSpec((seq // tk, hd2, tk), lambda b, h, i: (b, h, 0)),
                  pl.BlockSpec(lam_vec.shape, lambda b, h, i: (0, 0)),
                  pl.BlockSpec((hd2, 1), lambda b, h, i: (0, 0))],
        out_specs=pl.BlockSpec((hd2, tq), lambda b, h, i: (h, b * nq + i)),
        out_shape=jax.ShapeDtypeStruct((d, rows), BF16),
        compiler_params=_cparams("parallel", "parallel", "parallel"),
        name="diff_attn_prompt",
    )(qt, kb, vtc, lam_vec, subln_g.reshape(hd2, 1))


def _diff_dec_kernel(pt_ref, q_ref, *rest, n_pg, ts, lam_init):
    page_refs = rest[:n_pg]
    kvn_ref, lam_ref, g_ref, o_ref, qq_sc, m_sc, l_sc, acc_sc = rest[n_pg:]
    j = pl.program_id(1)
    hd2 = 2 * DA_HEAD_DIM
    grp = 2 * ts
    page = page_refs[0].shape[0]
    rows = DA_HEADS * grp

    @pl.when(j == 0)
    def _():
        q = q_ref[...]
        comp = lax.broadcasted_iota(jnp.int32, (ts, hd2), 1) // DA_HEAD_DIM
        for h in range(DA_HEADS):
            qh = q[:, h * hd2:(h + 1) * hd2]
            zero = jnp.zeros_like(qh)
            qq_sc[h * grp:h * grp + ts, :] = jnp.where(comp == 0, qh, zero)
            qq_sc[h * grp + ts:(h + 1) * grp, :] = jnp.where(comp == 1, qh, zero)
        m_sc[...] = jnp.full(m_sc.shape, NEG_INF, F32)
        l_sc[...] = jnp.zeros(l_sc.shape, F32)
        acc_sc[...] = jnp.zeros(acc_sc.shape, F32)

    def attend(k_all, v_all, visible, m, l, acc):
        n_cols = k_all.shape[0]
        s = _nt_dot(qq_sc[...], k_all)
        keep = (lax.broadcasted_iota(jnp.int32, (rows, n_cols), 0) // grp
                == lax.broadcasted_iota(jnp.int32, (rows, n_cols), 1) % DA_HEADS)
        if visible is not None:
            keep = keep & visible
        s = jnp.where(keep, s, NEG_INF)
        m_new = jnp.maximum(m, jnp.max(s, axis=-1, keepdims=True))
        alpha = jnp.exp(m - m_new)
        p = jnp.where(keep, jnp.exp(s - m_new), 0.0)
        l_new = alpha * l + jnp.sum(p, axis=-1, keepdims=True)
        return m_new, l_new, alpha * acc + _dot(p.astype(BF16), v_all)

    def k_and_v(ref, n_tok):
        k = ref[:, 0:DA_HEADS, :].reshape(n_tok * DA_HEADS, hd2).astype(BF16)
        v = ref[:, DA_HEADS:2 * DA_HEADS, :].reshape(n_tok * DA_HEADS, hd2).astype(BF16)
        return k, v

    kv = [k_and_v(ref, page) for ref in page_refs]
    k_all = jnp.concatenate([k for k, _ in kv], axis=0)
    v_all = jnp.concatenate([v for _, v in kv], axis=0)
    m, l, acc = attend(k_all, v_all, None, m_sc[...], l_sc[...], acc_sc[...])
    m_sc[...], l_sc[...], acc_sc[...] = m, l, acc

    @pl.when(j == pl.num_programs(1) - 1)
    def _():
        kn, vn = k_and_v(kvn_ref, ts)
        pad = jnp.zeros((LANES - ts * DA_HEADS, hd2), BF16)
        qi = lax.broadcasted_iota(jnp.int32, (rows, LANES), 0) % ts
        tok = lax.broadcasted_iota(jnp.int32, (rows, LANES), 1) // DA_HEADS
        mf, lf, af = attend(jnp.concatenate([kn, pad], axis=0), jnp.concatenate([vn, pad], axis=0),
                            tok <= qi, m_sc[...], l_sc[...], acc_sc[...])
        lam = _diff_lambda(lam_ref, lam_init)
        af = af * (1.0 / jnp.maximum(lf, 1e-30))
        for h in range(DA_HEADS):
            o = af[h * grp:h * grp + ts] - lam * af[h * grp + ts:(h + 1) * grp]
            o_ref[:, h * hd2:(h + 1) * hd2] = _subln(o, g_ref[...], lam_init).astype(BF16)


def _diff_attn_sample(q, cache, layer, page_table, kv_new, lam_vec, subln_g, lam_init):
    db, ts, d = q.shape
    n_pages = page_table.shape[1]
    page = cache.shape[2]
    hd2 = 2 * DA_HEAD_DIM
    n_pg = 8 if n_pages % 8 == 0 else 1
    n_kvh = 2 * DA_HEADS
    assert ts * DA_HEADS <= LANES
    cache2 = cache.reshape(cache.shape[0], cache.shape[1], page, n_kvh, hd2)
    kv_new = kv_new.reshape(db, ts, n_kvh, hd2)
    rows = 2 * DA_HEADS * ts

    def page_spec(u):
        return pl.BlockSpec((None, None, page, n_kvh, hd2), lambda s, j, pt: (layer, pt[s, j * n_pg + u], 0, 0, 0))

    gs = pltpu.PrefetchScalarGridSpec(
        num_scalar_prefetch=1,
        grid=(db, n_pages // n_pg),
        in_specs=[pl.BlockSpec((None, ts, d), lambda s, j, pt: (s, 0, 0))]
                 + [page_spec(u) for u in range(n_pg)]
                 + [pl.BlockSpec((None, ts, n_kvh, hd2), lambda s, j, pt: (s, 0, 0, 0)),
                    pl.BlockSpec(lam_vec.shape, lambda s, j, pt: (0, 0)),
                    pl.BlockSpec((1, hd2), lambda s, j, pt: (0, 0))],
        out_specs=pl.BlockSpec((None, ts, d), lambda s, j, pt: (s, 0, 0)),
        scratch_shapes=[pltpu.VMEM((rows, hd2), BF16), pltpu.VMEM((rows, 1), F32),
                        pltpu.VMEM((rows, 1), F32), pltpu.VMEM((rows, hd2), F32)],
    )
    return pl.pallas_call(
        functools.partial(_diff_dec_kernel, n_pg=n_pg, ts=ts, lam_init=lam_init),
        grid_spec=gs,
        out_shape=jax.ShapeDtypeStruct((db, ts, d), BF16),
        compiler_params=_cparams("parallel", "arbitrary"),
        name="diff_attn_sample",
    )(page_table, q, *([cache2] * n_pg), kv_new, lam_vec, subln_g.reshape(1, -1))


def _outproj_kernel(h_ref, ot_ref, w_ref, out_ref):
    out_ref[...] = h_ref[...] + _tn_dot(ot_ref[...], w_ref[...])


def _outproj(h, ot, w_bf):
    rows, d = h.shape
    tm = min(ROW_TILE, rows)
    row = lambda i: (i, 0)
    return pl.pallas_call(
        _outproj_kernel,
        grid=(rows // tm,),
        in_specs=[pl.BlockSpec((tm, d), row), pl.BlockSpec((ot.shape[0], tm), lambda i: (0, i)),
                  pl.BlockSpec(w_bf.shape, lambda i: (0, 0))],
        out_specs=pl.BlockSpec((tm, d), row),
        out_shape=jax.ShapeDtypeStruct((rows, d), F32),
        compiler_params=_cparams("parallel"),
        name="outproj",
    )(h, ot, w_bf)


def _nsa_proj_kernel(x_ref, g_ref, w_ref, wg_ref, c_ref, sa_ref, sb_ref,
                     qpl_ref, qrt_ref, kvc_ref, kvs_ref, kvw_ref, ks_ref, vst_ref, kw_ref, vwt_ref, gate_ref,
                     *, half, scale):
    d = x_ref.shape[1]
    hd = NSA_HEAD_DIM
    kw = NSA_KV_HEADS * hd
    xn = _rmsnorm_rows(x_ref[...], g_ref[...]).astype(BF16)
    y = _dot(xn, w_ref[...])
    gl = _dot(xn, wg_ref[...])
    gate_ref[...] = (1.0 / (1.0 + jnp.exp(-gl))).T
    c, sa, sb = c_ref[...], sa_ref[...], sb_ref[...]
    q = y[:, :d]
    qpl_ref[...] = (q * scale).T.astype(BF16)
    qrt_ref[...] = (_rope_cols(q, c, sa, sb, half) * scale).T.astype(BF16)
    kvc_ref[...] = y[:, d:d + 2 * kw]
    off = d + 2 * kw
    for kv_ref, k_ref, vt_ref in ((kvs_ref, ks_ref, vst_ref), (kvw_ref, kw_ref, vwt_ref)):
        k = _rope_cols(y[:, off:off + kw], c, sa, sb, half)
        v = y[:, off + kw:off + 2 * kw]
        kv_ref[:, :kw] = k
        kv_ref[:, kw:] = v
        kb = k.astype(BF16)
        for h in range(NSA_KV_HEADS):
            k_ref[h] = kb[:, h * hd:(h + 1) * hd]
        _store_lane_chunks(vt_ref, v.T.astype(BF16))
        off += 2 * kw


def _nsa_proj(x, g, w_bf, wg_bf, tables, period_rows):
    rows, d = x.shape
    c, sa, sb, half = tables
    tm = min(ROW_TILE, rows)
    nper = period_rows // tm
    hd = NSA_HEAD_DIM
    kw = NSA_KV_HEADS * hd
    row = lambda i: (i, 0)
    col = lambda i: (0, i)
    tab = pl.BlockSpec((tm, LANES), lambda i: (i % nper, 0))
    const = lambda i: (0, 0)
    qt = jax.ShapeDtypeStruct((d, rows), BF16)
    khm = jax.ShapeDtypeStruct((NSA_KV_HEADS, rows, hd), BF16)
    cw = min(NSA_V_CHUNK, tm)
    vtc = jax.ShapeDtypeStruct((rows // cw, kw, cw), BF16)
    kvf = jax.ShapeDtypeStruct((rows, 2 * kw), F32)
    khm_spec = pl.BlockSpec((NSA_KV_HEADS, tm, hd), lambda i: (0, i, 0))
    vtc_spec = pl.BlockSpec((tm // cw, kw, cw), lambda i: (i, 0, 0))
    kv_spec = pl.BlockSpec((tm, 2 * kw), row)
    return pl.pallas_call(
        functools.partial(_nsa_proj_kernel, half=half, scale=hd ** -0.5),
        grid=(rows // tm,),
        in_specs=[pl.BlockSpec((tm, d), row), pl.BlockSpec((1, d), const),
                  pl.BlockSpec(w_bf.shape, const), pl.BlockSpec(wg_bf.shape, const), tab, tab, tab],
        out_specs=[pl.BlockSpec((d, tm), col), pl.BlockSpec((d, tm), col), kv_spec, kv_spec, kv_spec,
                   khm_spec, vtc_spec, khm_spec, vtc_spec, pl.BlockSpec((LANES, tm), col)],
        out_shape=[qt, qt, kvf, kvf, kvf, khm, vtc, khm, vtc, jax.ShapeDtypeStruct((LANES, rows), F32)],
        compiler_params=_cparams("parallel"),
        name="nsa_proj",
    )(x, g.reshape(1, d), w_bf, wg_bf, c, sa, sb)


def _compress_weights(cmp_pos, w1, b1, w2, b2):
    kvh, hd, blk = NSA_KV_HEADS, NSA_HEAD_DIM, NSA_BLOCK
    hid = w1.shape[-1]
    eye = jnp.eye(kvh, dtype=F32)
    eye2 = jnp.eye(2, dtype=F32)
    pos_rep = jnp.broadcast_to(cmp_pos[:, :, None, :], (blk, 2, kvh, hd)).reshape(blk, 2 * kvh * hd)
    w1bd = jnp.einsum('crde,kl->rckdle', w1, eye).reshape(blk, 2, kvh * hd, kvh * hid).astype(BF16)
    b1_rep = jnp.broadcast_to(b1[:, None, :], (2, kvh, hid)).reshape(1, 2 * kvh * hid)
    w2bd = jnp.einsum('aed,ab,kl->akebld', w2, eye2, eye).reshape(2 * kvh * hid, 2 * kvh * hd).astype(BF16)
    b2_rep = jnp.broadcast_to(b2[:, None, :], (2, kvh, hd)).reshape(1, 2 * kvh * hd)
    token_major = (pos_rep, w1bd, b1_rep, w2bd, b2_rep)
    pos_t = jnp.tile(cmp_pos.transpose(1, 2, 0), (1, 1, 2))
    w1_t = jnp.einsum('crde,ab->cdarbe', w1, eye2).reshape(2, hd, 2 * blk, 2 * hid).astype(BF16)
    b1_t = jnp.tile(b1, (1, 2)).reshape(2, 1, 2 * hid)
    w2_t = jnp.einsum('ced,ab->caebd', w2, eye2).reshape(2, 2 * hid, 2 * hd).astype(BF16)
    b2_t = jnp.tile(b2, (1, 2)).reshape(2, 1, 2 * hd)
    feature_major = (pos_t, w1_t, b1_t, w2_t, b2_t)
    return token_major, feature_major


def _compress_prompt_kernel(*refs):
    pos_ref, w1_ref, b1_ref, w2_ref, b2_ref, kc_ref, vct_ref = refs[-7:]
    slabs = refs[:-7]
    n_blk = slabs[0].shape[0] // NSA_BLOCK
    half = w1_ref.shape[2]
    acc = [jnp.zeros((n_blk, half), F32), jnp.zeros((n_blk, half), F32)]
    for r in range(NSA_BLOCK):
        xr = jnp.concatenate([ref[pl.ds(r, n_blk, stride=NSA_BLOCK), :] for ref in slabs], axis=1)
        xr = (xr + pos_ref[r:r + 1, :]).astype(BF16)
        for c in range(2):
            acc[c] = acc[c] + _dot(xr[:, c * half:(c + 1) * half], w1_ref[r, c])
    hid = _gelu(jnp.concatenate(acc, axis=1) + b1_ref[...]).astype(BF16)
    out = _dot(hid, w2_ref[...]) + b2_ref[...]
    hd = NSA_HEAD_DIM
    kb = out[:, :half].astype(BF16)
    for h in range(NSA_KV_HEADS):
        kc_ref[h] = kb[:, h * hd:(h + 1) * hd]
    vct_ref[...] = out[:, half:].T.astype(BF16)


def _compress_prompt(kv_c, cw, batch, seq):
    rows, w = kv_c.shape
    nc = seq // NSA_BLOCK
    chunks = w // LANES
    kw = NSA_KV_HEADS * NSA_HEAD_DIM
    full = lambda a: pl.BlockSpec(a.shape, lambda b: (0,) * a.ndim)
    return pl.pallas_call(
        _compress_prompt_kernel,
        grid=(batch,),
        in_specs=[pl.BlockSpec((seq, LANES), lambda b, q=q: (b, q)) for q in range(chunks)] + [full(a) for a in cw],
        out_specs=[pl.BlockSpec((None, NSA_KV_HEADS, nc, NSA_HEAD_DIM), lambda b: (b, 0, 0, 0)),
                   pl.BlockSpec((None, kw, nc), lambda b: (b, 0, 0))],
        out_shape=[jax.ShapeDtypeStruct((batch, NSA_KV_HEADS, nc, NSA_HEAD_DIM), BF16),
                   jax.ShapeDtypeStruct((batch, kw, nc), BF16)],
        compiler_params=_cparams("parallel"),
        name="nsa_compress_prompt",
    )(*([kv_c] * chunks), *cw)


def _compress_sample_kernel(pt_ref, *rest, n_pg):
    page_refs = rest[:n_pg]
    pos_ref, w1_ref, b1_ref, w2_ref, b2_ref, kc_ref, vc_ref = rest[n_pg:]
    hd, kvh = NSA_HEAD_DIM, NSA_KV_HEADS
    lanes = page_refs[0].shape[1]
    outs = []
    for c in range(2):
        acc = jnp.zeros((n_pg * kvh, w1_ref.shape[3]), F32)
        for dd in range(hd):
            xr = jnp.concatenate([ref[pl.ds(c * kvh * hd + dd, kvh, stride=hd), :] for ref in page_refs], axis=0)
            xr = (xr + pos_ref[c, dd:dd + 1, :]).astype(BF16)
            acc = acc + _dot(xr, w1_ref[c, dd])
        hid = _gelu(acc + b1_ref[c]).astype(BF16)
        outs.append((_dot(hid, w2_ref[c]) + b2_ref[c]).astype(BF16))
    kc_ref[...] = outs[0].reshape(kc_ref.shape)
    vc_ref[...] = outs[1].reshape(vc_ref.shape)


def _feature_major_pages(cache):
    l, p, page = cache.shape[:3]
    return cache.transpose(0, 1, 3, 4, 5, 2).reshape(l, p, -1, page)


def _compress_sample(cache, layer, page_table, cw):
    db, n_pages = page_table.shape
    page = cache.shape[2]
    kvh, hd = NSA_KV_HEADS, NSA_HEAD_DIM
    bpp = page // NSA_BLOCK
    assert bpp == 2
    cache_t = _feature_major_pages(cache)
    n_pg = 16 if n_pages % 16 == 0 else 1
    full = lambda a: pl.BlockSpec(a.shape, lambda s, j, pt: (0,) * a.ndim)

    def page_spec(u):
        return pl.BlockSpec((None, None, cache_t.shape[2], page), lambda s, j, pt: (layer, pt[s, j * n_pg + u], 0, 0))

    out = jax.ShapeDtypeStruct((db, n_pages, kvh, bpp * hd), BF16)
    ospec = pl.BlockSpec((None, n_pg, kvh, bpp * hd), lambda s, j, pt: (s, j, 0, 0))
    gs = pltpu.PrefetchScalarGridSpec(
        num_scalar_prefetch=1,
        grid=(db, n_pages // n_pg),
        in_specs=[page_spec(u) for u in range(n_pg)] + [full(a) for a in cw],
        out_specs=[ospec, ospec],
    )
    kc, vc = pl.pallas_call(
        functools.partial(_compress_sample_kernel, n_pg=n_pg),
        grid_spec=gs,
        out_shape=[out, out],
        compiler_params=_cparams("parallel", "parallel"),
        name="nsa_compress_sample",
    )(page_table, *([cache_t] * n_pg), *cw)
    to_blocks = lambda a: a.reshape(db, n_pages, kvh, bpp, hd).transpose(0, 2, 1, 3, 4).reshape(db, kvh, -1, hd)
    return to_blocks(kc), to_blocks(vc)


def _block_scores(imp, jblk, cur, nb):
    forced = (jblk == 0) | (jblk == cur) | (jblk == cur - 1)
    score = jnp.where(forced, NSA_FORCE_SCORE, imp)
    return jnp.where((jblk <= cur) & (jblk < nb), score, NEG_INF)


def _cmp_sel_prompt_kernel(qt_ref, kc_ref, vct_ref, oc_ref, sel_ref, *, tq, nb):
    i = pl.program_id(1)
    nc = kc_ref.shape[1]
    nbp = sel_ref.shape[1]
    nr = -(-nb // SUBLANES) * SUBLANES
    hd = NSA_HEAD_DIM
    pos = i * tq + lax.broadcasted_iota(jnp.int32, (nr, tq), 1)
    jblk = lax.broadcasted_iota(jnp.int32, (nr, tq), 0)
    vis = (((lax.broadcasted_iota(jnp.int32, (nc, tq), 0) + 1) * NSA_BLOCK - 1)
           <= i * tq + lax.broadcasted_iota(jnp.int32, (nc, tq), 1))
    cur = pos // NSA_BLOCK
    for k in range(NSA_KV_HEADS):
        kc = kc_ref[k]
        vct = vct_ref[k * hd:(k + 1) * hd, :]
        imp = jnp.zeros((nc, tq), F32)
        for g in range(NSA_GROUP):
            h = k * NSA_GROUP + g
            s = jnp.where(vis, _dot(kc, qt_ref[h * hd:(h + 1) * hd, :]), NEG_INF)
            m = jnp.max(s, axis=0, keepdims=True)
            e = jnp.where(vis, jnp.exp(s - m), 0.0)
            p = e / jnp.maximum(jnp.sum(e, axis=0, keepdims=True), 1e-30)
            imp = imp + p
            oc_ref[h * hd:(h + 1) * hd, :] = _dot(vct, p.astype(BF16)).astype(BF16)
        if nr > nc:
            imp = jnp.concatenate([imp, jnp.zeros((nr - nc, tq), F32)], axis=0)
        score = _block_scores(imp, jblk, cur, nb)
        rank = jnp.zeros((nr, tq), F32)
        for r in range(nb):
            row = score[r:r + 1, :]
            ahead = (row > score) | ((row == score) & (r < jblk))
            rank = rank + jnp.where(ahead, 1.0, 0.0)
        sel = jnp.where((rank < NSA_TOPN) & (score > 0.5 * NEG_INF), 1.0, 0.0)
        if nbp > nr:
            sel = jnp.concatenate([sel, jnp.zeros((nbp - nr, tq), F32)], axis=0)
        sel_ref[k] = sel


def _cmp_sel_prompt(qplt, kc, vct, batch, seq):
    d, rows = qplt.shape
    hd = NSA_HEAD_DIM
    tq = min(256, seq)
    nq = seq // tq
    nc = kc.shape[2]
    nb = -(-seq // NSA_BLOCK)
    nbp = -(-nb // LANES) * LANES
    return pl.pallas_call(
        functools.partial(_cmp_sel_prompt_kernel, tq=tq, nb=nb),
        grid=(batch, nq),
        in_specs=[pl.BlockSpec((d, tq), lambda b, i: (0, b * nq + i)),
                  pl.BlockSpec((None, NSA_KV_HEADS, nc, hd), lambda b, i: (b, 0, 0, 0)),
                  pl.BlockSpec((None,) + vct.shape[1:], lambda b, i: (b, 0, 0))],
        out_specs=[pl.BlockSpec((d, tq), lambda b, i: (0, b * nq + i)),
                   pl.BlockSpec((NSA_KV_HEADS, nbp, tq), lambda b, i: (0, 0, b * nq + i))],
        out_shape=[jax.ShapeDtypeStruct((d, rows), BF16),
                   jax.ShapeDtypeStruct((NSA_KV_HEADS, nbp, rows), F32)],
        compiler_params=_cparams("parallel", "parallel"),
        name="nsa_cmp_select_prompt",
    )(qplt, kc, vct)


def _sel_win_prompt_kernel(qt_ref, ks_ref, vst_ref, kw_ref, vwt_ref, sel_ref, os_ref, ow_ref, *, tq, tk):
    i = pl.program_id(2)
    hd = NSA_HEAD_DIM
    g = NSA_GROUP
    nbp = sel_ref.shape[0]
    tw = vwt_ref.shape[2]
    cps = tk // vst_ref.shape[2]
    q4 = jnp.concatenate([qt_ref[a * hd:(a + 1) * hd, :] for a in range(g)], axis=1)
    flags = sel_ref[...].astype(BF16)
    blk_col = lax.broadcasted_iota(jnp.int32, (tk, nbp), 1)
    blk_of_row = lax.broadcasted_iota(jnp.int32, (tk, nbp), 0) // NSA_BLOCK

    def sel_step(j, carry, diagonal):
        start = pl.multiple_of(j * tk, tk)
        st = _dot(ks_ref[pl.ds(start, tk), :], q4)
        expand = jnp.where(blk_col == blk_of_row + j * (tk // NSA_BLOCK), 1.0, 0.0).astype(BF16)
        chosen = _dot(expand, flags) > 0.5
        if diagonal:
            qpos = i * tq + lax.broadcasted_iota(jnp.int32, (tk, tq), 1)
            chosen = chosen & ((lax.broadcasted_iota(jnp.int32, (tk, tq), 0) + j * tk) <= qpos)
        mask = jnp.concatenate([chosen] * g, axis=1)
        vt = jnp.concatenate([vst_ref[j * cps + u] for u in range(cps)], axis=1)
        return _online_update_t(st, mask, vt, *carry, some_key_visible=True)

    def win_step(j, carry):
        start = pl.multiple_of(j * tw, tw)
        st = _dot(kw_ref[pl.ds(start, tw), :], q4)
        dist = (i * tq + lax.broadcasted_iota(jnp.int32, (tw, tq), 1)
                - (lax.broadcasted_iota(jnp.int32, (tw, tq), 0) + j * tw))
        inside = (dist >= 0) & (dist < NSA_WINDOW)
        return _online_update_t(st, jnp.concatenate([inside] * g, axis=1), vwt_ref[j], *carry)

    def finish(carry, o_ref):
        m, l, acc = carry
        o = (acc * (1.0 / jnp.maximum(l, 1e-30))).astype(BF16)
        for a in range(g):
            o_ref[a * hd:(a + 1) * hd, :] = o[:, a * tq:(a + 1) * tq]

    init = (jnp.full((1, g * tq), NEG_INF, F32), jnp.zeros((1, g * tq), F32), jnp.zeros((hd, g * tq), F32))
    n_full = (i * tq) // tk
    n_kv = ((i + 1) * tq + tk - 1) // tk
    carry = lax.fori_loop(0, n_full, lambda j, c: sel_step(j, c, False), init)
    finish(lax.fori_loop(n_full, n_kv, lambda j, c: sel_step(j, c, True), carry), os_ref)
    first_win = jnp.maximum(i * tq - (NSA_WINDOW - 1), 0) // tw
    finish(lax.fori_loop(first_win, ((i + 1) * tq + tw - 1) // tw, win_step, init), ow_ref)


def _sel_win_prompt(qrtt, ks, vst, kw, vwt, sel, batch, seq):
    d, rows = qrtt.shape
    hd = NSA_HEAD_DIM
    cw = vst.shape[2]
    tq = min(256, seq)
    tk = _key_chunk(seq)
    assert tk % cw == 0 and seq % tk == 0
    nq = seq // tq
    nbp = sel.shape[1]
    gw = NSA_GROUP * hd
    qspec = pl.BlockSpec((gw, tq), lambda b, k, i: (k, b * nq + i))
    kspec = pl.BlockSpec((None, seq, hd), lambda b, k, i: (k, b, 0))
    vspec = pl.BlockSpec((seq // cw, hd, cw), lambda b, k, i: (b, k, 0))
    out = jax.ShapeDtypeStruct((d, rows), BF16)
    return pl.pallas_call(
        functools.partial(_sel_win_prompt_kernel, tq=tq, tk=tk),
        grid=(batch, NSA_KV_HEADS, nq),
        in_specs=[qspec, kspec, vspec, kspec, vspec,
                  pl.BlockSpec((None, nbp, tq), lambda b, k, i: (k, 0, b * nq + i))],
        out_specs=[qspec, qspec],
        out_shape=[out, out],
        compiler_params=_cparams("parallel", "parallel", "parallel"),
        name="nsa_sel_win_prompt",
    )(qrtt, ks, vst, kw, vwt, sel)


def _rows_by_head(q_ref):
    return jnp.concatenate([q_ref[h] for h in range(q_ref.shape[0])], axis=0)


def _cmp_sel_sample_kernel(q_ref, kc_ref, vc_ref, oc_ref, sel_ref, *, ts, past_len):
    nc = kc_ref.shape[1]
    nbp = sel_ref.shape[2]
    nb = -(-(past_len + ts) // NSA_BLOCK)
    gt = NSA_GROUP * ts
    q = _rows_by_head(q_ref)
    pos_g = past_len + lax.broadcasted_iota(jnp.int32, (gt, nc), 0) % ts
    vis = ((lax.broadcasted_iota(jnp.int32, (gt, nc), 1) + 1) * NSA_BLOCK - 1) <= pos_g
    pos = past_len + lax.broadcasted_iota(jnp.int32, (ts, nbp), 0)
    jblk = lax.broadcasted_iota(jnp.int32, (ts, nbp), 1)
    cur = pos // NSA_BLOCK
    for k in range(NSA_KV_HEADS):
        s = jnp.where(vis, _nt_dot(q[k * gt:(k + 1) * gt], kc_ref[k]), NEG_INF)
        m = jnp.max(s, axis=-1, keepdims=True)
        e = jnp.where(vis, jnp.exp(s - m), 0.0)
        p = e / jnp.maximum(jnp.sum(e, axis=-1, keepdims=True), 1e-30)
        o = _dot(p.astype(BF16), vc_ref[k]).astype(BF16)
        imp = p[0:ts]
        for g in range(NSA_GROUP):
            oc_ref[k * NSA_GROUP + g] = o[g * ts:(g + 1) * ts]
            if g:
                imp = imp + p[g * ts:(g + 1) * ts]
        if nbp > nc:
            imp = jnp.concatenate([imp, jnp.zeros((ts, nbp - nc), F32)], axis=1)
        score = _block_scores(imp, jblk, cur, nb)
        rank = jnp.zeros((ts, nbp), F32)
        for r in range(nb):
            col = score[:, r:r + 1]
            ahead = (col > score) | ((col == score) & (r < jblk))
            rank = rank + jnp.where(ahead, 1.0, 0.0)
        sel_ref[k] = jnp.where((rank < NSA_TOPN) & (score > 0.5 * NEG_INF), 1.0, 0.0)


def _cmp_sel_sample(q_pl, kc, vc, past_len):
    db, _, ts, hd = q_pl.shape
    nb = -(-(past_len + ts) // NSA_BLOCK)
    nbp = -(-nb // LANES) * LANES
    seqspec = lambda a: pl.BlockSpec((None,) + a.shape[1:], lambda s: (s, 0, 0, 0))
    return pl.pallas_call(
        functools.partial(_cmp_sel_sample_kernel, ts=ts, past_len=past_len),
        grid=(db,),
        in_specs=[seqspec(q_pl), seqspec(kc), seqspec(vc)],
        out_specs=[pl.BlockSpec((None, NSA_HEADS, ts, hd), lambda s: (s, 0, 0, 0)),
                   pl.BlockSpec((None, NSA_KV_HEADS, ts, nbp), lambda s: (s, 0, 0, 0))],
        out_shape=[jax.ShapeDtypeStruct((db, NSA_HEADS, ts, hd), BF16),
                   jax.ShapeDtypeStruct((db, NSA_KV_HEADS, ts, nbp), F32)],
        compiler_params=_cparams("parallel"),
        name="nsa_cmp_select_sample",
    )(q_pl, kc, vc)


def _new_token_kv(kvn, k, ts, n_rows):
    hd = NSA_HEAD_DIM
    kw = NSA_KV_HEADS * hd
    pad = jnp.zeros((n_rows - ts, hd), F32)
    kn = jnp.concatenate([kvn[:, k * hd:(k + 1) * hd], pad], axis=0).astype(BF16)
    vn = jnp.concatenate([kvn[:, kw + k * hd:kw + (k + 1) * hd], pad], axis=0).astype(BF16)
    return kn, vn


def _softmax_step(s, mask, m, l):
    s = jnp.where(mask, s, NEG_INF)
    m_new = jnp.maximum(m, jnp.max(s, axis=-1, keepdims=True))
    alpha = jnp.exp(m - m_new)
    p = jnp.where(mask, jnp.exp(s - m_new), 0.0)
    return m_new, alpha, alpha * l + jnp.sum(p, axis=-1, keepdims=True), p.astype(BF16)


def _sel_sample_kernel(pt_ref, q_ref, sel_ref, *rest, n_pg, ts, past_len):
    page_refs = rest[:n_pg]
    kvn_ref, o_ref, flag_sc, m_sc, l_sc, acc_sc = rest[n_pg:]
    j = pl.program_id(1)
    rows = NSA_HEADS * ts
    gt = NSA_GROUP * ts
    hd, kvh = NSA_HEAD_DIM, NSA_KV_HEADS
    page = page_refs[0].shape[1]
    nbp = sel_ref.shape[2]

    @pl.when(j == 0)
    def _():
        flag_sc[...] = jnp.concatenate([sel_ref[h // NSA_GROUP] for h in range(NSA_HEADS)], axis=0).astype(BF16)
        m_sc[...] = jnp.full(m_sc.shape, NEG_INF, F32)
        l_sc[...] = jnp.zeros(l_sc.shape, F32)
        acc_sc[...] = jnp.zeros(acc_sc.shape, F32)

    q = _rows_by_head(q_ref)
    flags = flag_sc[...]
    def chosen_keys(first_block, n_keys):
        blk_row = lax.broadcasted_iota(jnp.int32, (nbp, n_keys), 0)
        key_blk = lax.broadcasted_iota(jnp.int32, (nbp, n_keys), 1) // NSA_BLOCK
        expand = jnp.where(blk_row == key_blk + first_block, 1.0, 0.0).astype(BF16)
        return _dot(flags, expand) > 0.5

    s = jnp.concatenate(
        [jnp.concatenate([_dot(q[k * gt:(k + 1) * gt], ref[k * hd:(k + 1) * hd, :].astype(BF16))
                          for k in range(kvh)], axis=0) for ref in page_refs], axis=1)
    mask = chosen_keys(j * n_pg * (page // NSA_BLOCK), n_pg * page)
    m, alpha, l, p = _softmax_step(s, mask, m_sc[...], l_sc[...])
    acc = alpha * acc_sc[...]
    for u, ref in enumerate(page_refs):
        acc = acc + jnp.concatenate(
            [_nt_dot(p[k * gt:(k + 1) * gt, u * page:(u + 1) * page], ref[(kvh + k) * hd:(kvh + k + 1) * hd, :].astype(BF16))
             for k in range(kvh)], axis=0)
    m_sc[...], l_sc[...], acc_sc[...] = m, l, acc

    @pl.when(j == pl.num_programs(1) - 1)
    def _():
        kvn = kvn_ref[...]
        new = [_new_token_kv(kvn, k, ts, page) for k in range(kvh)]
        qi = lax.broadcasted_iota(jnp.int32, (rows, page), 0) % ts
        col = lax.broadcasted_iota(jnp.int32, (rows, page), 1)
        s = jnp.concatenate([_nt_dot(q[k * gt:(k + 1) * gt], new[k][0]) for k in range(kvh)], axis=0)
        mask = chosen_keys(past_len // NSA_BLOCK, page) & (col <= qi)
        mf, alpha, lf, p = _softmax_step(s, mask, m_sc[...], l_sc[...])
        pv = jnp.concatenate([_dot(p[k * gt:(k + 1) * gt], new[k][1]) for k in range(kvh)], axis=0)
        o = ((alpha * acc_sc[...] + pv) * (1.0 / jnp.maximum(lf, 1e-30))).astype(BF16)
        for h in range(NSA_HEADS):
            o_ref[h] = o[h * ts:(h + 1) * ts]


def _sel_sample(q_rt, sel, cache, layer, page_table, kv_new, past_len):
    db, _, ts, hd = q_rt.shape
    n_pages = page_table.shape[1]
    page = cache.shape[2]
    kw = NSA_KV_HEADS * hd
    nbp = sel.shape[3]
    cache_t = _feature_major_pages(cache)
    n_pg = 16 if n_pages % 16 == 0 else 1
    rows = NSA_HEADS * ts

    def page_spec(u):
        return pl.BlockSpec((None, None, 2 * kw, page), lambda s, j, pt: (layer, pt[s, j * n_pg + u], 0, 0))

    gs = pltpu.PrefetchScalarGridSpec(
        num_scalar_prefetch=1,
        grid=(db, n_pages // n_pg),
        in_specs=[pl.BlockSpec((None, NSA_HEADS, ts, hd), lambda s, j, pt: (s, 0, 0, 0)),
                  pl.BlockSpec((None, NSA_KV_HEADS, ts, nbp), lambda s, j, pt: (s, 0, 0, 0))]
                 + [page_spec(u) for u in range(n_pg)]
                 + [pl.BlockSpec((None, ts, 2 * kw), lambda s, j, pt: (s, 0, 0))],
        out_specs=pl.BlockSpec((None, NSA_HEADS, ts, hd), lambda s, j, pt: (s, 0, 0, 0)),
        scratch_shapes=[pltpu.VMEM((rows, nbp), BF16), pltpu.VMEM((rows, 1), F32),
                        pltpu.VMEM((rows, 1), F32), pltpu.VMEM((rows, hd), F32)],
    )
    return pl.pallas_call(
        functools.partial(_sel_sample_kernel, n_pg=n_pg, ts=ts, past_len=past_len),
        grid_spec=gs,
        out_shape=jax.ShapeDtypeStruct((db, NSA_HEADS, ts, hd), BF16),
        compiler_params=_cparams("parallel", "arbitrary"),
        name="nsa_sel_sample",
    )(page_table, q_rt, sel, *([cache_t] * n_pg), kv_new)


def _win_sample_kernel(q_ref, win_ref, kvn_ref, o_ref, *, ts, past_len):
    rows = NSA_HEADS * ts
    gt = NSA_GROUP * ts
    hd, kvh = NSA_HEAD_DIM, NSA_KV_HEADS
    kw = kvh * hd
    w_buf = win_ref.shape[0]
    q = _rows_by_head(q_ref)
    win = win_ref[...].astype(BF16)
    qpos = past_len + lax.broadcasted_iota(jnp.int32, (rows, w_buf), 0) % ts
    kpos = past_len - w_buf + lax.broadcasted_iota(jnp.int32, (rows, w_buf), 1)
    dist = qpos - kpos
    mask = (dist >= 0) & (dist < NSA_WINDOW) & (kpos >= 0)
    m = jnp.full((rows, 1), NEG_INF, F32)
    l = jnp.zeros((rows, 1), F32)
    s = jnp.concatenate([_nt_dot(q[k * gt:(k + 1) * gt], win[:, k * hd:(k + 1) * hd]) for k in range(kvh)], axis=0)
    m, alpha, l, p = _softmax_step(s, mask, m, l)
    acc = jnp.concatenate([_dot(p[k * gt:(k + 1) * gt], win[:, kw + k * hd:kw + (k + 1) * hd])
                           for k in range(kvh)], axis=0)
    new = [_new_token_kv(kvn_ref[...], k, ts, LANES) for k in range(kvh)]
    qi = lax.broadcasted_iota(jnp.int32, (rows, LANES), 0) % ts
    col = lax.broadcasted_iota(jnp.int32, (rows, LANES), 1)
    s = jnp.concatenate([_nt_dot(q[k * gt:(k + 1) * gt], new[k][0]) for k in range(kvh)], axis=0)
    m, alpha, l, p = _softmax_step(s, col <= qi, m, l)
    pv = jnp.concatenate([_dot(p[k * gt:(k + 1) * gt], new[k][1]) for k in range(kvh)], axis=0)
    o = ((alpha * acc + pv) * (1.0 / jnp.maximum(l, 1e-30))).astype(BF16)
    for h in range(NSA_HEADS):
        o_ref[h] = o[h * ts:(h + 1) * ts]


def _win_sample(q_rt, win_state, layer, kv_new, past_len):
    db, _, ts, hd = q_rt.shape
    kw = NSA_KV_HEADS * hd
    w_buf = win_state.shape[2]
    win2 = win_state.reshape(win_state.shape[0], db, w_buf, 2 * kw)
    return pl.pallas_call(
        functools.partial(_win_sample_kernel, ts=ts, past_len=past_len),
        grid=(db,),
        in_specs=[pl.BlockSpec((None, NSA_HEADS, ts, hd), lambda s: (s, 0, 0, 0)),
                  pl.BlockSpec((None, None, w_buf, 2 * kw), lambda s: (layer, s, 0, 0)),
                  pl.BlockSpec((None, ts, 2 * kw), lambda s: (s, 0, 0))],
        out_specs=pl.BlockSpec((None, NSA_HEADS, ts, hd), lambda s: (s, 0, 0, 0)),
        out_shape=jax.ShapeDtypeStruct((db, NSA_HEADS, ts, hd), BF16),
        compiler_params=_cparams("parallel"),
        name="nsa_win_sample",
    )(q_rt, win2, kv_new)


def _nsa_out_kernel(h_ref, oc_ref, os_ref, ow_ref, gate_ref, w_ref, out_ref, o_sc):
    hd = NSA_HEAD_DIM
    for h in range(NSA_HEADS):
        rows = slice(h * hd, (h + 1) * hd)
        o = (gate_ref[3 * h:3 * h + 1, :] * oc_ref[rows, :].astype(F32)
             + gate_ref[3 * h + 1:3 * h + 2, :] * os_ref[rows, :].astype(F32)
             + gate_ref[3 * h + 2:3 * h + 3, :] * ow_ref[rows, :].astype(F32))
        o_sc[rows, :] = o.astype(BF16)
    out_ref[...] = h_ref[...] + _tn_dot(o_sc[...], w_ref[...])


def _nsa_out(h, oct_, ost, owt, gates_t, w_out_bf):
    rows, d = h.shape
    tm = min(ROW_TILE, rows)
    col = lambda i: (0, i)
    ot = pl.BlockSpec((d, tm), col)
    return pl.pallas_call(
        _nsa_out_kernel,
        grid=(rows // tm,),
        in_specs=[pl.BlockSpec((tm, d), lambda i: (i, 0)), ot, ot, ot, pl.BlockSpec((LANES, tm), col),
                  pl.BlockSpec(w_out_bf.shape, lambda i: (0, 0))],
        out_specs=pl.BlockSpec((tm, d), lambda i: (i, 0)),
        out_shape=jax.ShapeDtypeStruct((rows, d), F32),
        scratch_shapes=[pltpu.VMEM((d, tm), BF16)],
        compiler_params=_cparams("parallel"),
        name="nsa_out",
    )(h, oct_, ost, owt, gates_t, w_out_bf)


def _top_desc(s, n):
    vals = []
    cur = s
    for _ in range(n):
        m = jnp.max(cur, axis=0, keepdims=True)
        vals.append(m)
        cur = jnp.where(cur == m, MASKED, cur)
    return jnp.concatenate(vals, axis=0)


def _oddeven_merge_sort_pairs(n):
    def merge(lo, hi, r):
        step = r * 2
        if step < hi - lo:
            yield from merge(lo, hi, step)
            yield from merge(lo + r, hi, step)
            yield from [(i, i + r) for i in range(lo + r, hi - r, step)]
        else:
            yield (lo, lo + r)

    def sort(lo, hi):
        if hi - lo >= 1:
            mid = lo + (hi - lo) // 2
            yield from sort(lo, mid)
            yield from sort(mid + 1, hi)
            yield from merge(lo, hi, 1)

    return list(sort(0, n - 1))


def _top_sorted(s, n):
    x = [s[v * SUBLANES:(v + 1) * SUBLANES, :] for v in range(n)]

    def exchange(i, j):
        x[i], x[j] = jnp.maximum(x[i], x[j]), jnp.minimum(x[i], x[j])

    for i, j in _oddeven_merge_sort_pairs(n):
        exchange(i, j)
    shift = SUBLANES // 2
    while shift:
        other = [pltpu.roll(v, shift, 0) for v in x]
        x = [jnp.maximum(x[v], other[n - 1 - v]) for v in range(n)]
        dist = n // 2
        while dist:
            for i in range(n):
                if not i & dist:
                    exchange(i, i + dist)
            dist //= 2
        shift //= 2
    return jnp.concatenate([v[0:1, :] for v in x], axis=0)


def _peer_score_kernel(h_ref, g_ref, wq_ref, sk_ref, xt_ref, th_ref, f1_ref, e2_ref):
    half = sk_ref.shape[2]
    xn = _rmsnorm_rows(h_ref[...], g_ref[...])
    xt_ref[...] = xn.T.astype(BF16)
    q = _dot(xn.astype(BF16), wq_ref[...])
    sk1, sk2 = sk_ref[0], sk_ref[1]
    kk = PEER_TOPK
    for h in range(PEER_HEADS):
        q1 = q[:, (2 * h) * half:(2 * h + 1) * half].astype(BF16)
        q2 = q[:, (2 * h + 1) * half:(2 * h + 2) * half].astype(BF16)
        s1 = _nt_dot(sk1, q1)
        s2 = _nt_dot(sk2, q2)
        top_of = _top_sorted if s1.shape[0] == kk * SUBLANES else _top_desc
        t1 = top_of(s1, kk)
        t2 = top_of(s2, kk)
        cand = [t1[0:1] + t2]
        for i in range(1, kk // 2):
            cand.append(t1[i:i + 1] + t2[0:kk // 2])
        cand.append(t1[kk // 2:] + t2[0:1])
        n_cand = sum(c.shape[0] for c in cand)
        if top_of is _top_sorted and n_cand <= kk * SUBLANES:
            cand.append(jnp.full((kk * SUBLANES - n_cand, s1.shape[1]), MASKED, F32))
        top = top_of(jnp.concatenate(cand, axis=0), kk)
        tau = top[kk - 1:kk]
        z = jnp.sum(jnp.exp(top - top[0:1]), axis=0, keepdims=True)
        thr = jnp.full(s1.shape, NO_KEEP, F32)
        for j in range(kk):
            t2j = t2[j:j + 1]
            thr = jnp.where((s1 + t2j) >= tau, t2j, thr)
        m2 = t2[0:1]
        th = jnp.exp(jnp.minimum(thr - m2, 1.0))
        f1 = jnp.exp(s1 - t1[0:1]) * (0.5 / z)
        e2 = jnp.exp(s2 - m2)
        for tc in range(s1.shape[1] // LANES):
            lanes = slice(tc * LANES, (tc + 1) * LANES)
            th_ref[h, tc] = th[:, lanes]
            f1_ref[h, tc] = f1[:, lanes]
            e2_ref[h, tc] = e2[:, lanes]


def _peer_scores(h, g, wq_bf, sk_bf, tt):
    rows, d = h.shape
    n_keys = sk_bf.shape[1]
    nt = rows // tt
    tab = jax.ShapeDtypeStruct((PEER_HEADS, rows // LANES, n_keys, LANES), F32)
    tspec = pl.BlockSpec((PEER_HEADS, tt // LANES, n_keys, LANES), lambda i: (0, i, 0, 0))
    return pl.pallas_call(
        _peer_score_kernel,
        grid=(nt,),
        in_specs=[pl.BlockSpec((tt, d), lambda i: (i, 0)), pl.BlockSpec((1, d), lambda i: (0, 0)),
                  pl.BlockSpec(wq_bf.shape, lambda i: (0, 0)), pl.BlockSpec(sk_bf.shape, lambda i: (0, 0, 0))],
        out_specs=[pl.BlockSpec((d, tt), lambda i: (0, i)), tspec, tspec, tspec],
        out_shape=[jax.ShapeDtypeStruct((d, rows), BF16), tab, tab, tab],
        compiler_params=_cparams("parallel"),
        name="peer_scores",
    )(h, g.reshape(1, d), wq_bf, sk_bf)


PEER_A_PER_STEP = 16
PEER_A_PER_SUB = 8
PEER_A_PER_ACC = 2


def _peer_expert_kernel(xt_ref, u_ref, vt_ref, th_ref, f1_ref, e2_ref, h_ref, out_ref, acc_sc):
    c = pl.program_id(1)
    n_keys = e2_ref.shape[2]
    tt = xt_ref.shape[1]
    a_per_step = th_ref.shape[2]
    sub = PEER_A_PER_SUB * n_keys

    @pl.when(c == 0)
    def _():
        acc_sc[...] = jnp.zeros(acc_sc.shape, F32)

    xt = xt_ref[...]
    total = None
    for sc in range(a_per_step // PEER_A_PER_SUB):
        rows = slice(sc * sub, (sc + 1) * sub)
        ux = _dot(u_ref[rows, :], xt)
        act = ux * (1.0 + lax.erf(ux * (2.0 ** -0.5)))
        g_parts = []
        for tc in range(tt // LANES):
            col = []
            for a0 in range(0, PEER_A_PER_SUB, PEER_A_PER_ACC):
                w = [jnp.zeros((n_keys, LANES), F32) for _ in range(PEER_A_PER_ACC)]
                for h in range(PEER_HEADS):
                    e2 = e2_ref[h, tc]
                    for u in range(PEER_A_PER_ACC):
                        a = sc * PEER_A_PER_SUB + a0 + u
                        keep = e2 >= th_ref[h, tc, a:a + 1, :]
                        w[u] = w[u] + jnp.where(keep, f1_ref[h, tc, a:a + 1, :] * e2, 0.0)
                col += w
            g_parts.append(jnp.concatenate(col, axis=0) * act[:, tc * LANES:(tc + 1) * LANES])
        g = jnp.concatenate(g_parts, axis=1).astype(BF16)
        part = _dot(vt_ref[:, rows], g)
        total = part if total is None else total + part
    acc_sc[...] += total

    @pl.when(c == pl.num_programs(1) - 1)
    def _():
        out_ref[...] = h_ref[...] + acc_sc[...].T


def _peer_experts(h, xt, u_bf, vt_bf, layer, th, f1, e2, tt):
    rows, d = h.shape
    n_exp = u_bf.shape[1]
    n_keys = e2.shape[2]
    ec = PEER_A_PER_STEP * n_keys
    e2spec = pl.BlockSpec((PEER_HEADS, tt // LANES, n_keys, LANES), lambda i, c: (0, i, 0, 0))
    aspec = pl.BlockSpec((PEER_HEADS, tt // LANES, PEER_A_PER_STEP, LANES), lambda i, c: (0, i, c, 0))
    return pl.pallas_call(
        _peer_expert_kernel,
        grid=(rows // tt, n_exp // ec),
        in_specs=[pl.BlockSpec((d, tt), lambda i, c: (0, i)),
                  pl.BlockSpec((None, ec, d), lambda i, c: (layer, c, 0)),
                  pl.BlockSpec((None, d, ec), lambda i, c: (layer, 0, c)),
                  aspec, aspec, e2spec,
                  pl.BlockSpec((tt, d), lambda i, c: (i, 0))],
        out_specs=pl.BlockSpec((tt, d), lambda i, c: (i, 0)),
        out_shape=jax.ShapeDtypeStruct((rows, d), F32),
        scratch_shapes=[pltpu.VMEM((d, tt), F32)],
        compiler_params=_cparams("parallel", "arbitrary"),
        name="peer_experts",
    )(xt, u_bf, vt_bf, th, f1, e2, h)


def _peer(h, g, wq_bf, sk_bf, u_bf, vt_bf, layer):
    rows = h.shape[0]
    tt = min(ROW_TILE, rows)
    xt, th, f1, e2 = _peer_scores(h, g, wq_bf, sk_bf, tt)
    return _peer_experts(h, xt, u_bf, vt_bf, layer, th, f1, e2, tt)


def _final_norm_kernel(h_ref, g_ref, o_ref):
    o_ref[...] = _rmsnorm_rows(h_ref[...], g_ref[...])


def _final_norm(h, g):
    rows, d = h.shape
    tm = min(ROW_TILE, rows)
    return pl.pallas_call(
        _final_norm_kernel,
        grid=(rows // tm,),
        in_specs=[pl.BlockSpec((tm, d), lambda i: (i, 0)), pl.BlockSpec((1, d), lambda i: (0, 0))],
        out_specs=pl.BlockSpec((tm, d), lambda i: (i, 0)),
        out_shape=jax.ShapeDtypeStruct((rows, d), F32),
        compiler_params=_cparams("parallel"),
        name="final_norm",
    )(h, g.reshape(1, d))


def _heads_from_t(xt, db, ts, hd):
    return xt.reshape(-1, hd, db, ts).transpose(2, 0, 3, 1)


def _heads_to_t(x):
    db, heads, ts, hd = x.shape
    return x.transpose(1, 3, 0, 2).reshape(heads * hd, db * ts)


def kernel(x_prompt, x_sample, cache_diff_kv, cache_nsa_cmp_kv, cache_nsa_sel_kv, state_nsa_win_kv, page_table, norm_mix_g, diff_w_in, diff_lambda, diff_subln_g, diff_w_out, nsa_w_in, nsa_cmp_pos, nsa_cmp_w1, nsa_cmp_b1, nsa_cmp_w2, nsa_cmp_b2, nsa_w_out, norm_ffn_g, peer_wq, peer_subkeys, peer_u, peer_v, final_norm_g):
    batch, seq, d = x_prompt.shape
    db, ts, _ = x_sample.shape
    depth = norm_mix_g.shape[0]
    past_len = page_table.shape[1] * cache_diff_kv.shape[2]
    assert past_len % NSA_BLOCK == 0 and ts < NSA_BLOCK and seq % LANES == 0 and (db * ts) % LANES == 0
    assert state_nsa_win_kv.shape[2] == min(NSA_WINDOW, past_len)

    pos_p = jnp.arange(seq)
    pos_s = jnp.tile(past_len + jnp.arange(ts), db)
    tab_p = _rope_tables(pos_p, DA_HEAD_DIM)
    tab_s = _rope_tables(pos_s, DA_HEAD_DIM)

    hp = x_prompt.reshape(batch * seq, d)
    hs = x_sample.reshape(db * ts, d)
    outs = {k: [] for k in ("diff_p", "diff_s", "cmp_p", "cmp_s", "sel_p", "sel_s", "win_p", "win_s")}
    u_bf = peer_u.astype(BF16)
    vt_bf = peer_v.transpose(0, 2, 1).astype(BF16)

    for i in range(depth):
        g_mix = norm_mix_g[i]
        if i % 2 == 0:
            a = i // 2
            lam_init = 0.8 - 0.6 * math.exp(-0.3 * i)
            w_in = diff_w_in[a].astype(BF16)
            w_out = diff_w_out[a].astype(BF16)
            qt, kvp, kb, vtc = _diff_proj(hp, g_mix, w_in, tab_p, seq)
            ot = _diff_attn_prompt(qt, kb, vtc, diff_lambda[a], diff_subln_g[a], batch, seq, lam_init)
            hp = _outproj(hp, ot, w_out)
            qt_s, kvs, _, _ = _diff_proj(hs, g_mix, w_in, tab_s, db * ts)
            os_ = _diff_attn_sample(qt_s.T.reshape(db, ts, d), cache_diff_kv, a, page_table,
                                    kvs.reshape(db, ts, 2 * d), diff_lambda[a], diff_subln_g[a], lam_init)
            hs = _outproj(hs, os_.reshape(db * ts, d).T, w_out)
            outs["diff_p"].append(kvp.reshape(batch, seq, 2, DA_HEADS, 2 * DA_HEAD_DIM))
            outs["diff_s"].append(kvs.reshape(db, ts, 2, DA_HEADS, 2 * DA_HEAD_DIM))
        else:
            b = i // 2
            hd = NSA_HEAD_DIM
            n_main = d + 3 * 2 * NSA_KV_HEADS * hd
            w_main = nsa_w_in[b][:, :n_main].astype(BF16)
            n_gate = nsa_w_in.shape[2] - n_main
            w_gate = jnp.pad(nsa_w_in[b][:, n_main:], ((0, 0), (0, LANES - n_gate))).astype(BF16)
            w_out = nsa_w_out[b].astype(BF16)
            cw_tok, cw_feat = _compress_weights(nsa_cmp_pos[b], nsa_cmp_w1[b], nsa_cmp_b1[b],
                                                nsa_cmp_w2[b], nsa_cmp_b2[b])
            kv_shape = (2, NSA_KV_HEADS, hd)
            (qplt, qrtt, kvc, kvs_, kvw, ks, vst, kw, vwt, gates_t) = _nsa_proj(hp, g_mix, w_main, w_gate, tab_p, seq)
            kc, vct = _compress_prompt(kvc, cw_tok, batch, seq)
            oct_, sel = _cmp_sel_prompt(qplt, kc, vct, batch, seq)
            ost, owt = _sel_win_prompt(qrtt, ks, vst, kw, vwt, sel, batch, seq)
            hp = _nsa_out(hp, oct_, ost, owt, gates_t, w_out)
            outs["cmp_p"].append(kvc.reshape((batch, seq) + kv_shape))
            outs["sel_p"].append(kvs_.reshape((batch, seq) + kv_shape))
            w_keep = min(NSA_WINDOW, seq)
            outs["win_p"].append(kvw.reshape((batch, seq) + kv_shape)[:, seq - w_keep:])
            (qplt, qrtt, kvc, kvs_, kvw, _, _, _, _, gates_t) = _nsa_proj(hs, g_mix, w_main, w_gate, tab_s, db * ts)
            kc, vc = _compress_sample(cache_nsa_cmp_kv, b, page_table, cw_feat)
            o_c, sel = _cmp_sel_sample(_heads_from_t(qplt, db, ts, hd), kc, vc, past_len)
            qrt_sm = _heads_from_t(qrtt, db, ts, hd)
            o_s = _sel_sample(qrt_sm, sel, cache_nsa_sel_kv, b, page_table, kvs_.reshape(db, ts, -1), past_len)
            o_w = _win_sample(qrt_sm, state_nsa_win_kv, b, kvw.reshape(db, ts, -1), past_len)
            hs = _nsa_out(hs, _heads_to_t(o_c), _heads_to_t(o_s), _heads_to_t(o_w), gates_t, w_out)
            outs["cmp_s"].append(kvc.reshape((db, ts) + kv_shape))
            outs["sel_s"].append(kvs_.reshape((db, ts) + kv_shape))
            win_all = jnp.concatenate([state_nsa_win_kv[b], kvw.reshape((db, ts) + kv_shape)], axis=1)
            outs["win_s"].append(win_all[:, win_all.shape[1] - state_nsa_win_kv.shape[2]:])
        g_ffn = norm_ffn_g[i]
        wq = peer_wq[i].astype(BF16)
        sk = peer_subkeys[i].astype(BF16)
        hp = _peer(hp, g_ffn, wq, sk, u_bf, vt_bf, i)
        hs = _peer(hs, g_ffn, wq, sk, u_bf, vt_bf, i)

    y_prompt = _final_norm(hp, final_norm_g).reshape(batch, seq, d)
    y_sample = _final_norm(hs, final_norm_g).reshape(db, ts, d)
    stack = lambda k: jnp.stack(outs[k])
    return (y_prompt, y_sample, stack("diff_p"), stack("diff_s"), stack("cmp_p"), stack("cmp_s"),
            stack("sel_p"), stack("sel_s"), stack("win_p"), stack("win_s"))
```
